```python
import math
import jax
import jax.numpy as jnp
from jax import lax
import numpy as np

D_MODEL = 1024
BATCH = 32
SEQ = 256
DEPTH = 2
DEC_BATCH = 8
DEC_SEQ = 1024
PAST_LEN = 512

GRID_W = 64
HEAD_DIM = 64
ROPE_PAIRS = HEAD_DIM // 4
ROPE_BASE = 10000.0
BLOCK = 128
EPS = 1e-6
NEG_INF = -1e30
D_FF = 2816
N_MOD = 9
N_BRANCH = 4
BRANCH_W = 512

HY_W = BRANCH_W
HY_SHORT = 3
HY_BANDS = 16
HY_EMB = 1 + 2 * HY_BANDS
HY_FH = 64
HY_SIN_W = 1.0
HY_TARGET = 1e-2
HY_DECAY_SHORT_PCT = 0.3
HY_DECAY_LONG_PCT = 1.5

DIFF_HEADS = 4
DIFF_QK_W = DIFF_HEADS * 2 * HEAD_DIM
DIFF_V_W = DIFF_HEADS * 2 * HEAD_DIM

WIN_Q_HEADS = 8
WIN_KV_HEADS = 2
WIN_GROUP = WIN_Q_HEADS // WIN_KV_HEADS
WINDOW = 128
WIN_Q_W = WIN_Q_HEADS * HEAD_DIM
WIN_KV_W = WIN_KV_HEADS * HEAD_DIM

RET_HEADS = 4
RET_DK = 64
RET_DV = 128
RET_CHUNK = 128
RET_QK_W = RET_HEADS * RET_DK
RET_V_W = RET_HEADS * RET_DV

COL_SIZES = (3 * HY_W, DIFF_QK_W, DIFF_QK_W, DIFF_V_W, WIN_Q_W, WIN_KV_W, WIN_KV_W,
             RET_QK_W, RET_QK_W, RET_V_W, RET_V_W, N_BRANCH * D_MODEL)
IN_COLS = sum(COL_SIZES)

kernel_name = 'hybrid_diffusion_prefix_trunk_step'


def rms_norm(x, gain=None):
    xf = x.astype(jnp.float32)
    y = xf * lax.rsqrt(jnp.mean(xf * xf, axis=-1, keepdims=True) + EPS)
    if gain is not None:
        y = y * gain.astype(jnp.float32)
    return y.astype(x.dtype)


def ada_mod(cond, w, b):
    return jax.nn.silu(cond) @ w + b


def modulate(h, shift, scale):
    return h * (1.0 + scale) + shift


def swiglu(h, w_in, w_out):
    a, g = jnp.split(h @ w_in, 2, axis=-1)
    return (jax.nn.silu(a) * g) @ w_out


def axial_rope(x):
    n = x.shape[1]
    rows = n // GRID_W
    row = jnp.broadcast_to(jnp.arange(rows)[:, None], (rows, GRID_W)).reshape(n)
    col = jnp.broadcast_to(jnp.arange(GRID_W)[None, :], (rows, GRID_W)).reshape(n)
    inv_freq = ROPE_BASE ** (-jnp.arange(ROPE_PAIRS, dtype=jnp.float32) / ROPE_PAIRS)
    bshape = (n,) + (1,) * (x.ndim - 3) + (ROPE_PAIRS,)

    def rotate(xa, pos):
        ang = (pos.astype(jnp.float32)[:, None] * inv_freq[None, :]).reshape(bshape)
        cos, sin = jnp.cos(ang).astype(x.dtype), jnp.sin(ang).astype(x.dtype)
        x1, x2 = xa[..., :ROPE_PAIRS], xa[..., ROPE_PAIRS:]
        return jnp.concatenate([x1 * cos - x2 * sin, x1 * sin + x2 * cos], axis=-1)

    half = HEAD_DIM // 2
    return jnp.concatenate([rotate(x[..., :half], row), rotate(x[..., half:], col)], axis=-1)


def sweep_query_blocks(fn, q):
    B, L = q.shape[:2]
    nb = L // BLOCK
    qb = jnp.moveaxis(q.reshape((B, nb, BLOCK) + q.shape[2:]), 1, 0)
    out = lax.map(fn, qb)
    return jnp.moveaxis(out, 0, 1).reshape((B, L) + out.shape[3:])


def diff_block(qb, k, v, lam):
    s = jnp.einsum('bqhcd,bkhcd->bhcqk', qb, k).astype(jnp.float32) * (HEAD_DIM ** -0.5)
    p = jax.nn.softmax(s, axis=-1)
    a = p[:, :, 0] - lam * p[:, :, 1]
    return jnp.einsum('bhqk,bkhe->bqhe', a.astype(v.dtype), v)


def window_dense_block(qb, k, v, sink):
    B, Q = qb.shape[:2]
    s = jnp.einsum('bqhgd,bkhd->bhgqk', qb, k).astype(jnp.float32) * (HEAD_DIM ** -0.5)
    s_sink = jnp.broadcast_to(sink.astype(jnp.float32).reshape(WIN_KV_HEADS, WIN_GROUP)[None, :, :, None, None],
                              (B, WIN_KV_HEADS, WIN_GROUP, Q, 1))
    p = jax.nn.softmax(jnp.concatenate([s, s_sink], axis=-1), axis=-1)[..., :-1]
    return jnp.einsum('bhgqk,bkhd->bqhgd', p.astype(v.dtype), v)


def window_latent(q, k, v, k_ctx, v_ctx, sink):
    B, L = q.shape[:2]
    nb = L // BLOCK
    C = k_ctx.shape[1]
    qb = q.reshape(B, nb, BLOCK, WIN_KV_HEADS, WIN_GROUP, HEAD_DIM)

    def band(x):
        xp = jnp.pad(x, ((0, 0), (BLOCK, BLOCK), (0, 0), (0, 0))).reshape(B, nb + 2, BLOCK, WIN_KV_HEADS, HEAD_DIM)
        return jnp.concatenate([xp[:, :-2], xp[:, 1:-1], xp[:, 2:]], axis=2)

    kb, vb = band(k), band(v)
    blk = jnp.arange(nb)
    tq = blk[:, None] * BLOCK + jnp.arange(BLOCK)[None, :]
    tk = (blk[:, None] - 1) * BLOCK + jnp.arange(3 * BLOCK)[None, :]
    rel = tk[:, None, :] - tq[:, :, None]
    valid = (jnp.abs(rel) <= WINDOW) & (tk[:, None, :] >= 0) & (tk[:, None, :] < L)
    scale = HEAD_DIM ** -0.5
    s_band = jnp.einsum('bnqhgd,bnkhd->bnhgqk', qb, kb).astype(jnp.float32) * scale
    s_band = jnp.where(valid[None, :, None, None], s_band, NEG_INF)
    s_ctx = jnp.einsum('bnqhgd,bchd->bnhgqc', qb, k_ctx).astype(jnp.float32) * scale
    s_sink = jnp.broadcast_to(sink.astype(jnp.float32).reshape(WIN_KV_HEADS, WIN_GROUP)[None, None, :, :, None, None],
                              (B, nb, WIN_KV_HEADS, WIN_GROUP, BLOCK, 1))
    p = jax.nn.softmax(jnp.concatenate([s_band, s_ctx, s_sink], axis=-1), axis=-1)
    p_band = p[..., :3 * BLOCK].astype(v.dtype)
    p_ctx = p[..., 3 * BLOCK:3 * BLOCK + C].astype(v.dtype)
    o = (jnp.einsum('bnhgqk,bnkhd->bnqhgd', p_band, vb)
         + jnp.einsum('bnhgqc,bchd->bnqhgd', p_ctx, v_ctx))
    return o.reshape(B, L, WIN_KV_HEADS, WIN_GROUP, HEAD_DIM)


def retention_scan(q, k, v, log_gamma, s0):
    B, L, H, _ = q.shape
    nc = L // RET_CHUNK
    f32 = jnp.float32

    def chunks(x):
        return jnp.moveaxis(x.astype(f32).reshape((B, nc, RET_CHUNK) + x.shape[2:]), 1, 0)

    lg = log_gamma.astype(f32)
    idx = jnp.arange(RET_CHUNK, dtype=f32)
    rel = idx[:, None] - idx[None, :]
    intra = jnp.where(rel[None] >= 0, jnp.exp(jnp.maximum(rel, 0.0)[None] * lg[:, None, None]), 0.0)
    q_dec = jnp.exp((idx + 1.0)[:, None] * lg[None, :])[..., None]
    k_dec = jnp.exp((RET_CHUNK - 1.0 - idx)[:, None] * lg[None, :])[..., None]
    chunk_dec = jnp.exp(RET_CHUNK * lg)[None, :, None, None]

    def step(state, inp):
        qc, kc, vc = inp
        scores = jnp.einsum('bqhd,bkhd->bhqk', qc, kc) * intra
        out = (jnp.einsum('bhqk,bkhe->bqhe', scores, vc)
               + jnp.einsum('bqhd,bhde->bqhe', qc * q_dec, state))
        state = state * chunk_dec + jnp.einsum('bkhd,bkhe->bhde', kc * k_dec, vc)
        return state, out

    s_final, out = lax.scan(step, s0.astype(f32), (chunks(q), chunks(k), chunks(v)))
    return jnp.moveaxis(out, 0, 1).reshape(B, L, H, v.shape[-1]), s_final


def retention_bidirectional(q, k, v, lg_f, lg_b, s0_f, s0_b):
    o_f, s_f = retention_scan(q, k, v, lg_f, s0_f)
    o_b, s_b = retention_scan(jnp.flip(q, 1), jnp.flip(k, 1), jnp.flip(v, 1), lg_b, s0_b)
    return o_f + jnp.flip(o_b, 1), s_f, s_b


def short_conv(u, w, b):
    up = jnp.pad(u, ((0, 0), (1, 1), (0, 0)))
    return up[:, :-2] * w[0] + up[:, 1:-1] * w[1] + up[:, 2:] * w[2] + b


def hyena_filters(L, w1, b1, w2, b2, w3):
    t = jnp.arange(L, dtype=jnp.float32) / L
    f = jnp.arange(1, HY_BANDS + 1, dtype=jnp.float32)
    ang = 2.0 * math.pi * t[:, None] * f[None, :]
    feats = jnp.concatenate([t[:, None], jnp.sin(ang), jnp.cos(ang)], axis=-1)
    z = jnp.sin(HY_SIN_W * (feats @ w1 + b1))
    z = jnp.sin(HY_SIN_W * (z @ w2 + b2))
    z = (z @ w3).astype(jnp.float32)
    min_decay = math.log(HY_TARGET) / HY_DECAY_LONG_PCT
    max_decay = math.log(HY_TARGET) / HY_DECAY_SHORT_PCT
    deltas = jnp.linspace(min_decay, max_decay, HY_W, dtype=jnp.float32)
    window = jnp.exp(-t[:, None] * jnp.abs(deltas)[None, :])
    h_f = z[:, :HY_W] * window
    h_b = z[:, HY_W:] * window
    norm = jnp.sum(jnp.abs(h_f), axis=0, keepdims=True) + jnp.sum(jnp.abs(h_b), axis=0, keepdims=True)
    return h_f / norm, h_b / norm


def long_conv(u, h_f, h_b, skip):
    L = u.shape[1]
    k = jnp.concatenate([h_f, jnp.zeros_like(h_f[:1]), h_b[1:][::-1]], axis=0)
    k_f = jnp.fft.rfft(k, n=2 * L, axis=0)
    u_f = jnp.fft.rfft(u.astype(jnp.float32), n=2 * L, axis=1)
    y = jnp.fft.irfft(u_f * k_f[None], n=2 * L, axis=1)[:, :L]
    return (y + u.astype(jnp.float32) * skip.astype(jnp.float32)).astype(u.dtype)


def hyena_branch(cols, lp):
    L = cols.shape[1]
    u = short_conv(cols, lp['hy_conv_w'], lp['hy_conv_b'])
    v, x0, x1 = jnp.split(u, 3, axis=-1)
    h_f, h_b = hyena_filters(L, lp['hy_f_w1'], lp['hy_f_b1'], lp['hy_f_w2'], lp['hy_f_b2'], lp['hy_f_w3'])
    return x0 * long_conv(v * x1, h_f, h_b, lp['hy_skip'])


def mix(h, lp, layer_idx, cache):
    B, L, _ = h.shape
    f32 = jnp.float32
    split_points = np.cumsum(COL_SIZES)[:-1].tolist()
    (hy, dq, dk, dv, wq, wk, wv, rq, rk, rv, rg, gl) = jnp.split(h @ lp['w_in'], split_points, axis=-1)

    y_hy = hyena_branch(hy, lp)

    q_d = rms_norm(dq.reshape(B, L, DIFF_HEADS, 2, HEAD_DIM), lp['diff_q_norm'])
    k_d = rms_norm(dk.reshape(B, L, DIFF_HEADS, 2, HEAD_DIM), lp['diff_k_norm'])
    v_d = dv.reshape(B, L, DIFF_HEADS, 2 * HEAD_DIM)
    lam_init = 0.8 - 0.6 * math.exp(-0.3 * layer_idx)
    dl = lp['diff_lambda'].astype(f32)
    lam = jnp.exp(jnp.sum(dl[0] * dl[1])) - jnp.exp(jnp.sum(dl[2] * dl[3])) + lam_init

    q_w = rms_norm(wq.reshape(B, L, WIN_KV_HEADS, WIN_GROUP, HEAD_DIM), lp['win_q_norm'])
    k_w = rms_norm(wk.reshape(B, L, WIN_KV_HEADS, HEAD_DIM), lp['win_k_norm'])
    v_w = wv.reshape(B, L, WIN_KV_HEADS, HEAD_DIM)

    q_r = rq.reshape(B, L, RET_HEADS, RET_DK)
    k_r = rk.reshape(B, L, RET_HEADS, RET_DK) * (RET_DK ** -0.5)
    v_r = rv.reshape(B, L, RET_HEADS, RET_DV)
    lg_f = -jax.nn.softplus(-lp['ret_decay_f'].astype(f32))
    lg_b = -jax.nn.softplus(-lp['ret_decay_b'].astype(f32))

    if cache is None:
        s0 = jnp.zeros((B, RET_HEADS, RET_DK, RET_DV), f32)
        y_diff = sweep_query_blocks(lambda qb: diff_block(qb, k_d, v_d, lam), q_d)
        y_win = sweep_query_blocks(lambda qb: window_dense_block(qb, k_w, v_w, lp['win_sink']), q_w)
        y_ret, s_f, s_b = retention_bidirectional(q_r, k_r, v_r, lg_f, lg_b, s0, s0)
        state = (k_d, v_d, k_w, v_w, s_f, s_b)
    else:
        ck_d, cv_d, ck_w, cv_w, cs_f, cs_b = cache
        q_dr, k_dr = axial_rope(q_d), axial_rope(k_d)
        keys_d = jnp.concatenate([k_dr, ck_d.astype(k_dr.dtype)], axis=1)
        vals_d = jnp.concatenate([v_d, cv_d.astype(v_d.dtype)], axis=1)
        y_diff = sweep_query_blocks(lambda qb: diff_block(qb, keys_d, vals_d, lam), q_dr)
        y_win = window_latent(axial_rope(q_w), axial_rope(k_w), v_w,
                              ck_w.astype(k_w.dtype), cv_w.astype(v_w.dtype), lp['win_sink'])
        y_ret, _, _ = retention_bidirectional(q_r, k_r, v_r, lg_f, lg_b, cs_f, cs_b)
        state = None

    y_diff = (rms_norm(y_diff, lp['diff_subln']) * (1.0 - lam_init)).reshape(B, L, DIFF_V_W)
    y_win = y_win.reshape(B, L, WIN_Q_W)
    y_ret = rms_norm(y_ret).reshape(B, L, RET_V_W).astype(h.dtype) * jax.nn.silu(rg)

    gates = jax.nn.sigmoid(gl.reshape(B, L, N_BRANCH, D_MODEL))
    branches = (y_hy, y_diff, y_win, y_ret)
    merged = gates[:, :, 0] * (branches[0] @ lp['w_branch'][0])
    for i in range(1, N_BRANCH):
        merged = merged + gates[:, :, i] * (branches[i] @ lp['w_branch'][i])
    return merged @ lp['w_out'], state


def trunk_layer(x, mod, lp, layer_idx, cache):
    sa, ca, ga, sm, cm, gm, sb, cb, gb = jnp.split(mod[:, None, :], N_MOD, axis=-1)
    h = modulate(rms_norm(x, lp['norm_ffa']), sa, ca)
    x = x + 0.5 * ga * swiglu(h, lp['w_ffa_in'], lp['w_ffa_out'])
    h = modulate(rms_norm(x, lp['norm_mix']), sm, cm)
    mixed, state = mix(h, lp, layer_idx, cache)
    x = x + gm * mixed
    h = modulate(rms_norm(x, lp['norm_ffb']), sb, cb)
    x = x + 0.5 * gb * swiglu(h, lp['w_ffb_in'], lp['w_ffb_out'])
    return x, state


def setup_inputs(seed: int = 0) -> dict:
    key = jax.random.key(seed)
    ks = iter(jax.random.split(key, 48))

    def nrm(shape, scale):
        return jax.random.normal(next(ks), shape, jnp.float32) * scale

    def gain(shape):
        return 1.0 + nrm(shape, 0.02)

    ret_base = jnp.log(2.0 ** (5.0 + jnp.arange(RET_HEADS, dtype=jnp.float32)) - 1.0)
    inp = {}
    inp['x_prompt'] = nrm((BATCH, SEQ, D_MODEL), 1.0)
    inp['x_sample'] = nrm((DEC_BATCH, DEC_SEQ, D_MODEL), 1.0)
    inp['c'] = nrm((DEC_BATCH, D_MODEL), 1.0)
    inp['cache_diff_k'] = nrm((DEC_BATCH, DEPTH, PAST_LEN, DIFF_HEADS, 2, HEAD_DIM), 1.0)
    inp['cache_diff_v'] = nrm((DEC_BATCH, DEPTH, PAST_LEN, DIFF_HEADS, 2 * HEAD_DIM), 1.0)
    inp['cache_win_k'] = nrm((DEC_BATCH, DEPTH, PAST_LEN, WIN_KV_HEADS, HEAD_DIM), 1.0)
    inp['cache_win_v'] = nrm((DEC_BATCH, DEPTH, PAST_LEN, WIN_KV_HEADS, HEAD_DIM), 1.0)
    inp['state_ret_f'] = nrm((DEC_BATCH, DEPTH, RET_HEADS, RET_DK, RET_DV), 0.5)
    inp['state_ret_b'] = nrm((DEC_BATCH, DEPTH, RET_HEADS, RET_DK, RET_DV), 0.5)
    inp['c_ctx'] = nrm((D_MODEL,), 1.0)
    inp['norm_ffa'] = gain((DEPTH, D_MODEL))
    inp['norm_mix'] = gain((DEPTH, D_MODEL))
    inp['norm_ffb'] = gain((DEPTH, D_MODEL))
    inp['w_ada'] = nrm((DEPTH, D_MODEL, N_MOD * D_MODEL), 0.5 * D_MODEL ** -0.5)
    inp['b_ada'] = nrm((DEPTH, N_MOD * D_MODEL), 0.02)
    inp['w_ffa_in'] = nrm((DEPTH, D_MODEL, 2 * D_FF), D_MODEL ** -0.5)
    inp['w_ffa_out'] = nrm((DEPTH, D_FF, D_MODEL), D_FF ** -0.5)
    inp['w_ffb_in'] = nrm((DEPTH, D_MODEL, 2 * D_FF), D_MODEL ** -0.5)
    inp['w_ffb_out'] = nrm((DEPTH, D_FF, D_MODEL), D_FF ** -0.5)
    inp['w_in'] = nrm((DEPTH, D_MODEL, IN_COLS), D_MODEL ** -0.5)
    inp['hy_conv_w'] = nrm((DEPTH, HY_SHORT, 3 * HY_W), HY_SHORT ** -0.5)
    inp['hy_conv_b'] = nrm((DEPTH, 3 * HY_W), 0.02)
    inp['hy_f_w1'] = nrm((DEPTH, HY_EMB, HY_FH), HY_EMB ** -0.5)
    inp['hy_f_b1'] = nrm((DEPTH, HY_FH), 0.02)
    inp['hy_f_w2'] = nrm((DEPTH, HY_FH, HY_FH), HY_FH ** -0.5)
    inp['hy_f_b2'] = nrm((DEPTH, HY_FH), 0.02)
    inp['hy_f_w3'] = nrm((DEPTH, HY_FH, 2 * HY_W), HY_FH ** -0.5)
    inp['hy_skip'] = nrm((DEPTH, HY_W), 0.5)
    inp['diff_q_norm'] = gain((DEPTH, HEAD_DIM))
    inp['diff_k_norm'] = gain((DEPTH, HEAD_DIM))
    inp['diff_lambda'] = nrm((DEPTH, 4, HEAD_DIM), 0.1)
    inp['diff_subln'] = gain((DEPTH, 2 * HEAD_DIM))
    inp['win_q_norm'] = gain((DEPTH, HEAD_DIM))
    inp['win_k_norm'] = gain((DEPTH, HEAD_DIM))
    inp['win_sink'] = nrm((DEPTH, WIN_Q_HEADS), 0.5)
    inp['ret_decay_f'] = ret_base[None, :] + nrm((DEPTH, RET_HEADS), 0.1)
    inp['ret_decay_b'] = ret_base[None, :] + nrm((DEPTH, RET_HEADS), 0.1)
    inp['w_branch'] = nrm((DEPTH, N_BRANCH, BRANCH_W, D_MODEL), BRANCH_W ** -0.5)
    inp['w_out'] = nrm((DEPTH, D_MODEL, D_MODEL), D_MODEL ** -0.5)
    return inp


def reference(x_prompt, x_sample, c, cache_diff_k, cache_diff_v, cache_win_k, cache_win_v,
              state_ret_f, state_ret_b, c_ctx, norm_ffa, norm_mix, norm_ffb, w_ada, b_ada,
              w_ffa_in, w_ffa_out, w_ffb_in, w_ffb_out, w_in, hy_conv_w, hy_conv_b,
              hy_f_w1, hy_f_b1, hy_f_w2, hy_f_b2, hy_f_w3, hy_skip,
              diff_q_norm, diff_k_norm, diff_lambda, diff_subln,
              win_q_norm, win_k_norm, win_sink, ret_decay_f, ret_decay_b, w_branch, w_out):
    y_ctx = x_prompt
    y_lat = x_sample
    st_dk, st_dv, st_wk, st_wv, st_rf, st_rb = [], [], [], [], [], []
    for l in range(DEPTH):
        lp = {
            'norm_ffa': norm_ffa[l], 'norm_mix': norm_mix[l], 'norm_ffb': norm_ffb[l],
            'w_ffa_in': w_ffa_in[l], 'w_ffa_out': w_ffa_out[l],
            'w_ffb_in': w_ffb_in[l], 'w_ffb_out': w_ffb_out[l],
            'w_in': w_in[l], 'hy_conv_w': hy_conv_w[l], 'hy_conv_b': hy_conv_b[l],
            'hy_f_w1': hy_f_w1[l], 'hy_f_b1': hy_f_b1[l], 'hy_f_w2': hy_f_w2[l], 'hy_f_b2': hy_f_b2[l],
            'hy_f_w3': hy_f_w3[l], 'hy_skip': hy_skip[l],
            'diff_q_norm': diff_q_norm[l], 'diff_k_norm': diff_k_norm[l],
            'diff_lambda': diff_lambda[l], 'diff_subln': diff_subln[l],
            'win_q_norm': win_q_norm[l], 'win_k_norm': win_k_norm[l], 'win_sink': win_sink[l],
            'ret_decay_f': ret_decay_f[l], 'ret_decay_b': ret_decay_b[l],
            'w_branch': w_branch[l], 'w_out': w_out[l],
        }
        mod_ctx = ada_mod(c_ctx[None, :], w_ada[l], b_ada[l])
        y_ctx, ctx_state = trunk_layer(y_ctx, mod_ctx, lp, l, None)
        st_dk.append(ctx_state[0])
        st_dv.append(ctx_state[1])
        st_wk.append(ctx_state[2])
        st_wv.append(ctx_state[3])
        st_rf.append(ctx_state[4])
        st_rb.append(ctx_state[5])
        mod_lat = ada_mod(c, w_ada[l], b_ada[l])
        cache = (cache_diff_k[:, l], cache_diff_v[:, l], cache_win_k[:, l], cache_win_v[:, l],
                 state_ret_f[:, l], state_ret_b[:, l])
        y_lat, _ = trunk_layer(y_lat, mod_lat, lp, l, cache)
    new_diff_k = jnp.stack(st_dk, axis=1)
    new_diff_v = jnp.stack(st_dv, axis=1)
    new_win_k = jnp.stack(st_wk, axis=1)
    new_win_v = jnp.stack(st_wv, axis=1)
    new_ret_f = jnp.stack(st_rf, axis=1)
    new_ret_b = jnp.stack(st_rb, axis=1)
    return (y_ctx, y_lat, new_diff_k, new_diff_v, new_win_k, new_win_v, new_ret_f, new_ret_b)
```

```python
import functools
import math

import jax
import jax.numpy as jnp
import numpy as np
from jax import lax
from jax.experimental import pallas as pl
from jax.experimental.pallas import tpu as pltpu

F32 = jnp.float32
BF16 = jnp.bfloat16

D_MODEL = 1024
DEPTH = 2
GRID_W = 64
HEAD_DIM = 64
ROPE_PAIRS = HEAD_DIM // 4
ROPE_BASE = 10000.0
EPS = 1e-6
NEG_INF = -1e30
D_FF = 2816
N_MOD = 9
N_BRANCH = 4
BRANCH_W = 512

HY_W = BRANCH_W
HY_BANDS = 16
HY_EMB = 1 + 2 * HY_BANDS
HY_FH = 64
HY_SIN_W = 1.0
HY_TARGET = 1e-2
HY_DECAY_SHORT_PCT = 0.3
HY_DECAY_LONG_PCT = 1.5

DIFF_HEADS = 4
WIN_Q_HEADS = 8
WIN_KV_HEADS = 2
WIN_GROUP = WIN_Q_HEADS // WIN_KV_HEADS
WINDOW = 128
BLOCK = 128
RET_HEADS = 4
RET_DK = 64
RET_DV = 128

COL_HY = 0
COL_DIFF = 3 * HY_W
COL_WIN = COL_DIFF + 3 * 512
COL_RET = COL_WIN + 512 + 128 + 128
COL_GATE = COL_RET + 256 + 256 + 512 + 512
IN_COLS = COL_GATE + N_BRANCH * D_MODEL

LANE = 128
TOKEN_TILE = 1024
FF_TILE = 256
ADA_TILE = 1152
ATT_Q_TILE = 256
VMEM_LIMIT = 56 * 1024 * 1024


def _cparams(sem):
    return pltpu.CompilerParams(dimension_semantics=sem, vmem_limit_bytes=VMEM_LIMIT)


def _bdot(a, b):
    return jnp.dot(a.astype(BF16), b.astype(BF16), preferred_element_type=F32)


def _bdot_nt(a, b):
    return lax.dot_general(a.astype(BF16), b.astype(BF16), (((1,), (1,)), ((), ())),
                           preferred_element_type=F32)


def _split(a):
    hi = a.astype(BF16)
    lo = (a - hi.astype(F32)).astype(BF16)
    return hi, lo


def _dot3(a, b):
    ah, al = _split(a)
    bh, bl = _split(b)
    d = functools.partial(jnp.dot, preferred_element_type=F32)
    return d(ah, bh) + d(ah, bl) + d(al, bh)


def _dot3_pre(ah, al, b):
    bh, bl = _split(b)
    d = functools.partial(jnp.dot, preferred_element_type=F32)
    return d(ah, bh) + d(ah, bl) + d(al, bh)


def _chunk_mean(sq, bd, n):
    hi, lo = _split(sq)
    d = functools.partial(jnp.dot, preferred_element_type=F32)
    return (d(hi, bd) + d(lo, bd)) * (1.0 / n)


def _rms(x, gain):
    y = x * lax.rsqrt(jnp.mean(x * x, axis=-1, keepdims=True) + EPS)
    return y if gain is None else y * gain


def _silu(x):
    return x * jax.nn.sigmoid(x)


def _lane(shape):
    return lax.broadcasted_iota(jnp.int32, shape, 1)


def _row(shape):
    return lax.broadcasted_iota(jnp.int32, shape, 0)


def _rope128(x, cos, sin):
    partner = jnp.where((_lane(x.shape) & 31) < 16,
                        pltpu.roll(x, LANE - ROPE_PAIRS, axis=1),
                        pltpu.roll(x, ROPE_PAIRS, axis=1))
    return x * cos + partner * sin


def _log_gamma(decay):
    x = -decay
    return -(jnp.maximum(x, 0.0) + jnp.log1p(jnp.exp(-jnp.abs(x))))


def _ada_kernel(cond_ref, w_ref, b_ref, o_ref):
    s = _silu(cond_ref[...])
    o_ref[...] = _bdot(s, w_ref[...]) + b_ref[...]


def _ada_call(cond, w_ada, b_ada):
    rows = cond.shape[0]
    n_out = N_MOD * D_MODEL
    return pl.pallas_call(
        _ada_kernel,
        grid=(DEPTH, n_out // ADA_TILE),
        in_specs=[
            pl.BlockSpec((rows, D_MODEL), lambda l, j: (0, 0)),
            pl.BlockSpec((None, D_MODEL, ADA_TILE), lambda l, j: (l, 0, j)),
            pl.BlockSpec((None, 1, ADA_TILE), lambda l, j: (l, 0, j)),
        ],
        out_specs=pl.BlockSpec((None, rows, ADA_TILE), lambda l, j: (l, 0, j)),
        out_shape=jax.ShapeDtypeStruct((DEPTH, rows, n_out), F32),
        compiler_params=_cparams(("parallel", "parallel")),
        name="ada_mod",
    )(cond, w_ada, b_ada.reshape(DEPTH, 1, n_out))


def _ffn_kernel(*refs, mod_base, emit_next):
    if emit_next:
        (x_ref, mod_ref, gain_ref, gain2_ref, wa_ref, wg_ref, wo_ref,
         y_ref, h2_ref, h_s, acc_s) = refs
    else:
        (x_ref, mod_ref, gain_ref, wa_ref, wg_ref, wo_ref, y_ref, h_s, acc_s) = refs
    j = pl.program_id(1)

    @pl.when(j == 0)
    def _():
        m = mod_ref[...]
        h = _rms(x_ref[...], gain_ref[...]) * (1.0 + m[mod_base + 1:mod_base + 2]) + m[mod_base:mod_base + 1]
        h_s[...] = h.astype(BF16)
        acc_s[...] = jnp.zeros_like(acc_s)

    h = h_s[...]
    a = jnp.dot(h, wa_ref[...], preferred_element_type=F32)
    g = jnp.dot(h, wg_ref[...], preferred_element_type=F32)
    act = (_silu(a) * g).astype(BF16)
    acc_s[...] += jnp.dot(act, wo_ref[...], preferred_element_type=F32)

    @pl.when(j == pl.num_programs(1) - 1)
    def _():
        m = mod_ref[...]
        y = x_ref[...] + 0.5 * m[mod_base + 2:mod_base + 3] * acc_s[...]
        y_ref[...] = y
        if emit_next:
            h2 = _rms(y, gain2_ref[...]) * (1.0 + m[mod_base + 4:mod_base + 5]) + m[mod_base + 3:mod_base + 4]
            h2_ref[...] = h2.astype(BF16)


def _ffn_call(x, mod, mod_row, gain, gain2, w_in, w_out, layer, mod_base, name):
    nt, tm, _ = x.shape
    nf = D_FF // FF_TILE
    emit_next = gain2 is not None
    tok = pl.BlockSpec((None, tm, D_MODEL), lambda i, j: (i, 0, 0))
    vec = pl.BlockSpec((1, D_MODEL), lambda i, j: (0, 0))
    in_specs = [tok, pl.BlockSpec((None, N_MOD, D_MODEL), lambda i, j: (mod_row(i), 0, 0)), vec]
    args = [x, mod, gain]
    if emit_next:
        in_specs.append(vec)
        args.append(gain2)
    in_specs += [
        pl.BlockSpec((None, D_MODEL, FF_TILE), lambda i, j: (layer, 0, j)),
        pl.BlockSpec((None, D_MODEL, FF_TILE), lambda i, j: (layer, 0, nf + j)),
        pl.BlockSpec((None, FF_TILE, D_MODEL), lambda i, j: (layer, j, 0)),
    ]
    args += [w_in, w_in, w_out]
    out_specs = [tok]
    out_shape = [jax.ShapeDtypeStruct(x.shape, F32)]
    if emit_next:
        out_specs.append(tok)
        out_shape.append(jax.ShapeDtypeStruct(x.shape, BF16))
    res = pl.pallas_call(
        functools.partial(_ffn_kernel, mod_base=mod_base, emit_next=emit_next),
        grid=(nt, nf),
        in_specs=in_specs,
        out_specs=out_specs,
        out_shape=out_shape,
        scratch_shapes=[pltpu.VMEM((tm, D_MODEL), BF16), pltpu.VMEM((tm, D_MODEL), F32)],
        compiler_params=_cparams(("parallel", "arbitrary")),
        name=name,
    )(*args)
    return res if emit_next else (res[0], None)


def _hy_filter_kernel(feats_ref, w1_ref, b1_ref, w2_ref, b2_ref, w3_ref, win_ref,
                      ch_ref, cl_ref, sh_ref, sl_ref, kr_ref, ki_ref, kn_ref, *, seq):
    n_fft = 2 * seq
    z = jnp.sin(HY_SIN_W * (_dot3(feats_ref[...], w1_ref[...]) + b1_ref[...]))
    z = jnp.sin(HY_SIN_W * (_dot3(z, w2_ref[...]) + b2_ref[...]))
    zz = _dot3(z, w3_ref[...])
    win = win_ref[...]
    hf = zz[:, :HY_W] * win
    hb = zz[:, HY_W:] * win
    norm = (jnp.sum(jnp.abs(hf), axis=0, keepdims=True)
            + jnp.sum(jnp.abs(hb), axis=0, keepdims=True))
    hf = hf / norm
    hb = hb / norm
    row = _row(hf.shape)
    hb0 = jnp.where(row == 0, 0.0, hb)
    even = hf + hb0
    odd = hb0 - hf
    wk = jnp.where(row == 0, 1.0 / n_fft, 2.0 / n_fft)
    kr_ref[...] = _dot3_pre(ch_ref[...], cl_ref[...], even) * wk
    ki_ref[...] = _dot3_pre(sh_ref[...], sl_ref[...], odd) * wk
    sgn = jnp.where((row & 1) == 0, 1.0, -1.0)
    kn_ref[...] = jnp.sum(even * sgn, axis=0, keepdims=True) * (1.0 / n_fft)


def _hy_filter_call(seq, feats, w1, b1, w2, b2, w3, win, ch, cl, sh, sl):
    args = (feats, w1, b1, w2, b2, w3, win, ch, cl, sh, sl)
    return pl.pallas_call(
        functools.partial(_hy_filter_kernel, seq=seq),
        out_shape=[jax.ShapeDtypeStruct((seq, HY_W), F32),
                   jax.ShapeDtypeStruct((seq, HY_W), F32),
                   jax.ShapeDtypeStruct((1, HY_W), F32)],
        compiler_params=pltpu.CompilerParams(vmem_limit_bytes=VMEM_LIMIT),
        name=f"hyena_filter_{seq}",
    )(*args)


def _hyena_kernel(h_ref, w_ref, cw_ref, cb_ref, kr_ref, ki_ref, kn_ref, skip_ref,
                  c_ref, s_ref, y_ref, *, group, seq):
    proj = jnp.dot(h_ref[...], w_ref[...], preferred_element_type=F32)
    cw = cw_ref[...]
    cmat = c_ref[...]
    smat = s_ref[...]
    kr = kr_ref[...]
    ki = ki_ref[...]
    row = _row((seq, 3 * HY_W))
    row_w = _row((seq, HY_W))
    sgn = jnp.where((row_w & 1) == 0, 1.0, -1.0)
    for g in range(group):
        hy = proj[g * seq:(g + 1) * seq]
        prev = jnp.where(row == 0, 0.0, pltpu.roll(hy, 1, axis=0))
        nxt = jnp.where(row == seq - 1, 0.0, pltpu.roll(hy, seq - 1, axis=0))
        u = prev * cw[0:1] + hy * cw[1:2] + nxt * cw[2:3] + cb_ref[...]
        v, x0, x1 = u[:, :HY_W], u[:, HY_W:2 * HY_W], u[:, 2 * HY_W:]
        z = v * x1
        zb = z.astype(BF16)
        zr = jnp.dot(cmat, zb, preferred_element_type=F32)
        zs = jnp.dot(smat, zb, preferred_element_type=F32)
        yr = (zr * kr + zs * ki).astype(BF16)
        yi = (zr * ki - zs * kr).astype(BF16)
        nyq = jnp.sum(z * sgn, axis=0, keepdims=True) * kn_ref[...]
        conv = (jnp.dot(cmat, yr, preferred_element_type=F32)
                - jnp.dot(smat, yi, preferred_element_type=F32) + sgn * nyq)
        y_ref[g * seq:(g + 1) * seq, :] = (x0 * (conv + skip_ref[...] * z)).astype(BF16)


def _hyena_call(h, w_hy, layer, cw, cb, kr, ki, kn, skip, cmat, smat, group, seq, name):
    nt, tm, _ = h.shape
    const2 = lambda a: pl.BlockSpec(a.shape, lambda i: (0, 0))
    return pl.pallas_call(
        functools.partial(_hyena_kernel, group=group, seq=seq),
        grid=(nt,),
        in_specs=[
            pl.BlockSpec((None, tm, D_MODEL), lambda i: (i, 0, 0)),
            pl.BlockSpec((None, D_MODEL, 3 * HY_W), lambda i: (layer, 0, 0)),
            const2(cw), const2(cb), const2(kr), const2(ki), const2(kn), const2(skip),
            const2(cmat), const2(smat),
        ],
        out_specs=pl.BlockSpec((None, tm, HY_W), lambda i: (i, 0, 0)),
        out_shape=jax.ShapeDtypeStruct((nt, tm, HY_W), BF16),
        compiler_params=_cparams(("parallel",)),
        name=name,
    )(h, w_hy, cw, cb, kr, ki, kn, skip, cmat, smat)


def _diff_kernel(*refs, group, seq, past, lam_init):
    has_cache = past > 0
    if has_cache:
        (h_ref, w_ref, gq_ref, gk_ref, lam_ref, gs_ref, bd_ref, cos_ref, sin_ref,
         ck_ref, cv_ref, y_ref, k_s, v_s) = refs
    else:
        (h_ref, w_ref, gq_ref, gk_ref, lam_ref, gs_ref, bd_ref, y_ref, ko_ref, vo_ref) = refs
    width = 2 * DIFF_HEADS * HEAD_DIM
    proj = jnp.dot(h_ref[...], w_ref[...], preferred_element_type=F32)
    q, k, v = proj[:, :width], proj[:, width:2 * width], proj[:, 2 * width:]
    bd = bd_ref[...]
    q = q * lax.rsqrt(_chunk_mean(q * q, bd, HEAD_DIM) + EPS) * gq_ref[...]
    k = k * lax.rsqrt(_chunk_mean(k * k, bd, HEAD_DIM) + EPS) * gk_ref[...]
    dl = lam_ref[...]
    lam = (jnp.exp(jnp.sum(dl[0:1] * dl[1:2], axis=1, keepdims=True))
           - jnp.exp(jnp.sum(dl[2:3] * dl[3:4], axis=1, keepdims=True)) + lam_init)
    scale = HEAD_DIM ** -0.5
    if has_cache:
        cos, sin = cos_ref[...], sin_ref[...]
        q = jnp.concatenate([_rope128(q[:, s * LANE:(s + 1) * LANE], cos, sin)
                             for s in range(width // LANE)], axis=1)
        k = jnp.concatenate([_rope128(k[:, s * LANE:(s + 1) * LANE], cos, sin)
                             for s in range(width // LANE)], axis=1)
        k_s[0:seq, :] = k.astype(BF16)
        k_s[seq:seq + past, :] = ck_ref[...].astype(BF16)
        v_s[0:seq, :] = v.astype(BF16)
        v_s[seq:seq + past, :] = cv_ref[...].astype(BF16)
    else:
        ko_ref[...] = k
        vo_ref[...] = v
    q = q * scale
    tq = min(seq, ATT_Q_TILE)
    lane = _lane((tq, LANE))
    for g in range(group):
        for hh in range(DIFF_HEADS):
            sl = slice(hh * LANE, (hh + 1) * LANE)
            if has_cache:
                keys, vals = k_s[:, sl], v_s[:, sl]
            else:
                keys = k[g * seq:(g + 1) * seq, sl].astype(BF16)
                vals = v[g * seq:(g + 1) * seq, sl].astype(BF16)
            for qi in range(seq // tq):
                r0 = g * seq + qi * tq
                q128 = q[r0:r0 + tq, sl]
                qs = jnp.concatenate([jnp.where(lane < HEAD_DIM, q128, 0.0),
                                      jnp.where(lane >= HEAD_DIM, q128, 0.0)], axis=0)
                s = _bdot_nt(qs, keys)
                e = jnp.exp(s - jnp.max(s, axis=-1, keepdims=True))
                r = 1.0 / jnp.sum(e, axis=-1, keepdims=True)
                a = e[:tq] * r[:tq] - e[tq:] * (lam * r[tq:])
                o = jnp.dot(a.astype(BF16), vals, preferred_element_type=F32)
                y = _rms(o, gs_ref[:, sl]) * (1.0 - lam_init)
                y_ref[r0:r0 + tq, sl] = y.astype(BF16)


def _diff_call(h, tile0, w_diff, layer, gq, gk, lam, gs, bd, lam_init, group, seq,
               rope=None, cache=None, name="diff"):
    nt, tm = h.shape[0] - tile0, h.shape[1]
    width = 2 * DIFF_HEADS * HEAD_DIM
    const2 = lambda a: pl.BlockSpec(a.shape, lambda i: (0, 0))
    in_specs = [
        pl.BlockSpec((None, tm, D_MODEL), lambda i: (i + tile0, 0, 0)),
        pl.BlockSpec((None, D_MODEL, 3 * width), lambda i: (layer, 0, 0)),
        const2(gq), const2(gk), const2(lam), const2(gs), const2(bd),
    ]
    args = [h, w_diff, gq, gk, lam, gs, bd]
    y_spec = pl.BlockSpec((None, tm, width), lambda i: (i, 0, 0))
    y_shape = jax.ShapeDtypeStruct((nt, tm, width), BF16)
    if cache is None:
        past = 0
        out_specs = [y_spec, y_spec, y_spec]
        out_shape = [y_shape, jax.ShapeDtypeStruct((nt, tm, width), F32),
                     jax.ShapeDtypeStruct((nt, tm, width), F32)]
        scratch = []
    else:
        cos, sin = rope
        ck, cv = cache
        past = ck.shape[2]
        cspec = pl.BlockSpec((None, None, past, width), lambda i: (i, layer, 0, 0))
        in_specs += [const2(cos), const2(sin), cspec, cspec]
        args += [cos, sin, ck, cv]
        out_specs = [y_spec]
        out_shape = [y_shape]
        scratch = [pltpu.VMEM((seq + past, width), BF16), pltpu.VMEM((seq + past, width), BF16)]
    return pl.pallas_call(
        functools.partial(_diff_kernel, group=group, seq=seq, past=past, lam_init=lam_init),
        grid=(nt,),
        in_specs=in_specs,
        out_specs=out_specs,
        out_shape=out_shape,
        scratch_shapes=scratch,
        compiler_params=_cparams(("parallel",)),
        name=name,
    )(*args)


def _win_heads(q, hk, lane):
    out = []
    for gq in range(WIN_GROUP):
        j = hk * WIN_GROUP + gq
        slab = q[:, (j // 2) * LANE:(j // 2 + 1) * LANE]
        if j % 2 != hk:
            slab = pltpu.roll(slab, HEAD_DIM, axis=1)
        out.append(jnp.where(lane >= HEAD_DIM if hk == 1 else lane < HEAD_DIM, slab, 0.0))
    return out


def _win_place(o_heads, hk, lane):
    slabs = []
    for pair in range(WIN_GROUP // 2):
        halves = []
        for gq in (2 * pair, 2 * pair + 1):
            j = hk * WIN_GROUP + gq
            o = o_heads[gq]
            if j % 2 != hk:
                o = pltpu.roll(o, HEAD_DIM, axis=1)
            halves.append(o)
        slabs.append(jnp.where(lane < HEAD_DIM, halves[0], halves[1]))
    return slabs


def _sink_col(sink_ref, layer, hk, rows_per_head):
    rows = WIN_GROUP * rows_per_head
    r = _row((rows, 1))
    col = jnp.full((rows, 1), sink_ref[layer, hk * WIN_GROUP], F32)
    for gq in range(1, WIN_GROUP):
        col = jnp.where(r >= gq * rows_per_head, sink_ref[layer, hk * WIN_GROUP + gq], col)
    return col


def _win_kernel(*refs, group, seq, past, layer):
    has_cache = past > 0
    if has_cache:
        (sink_ref, h_ref, w_ref, gq_ref, gk_ref, bd_ref, cos_ref, sin_ref, ck_ref, cv_ref,
         y_ref, k_s, v_s) = refs
    else:
        (sink_ref, h_ref, w_ref, gq_ref, gk_ref, bd_ref, y_ref, ko_ref, vo_ref) = refs
    qw = WIN_Q_HEADS * HEAD_DIM
    kw = WIN_KV_HEADS * HEAD_DIM
    proj = jnp.dot(h_ref[...], w_ref[...], preferred_element_type=F32)
    q, k, v = proj[:, :qw], proj[:, qw:qw + kw], proj[:, qw + kw:]
    bd = bd_ref[...]
    q = q * lax.rsqrt(_chunk_mean(q * q, bd, HEAD_DIM) + EPS) * gq_ref[...]
    k = k * lax.rsqrt(_chunk_mean(k * k, bd[:kw, :kw], HEAD_DIM) + EPS) * gk_ref[...]
    scale = HEAD_DIM ** -0.5
    if not has_cache:
        ko_ref[...] = k
        vo_ref[...] = v
        q = q * scale
        lane = _lane((seq, LANE))
        for g in range(group):
            rows = slice(g * seq, (g + 1) * seq)
            kb = k[rows].astype(BF16)
            vb = v[rows].astype(BF16)
            qg = q[rows]
            for hk in range(WIN_KV_HEADS):
                qs = jnp.concatenate(_win_heads(qg, hk, lane), axis=0)
                s = _bdot_nt(qs, kb)
                sink = _sink_col(sink_ref, layer, hk, seq)
                m = jnp.maximum(jnp.max(s, axis=-1, keepdims=True), sink)
                e = jnp.exp(s - m)
                r = 1.0 / (jnp.sum(e, axis=-1, keepdims=True) + jnp.exp(sink - m))
                o = jnp.dot(e.astype(BF16), vb, preferred_element_type=F32) * r
                slabs = _win_place([o[gq * seq:(gq + 1) * seq] for gq in range(WIN_GROUP)], hk, lane)
                for pair, slab in enumerate(slabs):
                    c0 = (hk * 2 + pair) * LANE
                    y_ref[rows, c0:c0 + LANE] = slab.astype(BF16)
        return

    cos, sin = cos_ref[...], sin_ref[...]
    q = jnp.concatenate([_rope128(q[:, s * LANE:(s + 1) * LANE], cos, sin)
                         for s in range(qw // LANE)], axis=1) * scale
    k = _rope128(k, cos, sin)
    zpad = jnp.zeros((BLOCK, kw), BF16)
    k_s[0:BLOCK, :] = zpad
    k_s[BLOCK:BLOCK + seq, :] = k.astype(BF16)
    k_s[BLOCK + seq:2 * BLOCK + seq, :] = zpad
    v_s[0:BLOCK, :] = zpad
    v_s[BLOCK:BLOCK + seq, :] = v.astype(BF16)
    v_s[BLOCK + seq:2 * BLOCK + seq, :] = zpad
    ckb = ck_ref[...].astype(BF16)
    cvb = cv_ref[...].astype(BF16)
    lane_l = _lane((seq, LANE))
    lane_b = _lane((BLOCK, LANE))
    nb = seq // BLOCK
    rows_q = WIN_GROUP * BLOCK
    kk = _lane((rows_q, 3 * BLOCK))
    qi = _row((rows_q, 3 * BLOCK)) & (BLOCK - 1)
    rel = kk - BLOCK - qi
    in_window = (rel <= WINDOW) & (rel >= -WINDOW)
    for hk in range(WIN_KV_HEADS):
        heads = _win_heads(q, hk, lane_l)
        sink = _sink_col(sink_ref, layer, hk, BLOCK)
        for n in range(nb):
            qs = jnp.concatenate([hd[n * BLOCK:(n + 1) * BLOCK] for hd in heads], axis=0)
            tk = kk + (n - 1) * BLOCK
            valid = in_window & (tk >= 0) & (tk < seq)
            sb = jnp.where(valid, _bdot_nt(qs, k_s[n * BLOCK:(n + 3) * BLOCK, :]), NEG_INF)
            sc = _bdot_nt(qs, ckb)
            m = jnp.maximum(jnp.maximum(jnp.max(sb, axis=-1, keepdims=True),
                                        jnp.max(sc, axis=-1, keepdims=True)), sink)
            eb = jnp.exp(sb - m)
            ec = jnp.exp(sc - m)
            r = 1.0 / (jnp.sum(eb, axis=-1, keepdims=True) + jnp.sum(ec, axis=-1, keepdims=True)
                       + jnp.exp(sink - m))
            o = (jnp.dot(eb.astype(BF16), v_s[n * BLOCK:(n + 3) * BLOCK, :], preferred_element_type=F32)
                 + jnp.dot(ec.astype(BF16), cvb, preferred_element_type=F32)) * r
            slabs = _win_place([o[gq * BLOCK:(gq + 1) * BLOCK] for gq in range(WIN_GROUP)], hk, lane_b)
            for pair, slab in enumerate(slabs):
                c0 = (hk * 2 + pair) * LANE
                y_ref[n * BLOCK:(n + 1) * BLOCK, c0:c0 + LANE] = slab.astype(BF16)


def _win_call(h, tile0, w_win, layer, sink, gq, gk, bd, group, seq, rope=None, cache=None, name="win"):
    nt, tm = h.shape[0] - tile0, h.shape[1]
    qw = WIN_Q_HEADS * HEAD_DIM
    kw = WIN_KV_HEADS * HEAD_DIM
    const2 = lambda a: pl.BlockSpec(a.shape, lambda i: (0, 0))
    in_specs = [
        pl.BlockSpec(memory_space=pltpu.SMEM),
        pl.BlockSpec((None, tm, D_MODEL), lambda i: (i + tile0, 0, 0)),
        pl.BlockSpec((None, D_MODEL, qw + 2 * kw), lambda i: (layer, 0, 0)),
        const2(gq), const2(gk), const2(bd),
    ]
    args = [sink, h, w_win, gq, gk, bd]
    y_spec = pl.BlockSpec((None, tm, qw), lambda i: (i, 0, 0))
    y_shape = jax.ShapeDtypeStruct((nt, tm, qw), BF16)
    if cache is None:
        past = 0
        kv_spec = pl.BlockSpec((None, tm, kw), lambda i: (i, 0, 0))
        kv_shape = jax.ShapeDtypeStruct((nt, tm, kw), F32)
        out_specs = [y_spec, kv_spec, kv_spec]
        out_shape = [y_shape, kv_shape, kv_shape]
        scratch = []
    else:
        cos, sin = rope
        ck, cv = cache
        past = ck.shape[2]
        cspec = pl.BlockSpec((None, None, past, kw), lambda i: (i, layer, 0, 0))
        in_specs += [const2(cos), const2(sin), cspec, cspec]
        args += [cos, sin, ck, cv]
        out_specs = [y_spec]
        out_shape = [y_shape]
        scratch = [pltpu.VMEM((seq + 2 * BLOCK, kw), BF16), pltpu.VMEM((seq + 2 * BLOCK, kw), BF16)]
    return pl.pallas_call(
        functools.partial(_win_kernel, group=group, seq=seq, past=past, layer=layer),
        grid=(nt,),
        in_specs=in_specs,
        out_specs=out_specs,
        out_shape=out_shape,
        scratch_shapes=scratch,
        compiler_params=_cparams(("parallel",)),
        name=name,
    )(*args)


def _ret_kernel(*refs, group, seq, has_state):
    if has_state:
        h_ref, w_ref, df_ref, db_ref, s0f_ref, s0b_ref, y_ref = refs
    else:
        h_ref, w_ref, df_ref, db_ref, y_ref, sf_ref, sb_ref = refs
    qk_w = RET_HEADS * RET_DK
    v_w = RET_HEADS * RET_DV
    proj = jnp.dot(h_ref[...], w_ref[...], preferred_element_type=F32)
    q = proj[:, :qk_w]
    k = proj[:, qk_w:2 * qk_w] * (RET_DK ** -0.5)
    v = proj[:, 2 * qk_w:2 * qk_w + v_w]
    rg = proj[:, 2 * qk_w + v_w:]
    lgf_all = _log_gamma(df_ref[...])
    lgb_all = _log_gamma(db_ref[...])
    tq = min(seq, ATT_Q_TILE)
    lane = _lane((seq, LANE))
    t_full = _row((seq, LANE)).astype(F32)
    for hd in range(RET_HEADS):
        slab = slice((hd // 2) * LANE, (hd // 2 + 1) * LANE)
        vsl = slice(hd * RET_DV, (hd + 1) * RET_DV)
        half = hd % 2
        lgf = lgf_all[hd:hd + 1, 0:1]
        lgb = lgb_all[hd:hd + 1, 0:1]
        for g in range(group):
            rows = slice(g * seq, (g + 1) * seq)
            qm = jnp.where(lane >= RET_DK if half == 1 else lane < RET_DK, q[rows, slab], 0.0)
            k128 = k[rows, slab]
            kb = k128.astype(BF16)
            vb = v[rows, vsl].astype(BF16)
            if has_state:
                q_f = qm * jnp.exp((t_full + 1.0) * lgf)
                q_b = qm * jnp.exp((seq - t_full) * lgb)
                o_state = (_bdot(q_f, s0f_ref[hd // 2]) + _bdot(q_b, s0b_ref[hd // 2]))
            else:
                k_f = k128 * jnp.exp((seq - 1.0 - t_full) * lgf)
                k_b = k128 * jnp.exp(t_full * lgb)
                s_f = _bdot(k_f.T, vb)
                s_b = _bdot(k_b.T, vb)
                sf_ref[g, hd] = s_f[half * RET_DK:(half + 1) * RET_DK]
                sb_ref[g, hd] = s_b[half * RET_DK:(half + 1) * RET_DK]
            for qi in range(seq // tq):
                qrows = slice(qi * tq, (qi + 1) * tq)
                s = _bdot_nt(qm[qrows], kb)
                rel = ((_row((tq, seq)) + qi * tq) - _lane((tq, seq))).astype(F32)
                dec = (jnp.where(rel >= 0, jnp.exp(jnp.maximum(rel, 0.0) * lgf), 0.0)
                       + jnp.where(rel <= 0, jnp.exp(jnp.maximum(-rel, 0.0) * lgb), 0.0))
                o = jnp.dot((s * dec).astype(BF16), vb, preferred_element_type=F32)
                if has_state:
                    o = o + o_state[qrows]
                r0 = g * seq + qi * tq
                y = _rms(o, None) * _silu(rg[r0:r0 + tq, vsl])
                y_ref[r0:r0 + tq, vsl] = y.astype(BF16)


def _ret_call(h, tile0, w_ret, layer, dec_f, dec_b, group, seq, state=None, name="ret"):
    nt, tm = h.shape[0] - tile0, h.shape[1]
    v_w = RET_HEADS * RET_DV
    const2 = lambda a: pl.BlockSpec(a.shape, lambda i: (0, 0))
    in_specs = [
        pl.BlockSpec((None, tm, D_MODEL), lambda i: (i + tile0, 0, 0)),
        pl.BlockSpec((None, D_MODEL, w_ret.shape[2]), lambda i: (layer, 0, 0)),
        const2(dec_f), const2(dec_b),
    ]
    args = [h, w_ret, dec_f, dec_b]
    y_spec = pl.BlockSpec((None, tm, v_w), lambda i: (i, 0, 0))
    y_shape = jax.ShapeDtypeStruct((nt, tm, v_w), BF16)
    if state is None:
        s_spec = pl.BlockSpec((group, RET_HEADS, RET_DK, RET_DV), lambda i: (i, 0, 0, 0))
        s_shape = jax.ShapeDtypeStruct((nt * group, RET_HEADS, RET_DK, RET_DV), F32)
        out_specs = [y_spec, s_spec, s_spec]
        out_shape = [y_shape, s_shape, s_shape]
    else:
        s0f, s0b = state
        sspec = pl.BlockSpec((None, None, 2, LANE, RET_DV), lambda i: (i, layer, 0, 0, 0))
        in_specs += [sspec, sspec]
        args += [s0f, s0b]
        out_specs = [y_spec]
        out_shape = [y_shape]
    return pl.pallas_call(
        functools.partial(_ret_kernel, group=group, seq=seq, has_state=state is not None),
        grid=(nt,),
        in_specs=in_specs,
        out_specs=out_specs,
        out_shape=out_shape,
        compiler_params=_cparams(("parallel",)),
        name=name,
    )(*args)


def _merge_kernel(x_ref, h_ref, mod_ref, y0_ref, y1_ref, y2_ref, y3_ref, wg_ref, wb_ref, wo_ref,
                  o_ref, acc_s):
    kb = pl.program_id(1)
    gate = jax.nn.sigmoid(jnp.dot(h_ref[...], wg_ref[...], preferred_element_type=F32))
    for idx, y_ref in enumerate((y0_ref, y1_ref, y2_ref, y3_ref)):
        @pl.when(kb == idx)
        def _(y_ref=y_ref, idx=idx):
            part = gate * jnp.dot(y_ref[...], wb_ref[...], preferred_element_type=F32)
            if idx == 0:
                acc_s[...] = part
            else:
                acc_s[...] += part

    @pl.when(kb == N_BRANCH - 1)
    def _():
        mixed = jnp.dot(acc_s[...].astype(BF16), wo_ref[...], preferred_element_type=F32)
        o_ref[...] = x_ref[...] + mod_ref[5:6] * mixed


def _merge_call(x, h, tile0, mod, mod_row, ys, w_gate, w_branch, w_out, layer, name):
    nt, tm, _ = x.shape
    tok = pl.BlockSpec((None, tm, D_MODEL), lambda i, k: (i, 0, 0))
    ysp = pl.BlockSpec((None, tm, BRANCH_W), lambda i, k: (i, 0, 0))
    return pl.pallas_call(
        _merge_kernel,
        grid=(nt, N_BRANCH),
        in_specs=[
            tok,
            pl.BlockSpec((None, tm, D_MODEL), lambda i, k: (i + tile0, 0, 0)),
            pl.BlockSpec((None, N_MOD, D_MODEL), lambda i, k: (mod_row(i), 0, 0)),
            ysp, ysp, ysp, ysp,
            pl.BlockSpec((None, D_MODEL, D_MODEL), lambda i, k: (layer, 0, k)),
            pl.BlockSpec((None, None, BRANCH_W, D_MODEL), lambda i, k: (layer, k, 0, 0)),
            pl.BlockSpec((None, D_MODEL, D_MODEL), lambda i, k: (layer, 0, 0)),
        ],
        out_specs=tok,
        out_shape=jax.ShapeDtypeStruct(x.shape, F32),
        scratch_shapes=[pltpu.VMEM((tm, D_MODEL), F32)],
        compiler_params=_cparams(("parallel", "arbitrary")),
        name=name,
    )(x, h, mod, *ys, w_gate, w_branch, w_out)


def _dft_tables(seq):
    n_fft = 2 * seq
    idx = np.arange(seq, dtype=np.int64)
    ang = 2.0 * np.pi * ((idx[:, None] * idx[None, :]) % n_fft).astype(np.float64) / n_fft
    out = []
    for m in (np.cos(ang), np.sin(ang)):
        hi = jnp.asarray(m, F32).astype(BF16)
        lo = (jnp.asarray(m, F32) - hi.astype(F32)).astype(BF16)
        out += [hi, lo]
    return out


def _hyena_tables(seq):
    t = np.arange(seq, dtype=np.float32) / np.float32(seq)
    f = np.arange(1, HY_BANDS + 1, dtype=np.float32)
    ang = np.float32(2.0 * math.pi) * t[:, None] * f[None, :]
    feats = np.zeros((seq, LANE), np.float32)
    feats[:, 0] = t
    feats[:, 1:1 + HY_BANDS] = np.sin(ang)
    feats[:, 1 + HY_BANDS:HY_EMB] = np.cos(ang)
    min_decay = math.log(HY_TARGET) / HY_DECAY_LONG_PCT
    max_decay = math.log(HY_TARGET) / HY_DECAY_SHORT_PCT
    deltas = np.linspace(min_decay, max_decay, HY_W, dtype=np.float32)
    window = np.exp(-t[:, None] * np.abs(deltas)[None, :]).astype(np.float32)
    return jnp.asarray(feats), jnp.asarray(window)


def _rope_tables(seq):
    pos = np.arange(seq)
    row = (pos // GRID_W).astype(np.float32)
    col = (pos % GRID_W).astype(np.float32)
    inv_freq = (np.float32(ROPE_BASE) ** (-np.arange(ROPE_PAIRS, dtype=np.float32) / np.float32(ROPE_PAIRS)))
    lane = np.arange(LANE)
    in_head = lane % HEAD_DIM
    use_col = in_head >= HEAD_DIM // 2
    pair = in_head % ROPE_PAIRS
    second = (in_head % (2 * ROPE_PAIRS)) >= ROPE_PAIRS
    p = np.where(use_col[None, :], col[:, None], row[:, None]).astype(np.float32)
    ang = (p * inv_freq[pair][None, :]).astype(np.float32)
    cos = np.cos(ang).astype(np.float32)
    sin = np.sin(ang).astype(np.float32)
    sin = np.where(second[None, :], sin, -sin)
    return jnp.asarray(cos), jnp.asarray(sin)


def _block_diag_ones(width, chunk):
    idx = np.arange(width) // chunk
    return jnp.asarray((idx[:, None] == idx[None, :]).astype(np.float32)).astype(BF16)


def kernel(x_prompt, x_sample, c, cache_diff_k, cache_diff_v, cache_win_k, cache_win_v, state_ret_f, state_ret_b, c_ctx, norm_ffa, norm_mix, norm_ffb, w_ada, b_ada, w_ffa_in, w_ffa_out, w_ffb_in, w_ffb_out, w_in, hy_conv_w, hy_conv_b, hy_f_w1, hy_f_b1, hy_f_w2, hy_f_b2, hy_f_w3, hy_skip, diff_q_norm, diff_k_norm, diff_lambda, diff_subln, win_q_norm, win_k_norm, win_sink, ret_decay_f, ret_decay_b, w_branch, w_out):
    batch, seq, _ = x_prompt.shape
    dec_batch, dec_seq, _ = x_sample.shape
    past = cache_diff_k.shape[2]
    ctx_group = TOKEN_TILE // seq
    assert TOKEN_TILE % seq == 0 and batch % ctx_group == 0 and dec_seq == TOKEN_TILE
    n_ctx = batch // ctx_group

    n_rows = 16
    cond = jnp.zeros((n_rows, D_MODEL), F32).at[0].set(c_ctx).at[1:1 + dec_batch].set(c)
    mod = _ada_call(cond, w_ada, b_ada).reshape(DEPTH, n_rows, N_MOD, D_MODEL)

    bf = lambda a: a.astype(BF16)
    w_ffa_in_b, w_ffa_out_b, w_ffb_in_b, w_ffb_out_b = bf(w_ffa_in), bf(w_ffa_out), bf(w_ffb_in), bf(w_ffb_out)
    w_hy = bf(w_in[:, :, COL_HY:COL_DIFF])
    w_diff = bf(w_in[:, :, COL_DIFF:COL_WIN])
    w_win = bf(w_in[:, :, COL_WIN:COL_RET])
    w_ret = bf(w_in[:, :, COL_RET:COL_GATE])
    w_gate = bf(w_in[:, :, COL_GATE:])
    w_branch_b, w_out_b = bf(w_branch), bf(w_out)

    bd64 = _block_diag_ones(2 * DIFF_HEADS * HEAD_DIM, HEAD_DIM)
    rope = _rope_tables(dec_seq)
    tables = {s: (_hyena_tables(s), _dft_tables(s)) for s in (seq, dec_seq)}

    ck_d = cache_diff_k.reshape(dec_batch, DEPTH, past, -1)
    cv_d = cache_diff_v.reshape(dec_batch, DEPTH, past, -1)
    ck_w = cache_win_k.reshape(dec_batch, DEPTH, past, -1)
    cv_w = cache_win_v.reshape(dec_batch, DEPTH, past, -1)
    s0_f = state_ret_f.reshape(dec_batch, DEPTH, 2, LANE, RET_DV)
    s0_b = state_ret_b.reshape(dec_batch, DEPTH, 2, LANE, RET_DV)

    pad_rows = lambda a, n: jnp.pad(a, ((0, n - a.shape[0]), (0, 0)))
    tile_lanes = lambda a, n: jnp.tile(a.reshape(1, -1), (1, n))

    y_ctx = x_prompt.reshape(n_ctx, TOKEN_TILE, D_MODEL)
    y_lat = x_sample
    ctx_row = lambda i: 0
    lat_row = lambda i: i + 1
    st = {k: [] for k in ("dk", "dv", "wk", "wv", "rf", "rb")}

    for l in range(DEPTH):
        lam_init = 0.8 - 0.6 * math.exp(-0.3 * l)
        gain = lambda a: a[l].reshape(1, -1)
        gq_d, gk_d = tile_lanes(diff_q_norm[l], 2 * DIFF_HEADS), tile_lanes(diff_k_norm[l], 2 * DIFF_HEADS)
        gs_d = tile_lanes(diff_subln[l], DIFF_HEADS)
        gq_w, gk_w = tile_lanes(win_q_norm[l], WIN_Q_HEADS), tile_lanes(win_k_norm[l], WIN_KV_HEADS)
        dec_f = jnp.broadcast_to(ret_decay_f[l][:, None], (RET_HEADS, LANE))
        dec_b = jnp.broadcast_to(ret_decay_b[l][:, None], (RET_HEADS, LANE))
        w1 = jnp.pad(hy_f_w1[l], ((0, LANE - HY_EMB), (0, LANE - HY_FH)))
        b1 = jnp.pad(hy_f_b1[l].reshape(1, -1), ((0, 0), (0, LANE - HY_FH)))
        w2 = jnp.pad(hy_f_w2[l], ((0, LANE - HY_FH), (0, LANE - HY_FH)))
        b2 = jnp.pad(hy_f_b2[l].reshape(1, -1), ((0, 0), (0, LANE - HY_FH)))
        w3 = pad_rows(hy_f_w3[l], LANE)
        cw, cb, skip = hy_conv_w[l], hy_conv_b[l].reshape(1, -1), hy_skip[l].reshape(1, -1)

        for is_lat in (False, True):
            x = y_lat if is_lat else y_ctx
            s_len = dec_seq if is_lat else seq
            group = 1 if is_lat else ctx_group
            mod_row = lat_row if is_lat else ctx_row
            tag = f"{'lat' if is_lat else 'ctx'}{l}"
            (feats, window), (c_hi, c_lo, s_hi, s_lo) = tables[s_len]

            x, h = _ffn_call(x, mod[l], mod_row, gain(norm_ffa), gain(norm_mix), w_ffa_in_b, w_ffa_out_b,
                             l, 0, f"ffn_a_{tag}")

            kr, ki, kn = _hy_filter_call(s_len, feats, w1, b1, w2, b2, w3, window, c_hi, c_lo, s_hi, s_lo)
            y_hy = _hyena_call(h, w_hy, l, cw, cb, kr, ki, kn, skip, c_hi, s_hi, group, s_len, f"hyena_{tag}")
            if is_lat:
                (y_diff,) = _diff_call(h, 0, w_diff, l, gq_d, gk_d, diff_lambda[l], gs_d, bd64, lam_init,
                                       group, s_len, rope=rope, cache=(ck_d, cv_d), name=f"diff_{tag}")
                (y_win,) = _win_call(h, 0, w_win, l, win_sink, gq_w, gk_w, bd64, group, s_len,
                                     rope=rope, cache=(ck_w, cv_w), name=f"win_{tag}")
                (y_ret,) = _ret_call(h, 0, w_ret, l, dec_f, dec_b, group, s_len, state=(s0_f, s0_b),
                                     name=f"ret_{tag}")
            else:
                y_diff, k_d, v_d = _diff_call(h, 0, w_diff, l, gq_d, gk_d, diff_lambda[l], gs_d, bd64,
                                              lam_init, group, s_len, name=f"diff_{tag}")
                y_win, k_w, v_w = _win_call(h, 0, w_win, l, win_sink, gq_w, gk_w, bd64, group, s_len,
                                            name=f"win_{tag}")
                y_ret, s_f, s_b = _ret_call(h, 0, w_ret, l, dec_f, dec_b, group, s_len, name=f"ret_{tag}")
                st["dk"].append(k_d.reshape(batch, seq, DIFF_HEADS, 2, HEAD_DIM))
                st["dv"].append(v_d.reshape(batch, seq, DIFF_HEADS, 2 * HEAD_DIM))
                st["wk"].append(k_w.reshape(batch, seq, WIN_KV_HEADS, HEAD_DIM))
                st["wv"].append(v_w.reshape(batch, seq, WIN_KV_HEADS, HEAD_DIM))
                st["rf"].append(s_f)
                st["rb"].append(s_b)

            x = _merge_call(x, h, 0, mod[l], mod_row, (y_hy, y_diff, y_win, y_ret), w_gate, w_branch_b,
                            w_out_b, l, f"merge_{tag}")
            x, _ = _ffn_call(x, mod[l], mod_row, gain(norm_ffb), None, w_ffb_in_b, w_ffb_out_b,
                             l, 6, f"ffn_b_{tag}")
            if is_lat:
                y_lat = x
            else:
                y_ctx = x

    stack = lambda xs: jnp.stack(xs, axis=1)
    return (y_ctx.reshape(batch, seq, D_MODEL), y_lat, stack(st["dk"]), stack(st["dv"]), stack(st["wk"]),
            stack(st["wv"]), stack(st["rf"]), stack(st["rb"]))
```

```python
import functools
import math

import jax
import jax.numpy as jnp
import numpy as np
from jax import lax
from jax.experimental import pallas as pl
from jax.experimental.pallas import tpu as pltpu

F32 = jnp.float32
BF16 = jnp.bfloat16

D_MODEL = 1024
DEPTH = 2
GRID_W = 64
HEAD_DIM = 64
ROPE_PAIRS = HEAD_DIM // 4
ROPE_BASE = 10000.0
EPS = 1e-6
NEG_INF = -1e30
D_FF = 2816
N_MOD = 9
N_BRANCH = 4
BRANCH_W = 512

HY_W = BRANCH_W
HY_BANDS = 16
HY_EMB = 1 + 2 * HY_BANDS
HY_FH = 64
HY_SIN_W = 1.0
HY_TARGET = 1e-2
HY_DECAY_SHORT_PCT = 0.3
HY_DECAY_LONG_PCT = 1.5

DIFF_HEADS = 4
WIN_Q_HEADS = 8
WIN_KV_HEADS = 2
WIN_GROUP = WIN_Q_HEADS // WIN_KV_HEADS
WINDOW = 128
BLOCK = 128
RET_HEADS = 4
RET_DK = 64
RET_DV = 128

COL_HY = 0
COL_DIFF = 3 * HY_W
COL_WIN = COL_DIFF + 3 * 512
COL_RET = COL_WIN + 512 + 128 + 128
COL_GATE = COL_RET + 256 + 256 + 512 + 512
IN_COLS = COL_GATE + N_BRANCH * D_MODEL

LANE = 128
TOKEN_TILE = 1024
DENSE_TILE = 512
FF_TILE = 256
MERGE_TILE = 256
ADA_TILE = 1152
ATT_Q_TILE = 256
VMEM_LIMIT = 56 * 1024 * 1024


def _cparams(sem):
    return pltpu.CompilerParams(dimension_semantics=sem, vmem_limit_bytes=VMEM_LIMIT)


def _bdot(a, b):
    return jnp.dot(a.astype(BF16), b.astype(BF16), preferred_element_type=F32)


def _bdot_nt(a, b):
    return lax.dot_general(a.astype(BF16), b.astype(BF16), (((1,), (1,)), ((), ())),
                           preferred_element_type=F32)


def _split(a):
    hi = a.astype(BF16)
    lo = (a - hi.astype(F32)).astype(BF16)
    return hi, lo


def _dot3(a, b):
    ah, al = _split(a)
    bh, bl = _split(b)
    d = functools.partial(jnp.dot, preferred_element_type=F32)
    return d(ah, bh) + d(ah, bl) + d(al, bh)


def _dot3_pre(ah, al, b):
    bh, bl = _split(b)
    d = functools.partial(jnp.dot, preferred_element_type=F32)
    return d(ah, bh) + d(ah, bl) + d(al, bh)


def _chunk_mean(sq, bd, n):
    hi, lo = _split(sq)
    d = functools.partial(jnp.dot, preferred_element_type=F32)
    return (d(hi, bd) + d(lo, bd)) * (1.0 / n)


def _rms(x, gain):
    y = x * lax.rsqrt(jnp.mean(x * x, axis=-1, keepdims=True) + EPS)
    return y if gain is None else y * gain


def _silu(x):
    return x * jax.nn.sigmoid(x)


def _lane(shape):
    return lax.broadcasted_iota(jnp.int32, shape, 1)


def _row(shape):
    return lax.broadcasted_iota(jnp.int32, shape, 0)


def _rope128(x, cos, sin):
    partner = jnp.where((_lane(x.shape) & 31) < 16,
                        pltpu.roll(x, LANE - ROPE_PAIRS, axis=1),
                        pltpu.roll(x, ROPE_PAIRS, axis=1))
    return x * cos + partner * sin


def _log_gamma(decay):
    x = -decay
    return -(jnp.maximum(x, 0.0) + jnp.log1p(jnp.exp(-jnp.abs(x))))


def _ada_kernel(cond_ref, w_ref, b_ref, o_ref):
    s = _silu(cond_ref[...])
    o_ref[...] = _bdot(s, w_ref[...]) + b_ref[...]


def _ada_call(cond, w_ada, b_ada):
    rows = cond.shape[0]
    n_out = N_MOD * D_MODEL
    return pl.pallas_call(
        _ada_kernel,
        grid=(DEPTH, n_out // ADA_TILE),
        in_specs=[
            pl.BlockSpec((rows, D_MODEL), lambda l, j: (0, 0)),
            pl.BlockSpec((None, D_MODEL, ADA_TILE), lambda l, j: (l, 0, j)),
            pl.BlockSpec((None, 1, ADA_TILE), lambda l, j: (l, 0, j)),
        ],
        out_specs=pl.BlockSpec((None, rows, ADA_TILE), lambda l, j: (l, 0, j)),
        out_shape=jax.ShapeDtypeStruct((DEPTH, rows, n_out), F32),
        compiler_params=_cparams(("parallel", "parallel")),
        name="ada_mod",
    )(cond, w_ada, b_ada.reshape(DEPTH, 1, n_out))


def _ffn_kernel(*refs, mod_base, emit_next):
    if emit_next:
        x_ref, mod_ref, gain_ref, gain2_ref, wi_ref, wo_ref, y_ref, h2_ref, act_s = refs
    else:
        x_ref, mod_ref, gain_ref, wi_ref, wo_ref, y_ref, act_s = refs
    m = mod_ref[...]
    x = x_ref[...]
    h = (_rms(x, gain_ref[...]) * (1.0 + m[mod_base + 1:mod_base + 2]) + m[mod_base:mod_base + 1]).astype(BF16)
    for c in range(D_FF // FF_TILE):
        a = jnp.dot(h, wi_ref[:, c * FF_TILE:(c + 1) * FF_TILE], preferred_element_type=F32)
        g = jnp.dot(h, wi_ref[:, D_FF + c * FF_TILE:D_FF + (c + 1) * FF_TILE], preferred_element_type=F32)
        act_s[:, c * FF_TILE:(c + 1) * FF_TILE] = (_silu(a) * g).astype(BF16)
    out = jnp.dot(act_s[...], wo_ref[...], preferred_element_type=F32)
    y = x + 0.5 * m[mod_base + 2:mod_base + 3] * out
    y_ref[...] = y
    if emit_next:
        h2 = _rms(y, gain2_ref[...]) * (1.0 + m[mod_base + 4:mod_base + 5]) + m[mod_base + 3:mod_base + 4]
        h2_ref[...] = h2.astype(BF16)


def _ffn_call(x, mod, mod_row, gain, gain2, w_in, w_out, layer, mod_base, name):
    nt, tm, _ = x.shape
    emit_next = gain2 is not None
    tok = pl.BlockSpec((None, tm, D_MODEL), lambda i: (i, 0, 0))
    vec = pl.BlockSpec((1, D_MODEL), lambda i: (0, 0))
    in_specs = [tok, pl.BlockSpec((None, N_MOD, D_MODEL), lambda i: (mod_row(i), 0, 0)), vec]
    args = [x, mod, gain]
    if emit_next:
        in_specs.append(vec)
        args.append(gain2)
    in_specs += [
        pl.BlockSpec((None, D_MODEL, 2 * D_FF), lambda i: (layer, 0, 0), pipeline_mode=pl.Buffered(1)),
        pl.BlockSpec((None, D_FF, D_MODEL), lambda i: (layer, 0, 0), pipeline_mode=pl.Buffered(1)),
    ]
    args += [w_in, w_out]
    out_specs = [tok]
    out_shape = [jax.ShapeDtypeStruct(x.shape, F32)]
    if emit_next:
        out_specs.append(tok)
        out_shape.append(jax.ShapeDtypeStruct(x.shape, BF16))
    res = pl.pallas_call(
        functools.partial(_ffn_kernel, mod_base=mod_base, emit_next=emit_next),
        grid=(nt,),
        in_specs=in_specs,
        out_specs=out_specs,
        out_shape=out_shape,
        scratch_shapes=[pltpu.VMEM((tm, D_FF), BF16)],
        compiler_params=_cparams(("parallel",)),
        name=name,
    )(*args)
    return res if emit_next else (res[0], None)


def _hy_filter_kernel(feats_ref, w1_ref, b1_ref, w2_ref, b2_ref, w3_ref, win_ref,
                      ch_ref, cl_ref, sh_ref, sl_ref, kr_ref, ki_ref, kn_ref, *, seq):
    n_fft = 2 * seq
    z = jnp.sin(HY_SIN_W * (_dot3(feats_ref[...], w1_ref[...]) + b1_ref[...]))
    z = jnp.sin(HY_SIN_W * (_dot3(z, w2_ref[...]) + b2_ref[...]))
    zz = _dot3(z, w3_ref[...])
    win = win_ref[...]
    hf = zz[:, :HY_W] * win
    hb = zz[:, HY_W:] * win
    norm = (jnp.sum(jnp.abs(hf), axis=0, keepdims=True)
            + jnp.sum(jnp.abs(hb), axis=0, keepdims=True))
    hf = hf / norm
    hb = hb / norm
    row = _row(hf.shape)
    hb0 = jnp.where(row == 0, 0.0, hb)
    even = hf + hb0
    odd = hb0 - hf
    wk = jnp.where(row == 0, 1.0 / n_fft, 2.0 / n_fft)
    kr_ref[...] = _dot3_pre(ch_ref[...], cl_ref[...], even) * wk
    ki_ref[...] = _dot3_pre(sh_ref[...], sl_ref[...], odd) * wk
    sgn = jnp.where((row & 1) == 0, 1.0, -1.0)
    kn_ref[...] = jnp.sum(even * sgn, axis=0, keepdims=True) * (1.0 / n_fft)


def _hy_filter_call(seq, feats, w1, b1, w2, b2, w3, win, ch, cl, sh, sl):
    args = (feats, w1, b1, w2, b2, w3, win, ch, cl, sh, sl)
    return pl.pallas_call(
        functools.partial(_hy_filter_kernel, seq=seq),
        out_shape=[jax.ShapeDtypeStruct((seq, HY_W), F32),
                   jax.ShapeDtypeStruct((seq, HY_W), F32),
                   jax.ShapeDtypeStruct((1, HY_W), F32)],
        compiler_params=pltpu.CompilerParams(vmem_limit_bytes=VMEM_LIMIT),
        name=f"hyena_filter_{seq}",
    )(*args)


def _hyena_kernel(h_ref, w_ref, cw_ref, cb_ref, kr_ref, ki_ref, kn_ref, skip_ref,
                  c_ref, s_ref, y_ref, *, group, seq):
    proj = jnp.dot(h_ref[...], w_ref[...], preferred_element_type=F32)
    cw = cw_ref[...]
    cmat = c_ref[...]
    smat = s_ref[...]
    kr = kr_ref[...]
    ki = ki_ref[...]
    row = _row((seq, 3 * HY_W))
    row_w = _row((seq, HY_W))
    sgn = jnp.where((row_w & 1) == 0, 1.0, -1.0)
    for g in range(group):
        hy = proj[g * seq:(g + 1) * seq]
        prev = jnp.where(row == 0, 0.0, pltpu.roll(hy, 1, axis=0))
        nxt = jnp.where(row == seq - 1, 0.0, pltpu.roll(hy, seq - 1, axis=0))
        u = prev * cw[0:1] + hy * cw[1:2] + nxt * cw[2:3] + cb_ref[...]
        v, x0, x1 = u[:, :HY_W], u[:, HY_W:2 * HY_W], u[:, 2 * HY_W:]
        z = v * x1
        zb = z.astype(BF16)
        zr = jnp.dot(cmat, zb, preferred_element_type=F32)
        zs = jnp.dot(smat, zb, preferred_element_type=F32)
        yr = (zr * kr + zs * ki).astype(BF16)
        yi = (zr * ki - zs * kr).astype(BF16)
        nyq = jnp.sum(z * sgn, axis=0, keepdims=True) * kn_ref[...]
        conv = (jnp.dot(cmat, yr, preferred_element_type=F32)
                - jnp.dot(smat, yi, preferred_element_type=F32) + sgn * nyq)
        y_ref[g * seq:(g + 1) * seq, :] = (x0 * (conv + skip_ref[...] * z)).astype(BF16)


def _hyena_call(h, w_hy, layer, cw, cb, kr, ki, kn, skip, cmat, smat, group, seq, name):
    nt, tm, _ = h.shape
    const2 = lambda a: pl.BlockSpec(a.shape, lambda i: (0, 0))
    return pl.pallas_call(
        functools.partial(_hyena_kernel, group=group, seq=seq),
        grid=(nt,),
        in_specs=[
            pl.BlockSpec((None, tm, D_MODEL), lambda i: (i, 0, 0)),
            pl.BlockSpec((None, D_MODEL, 3 * HY_W), lambda i: (layer, 0, 0)),
            const2(cw), const2(cb), const2(kr), const2(ki), const2(kn), const2(skip),
            const2(cmat), const2(smat),
        ],
        out_specs=pl.BlockSpec((None, tm, HY_W), lambda i: (i, 0, 0)),
        out_shape=jax.ShapeDtypeStruct((nt, tm, HY_W), BF16),
        compiler_params=_cparams(("parallel",)),
        name=name,
    )(h, w_hy, cw, cb, kr, ki, kn, skip, cmat, smat)


def _diff_kernel(*refs, group, seq, past, lam_init):
    has_cache = past > 0
    if has_cache:
        (h_ref, w_ref, gq_ref, gk_ref, lam_ref, gs_ref, bd_ref, cos_ref, sin_ref,
         ck_ref, cv_ref, y_ref, k_s, v_s) = refs
    else:
        (h_ref, w_ref, gq_ref, gk_ref, lam_ref, gs_ref, bd_ref, y_ref, ko_ref, vo_ref) = refs
    width = 2 * DIFF_HEADS * HEAD_DIM
    proj = jnp.dot(h_ref[...], w_ref[...], preferred_element_type=F32)
    q, k, v = proj[:, :width], proj[:, width:2 * width], proj[:, 2 * width:]
    bd = bd_ref[...]
    q = q * lax.rsqrt(_chunk_mean(q * q, bd, HEAD_DIM) + EPS) * gq_ref[...]
    k = k * lax.rsqrt(_chunk_mean(k * k, bd, HEAD_DIM) + EPS) * gk_ref[...]
    dl = lam_ref[...]
    lam = (jnp.exp(jnp.sum(dl[0:1] * dl[1:2], axis=1, keepdims=True))
           - jnp.exp(jnp.sum(dl[2:3] * dl[3:4], axis=1, keepdims=True)) + lam_init)
    scale = HEAD_DIM ** -0.5
    if has_cache:
        cos, sin = cos_ref[...], sin_ref[...]
        q = jnp.concatenate([_rope128(q[:, s * LANE:(s + 1) * LANE], cos, sin)
                             for s in range(width // LANE)], axis=1)
        k = jnp.concatenate([_rope128(k[:, s * LANE:(s + 1) * LANE], cos, sin)
                             for s in range(width // LANE)], axis=1)
        k_s[0:seq, :] = k.astype(BF16)
        k_s[seq:seq + past, :] = ck_ref[...].astype(BF16)
        v_s[0:seq, :] = v.astype(BF16)
        v_s[seq:seq + past, :] = cv_ref[...].astype(BF16)
    else:
        ko_ref[...] = k
        vo_ref[...] = v
    q = q * scale
    tq = min(seq, ATT_Q_TILE)
    lane = _lane((tq, LANE))
    for g in range(group):
        for hh in range(DIFF_HEADS):
            sl = slice(hh * LANE, (hh + 1) * LANE)
            if has_cache:
                keys, vals = k_s[:, sl], v_s[:, sl]
            else:
                keys = k[g * seq:(g + 1) * seq, sl].astype(BF16)
                vals = v[g * seq:(g + 1) * seq, sl].astype(BF16)
            for qi in range(seq // tq):
                r0 = g * seq + qi * tq
                q128 = q[r0:r0 + tq, sl]
                qs = jnp.concatenate([jnp.where(lane < HEAD_DIM, q128, 0.0),
                                      jnp.where(lane >= HEAD_DIM, q128, 0.0)], axis=0)
                s = _bdot_nt(qs, keys)
                e = jnp.exp(s - jnp.max(s, axis=-1, keepdims=True))
                r = 1.0 / jnp.sum(e, axis=-1, keepdims=True)
                a = e[:tq] * r[:tq] - e[tq:] * (lam * r[tq:])
                o = jnp.dot(a.astype(BF16), vals, preferred_element_type=F32)
                y = _rms(o, gs_ref[:, sl]) * (1.0 - lam_init)
                y_ref[r0:r0 + tq, sl] = y.astype(BF16)


def _diff_call(h, w_diff, layer, gq, gk, lam, gs, bd, lam_init, group, seq,
               rope=None, cache=None, name="diff"):
    nt, tm = h.shape[0], h.shape[1]
    width = 2 * DIFF_HEADS * HEAD_DIM
    const2 = lambda a: pl.BlockSpec(a.shape, lambda i: (0, 0))
    in_specs = [
        pl.BlockSpec((None, tm, D_MODEL), lambda i: (i, 0, 0)),
        pl.BlockSpec((None, D_MODEL, 3 * width), lambda i: (layer, 0, 0)),
        const2(gq), const2(gk), const2(lam), const2(gs), const2(bd),
    ]
    args = [h, w_diff, gq, gk, lam, gs, bd]
    y_spec = pl.BlockSpec((None, tm, width), lambda i: (i, 0, 0))
    y_shape = jax.ShapeDtypeStruct((nt, tm, width), BF16)
    if cache is None:
        past = 0
        out_specs = [y_spec, y_spec, y_spec]
        out_shape = [y_shape, jax.ShapeDtypeStruct((nt, tm, width), F32),
                     jax.ShapeDtypeStruct((nt, tm, width), F32)]
        scratch = []
    else:
        cos, sin = rope
        ck, cv = cache
        past = ck.shape[2]
        cspec = pl.BlockSpec((None, None, past, width), lambda i: (i, layer, 0, 0))
        in_specs += [const2(cos), const2(sin), cspec, cspec]
        args += [cos, sin, ck, cv]
        out_specs = [y_spec]
        out_shape = [y_shape]
        scratch = [pltpu.VMEM((seq + past, width), BF16), pltpu.VMEM((seq + past, width), BF16)]
    return pl.pallas_call(
        functools.partial(_diff_kernel, group=group, seq=seq, past=past, lam_init=lam_init),
        grid=(nt,),
        in_specs=in_specs,
        out_specs=out_specs,
        out_shape=out_shape,
        scratch_shapes=scratch,
        compiler_params=_cparams(("parallel",)),
        name=name,
    )(*args)


def _win_heads(q, hk, lane):
    out = []
    for gq in range(WIN_GROUP):
        j = hk * WIN_GROUP + gq
        slab = q[:, (j // 2) * LANE:(j // 2 + 1) * LANE]
        if j % 2 != hk:
            slab = pltpu.roll(slab, HEAD_DIM, axis=1)
        out.append(jnp.where(lane >= HEAD_DIM if hk == 1 else lane < HEAD_DIM, slab, 0.0))
    return out


def _win_place(o_heads, hk, lane):
    slabs = []
    for pair in range(WIN_GROUP // 2):
        halves = []
        for gq in (2 * pair, 2 * pair + 1):
            j = hk * WIN_GROUP + gq
            o = o_heads[gq]
            if j % 2 != hk:
                o = pltpu.roll(o, HEAD_DIM, axis=1)
            halves.append(o)
        slabs.append(jnp.where(lane < HEAD_DIM, halves[0], halves[1]))
    return slabs


def _sink_col(sink_ref, layer, hk, rows_per_head):
    rows = WIN_GROUP * rows_per_head
    r = _row((rows, 1))
    col = jnp.full((rows, 1), sink_ref[layer, hk * WIN_GROUP], F32)
    for gq in range(1, WIN_GROUP):
        col = jnp.where(r >= gq * rows_per_head, sink_ref[layer, hk * WIN_GROUP + gq], col)
    return col


def _win_kernel(*refs, group, seq, past, layer):
    has_cache = past > 0
    if has_cache:
        (sink_ref, h_ref, w_ref, gq_ref, gk_ref, bd_ref, cos_ref, sin_ref, ck_ref, cv_ref,
         y_ref, k_s, v_s) = refs
    else:
        (sink_ref, h_ref, w_ref, gq_ref, gk_ref, bd_ref, y_ref, ko_ref, vo_ref) = refs
    qw = WIN_Q_HEADS * HEAD_DIM
    kw = WIN_KV_HEADS * HEAD_DIM
    proj = jnp.dot(h_ref[...], w_ref[...], preferred_element_type=F32)
    q, k, v = proj[:, :qw], proj[:, qw:qw + kw], proj[:, qw + kw:]
    bd = bd_ref[...]
    q = q * lax.rsqrt(_chunk_mean(q * q, bd, HEAD_DIM) + EPS) * gq_ref[...]
    k = k * lax.rsqrt(_chunk_mean(k * k, bd[:kw, :kw], HEAD_DIM) + EPS) * gk_ref[...]
    scale = HEAD_DIM ** -0.5
    if not has_cache:
        ko_ref[...] = k
        vo_ref[...] = v
        q = q * scale
        lane = _lane((seq, LANE))
        for g in range(group):
            rows = slice(g * seq, (g + 1) * seq)
            kb = k[rows].astype(BF16)
            vb = v[rows].astype(BF16)
            qg = q[rows]
            for hk in range(WIN_KV_HEADS):
                qs = jnp.concatenate(_win_heads(qg, hk, lane), axis=0)
                s = _bdot_nt(qs, kb)
                sink = _sink_col(sink_ref, layer, hk, seq)
                m = jnp.maximum(jnp.max(s, axis=-1, keepdims=True), sink)
                e = jnp.exp(s - m)
                r = 1.0 / (jnp.sum(e, axis=-1, keepdims=True) + jnp.exp(sink - m))
                o = jnp.dot(e.astype(BF16), vb, preferred_element_type=F32) * r
                slabs = _win_place([o[gq * seq:(gq + 1) * seq] for gq in range(WIN_GROUP)], hk, lane)
                for pair, slab in enumerate(slabs):
                    c0 = (hk * 2 + pair) * LANE
                    y_ref[rows, c0:c0 + LANE] = slab.astype(BF16)
        return

    cos, sin = cos_ref[...], sin_ref[...]
    q = jnp.concatenate([_rope128(q[:, s * LANE:(s + 1) * LANE], cos, sin)
                         for s in range(qw // LANE)], axis=1) * scale
    k = _rope128(k, cos, sin)
    zpad = jnp.zeros((BLOCK, kw), BF16)
    k_s[0:BLOCK, :] = zpad
    k_s[BLOCK:BLOCK + seq, :] = k.astype(BF16)
    k_s[BLOCK + seq:2 * BLOCK + seq, :] = zpad
    v_s[0:BLOCK, :] = zpad
    v_s[BLOCK:BLOCK + seq, :] = v.astype(BF16)
    v_s[BLOCK + seq:2 * BLOCK + seq, :] = zpad
    ckb = ck_ref[...].astype(BF16)
    cvb = cv_ref[...].astype(BF16)
    lane_l = _lane((seq, LANE))
    lane_b = _lane((BLOCK, LANE))
    nb = seq // BLOCK
    rows_q = WIN_GROUP * BLOCK
    kk = _lane((rows_q, 3 * BLOCK))
    qi = _row((rows_q, 3 * BLOCK)) & (BLOCK - 1)
    rel = kk - BLOCK - qi
    in_window = (rel <= WINDOW) & (rel >= -WINDOW)
    for hk in range(WIN_KV_HEADS):
        heads = _win_heads(q, hk, lane_l)
        sink = _sink_col(sink_ref, layer, hk, BLOCK)
        for n in range(nb):
            qs = jnp.concatenate([hd[n * BLOCK:(n + 1) * BLOCK] for hd in heads], axis=0)
            tk = kk + (n - 1) * BLOCK
            valid = in_window & (tk >= 0) & (tk < seq)
            sb = jnp.where(valid, _bdot_nt(qs, k_s[n * BLOCK:(n + 3) * BLOCK, :]), NEG_INF)
            sc = _bdot_nt(qs, ckb)
            m = jnp.maximum(jnp.maximum(jnp.max(sb, axis=-1, keepdims=True),
                                        jnp.max(sc, axis=-1, keepdims=True)), sink)
            eb = jnp.exp(sb - m)
            ec = jnp.exp(sc - m)
            r = 1.0 / (jnp.sum(eb, axis=-1, keepdims=True) + jnp.sum(ec, axis=-1, keepdims=True)
                       + jnp.exp(sink - m))
            o = (jnp.dot(eb.astype(BF16), v_s[n * BLOCK:(n + 3) * BLOCK, :], preferred_element_type=F32)
                 + jnp.dot(ec.astype(BF16), cvb, preferred_element_type=F32)) * r
            slabs = _win_place([o[gq * BLOCK:(gq + 1) * BLOCK] for gq in range(WIN_GROUP)], hk, lane_b)
            for pair, slab in enumerate(slabs):
                c0 = (hk * 2 + pair) * LANE
                y_ref[n * BLOCK:(n + 1) * BLOCK, c0:c0 + LANE] = slab.astype(BF16)


def _win_call(h, w_win, layer, sink, gq, gk, bd, group, seq, rope=None, cache=None, name="win"):
    nt, tm = h.shape[0], h.shape[1]
    qw = WIN_Q_HEADS * HEAD_DIM
    kw = WIN_KV_HEADS * HEAD_DIM
    const2 = lambda a: pl.BlockSpec(a.shape, lambda i: (0, 0))
    in_specs = [
        pl.BlockSpec(memory_space=pltpu.SMEM),
        pl.BlockSpec((None, tm, D_MODEL), lambda i: (i, 0, 0)),
        pl.BlockSpec((None, D_MODEL, qw + 2 * kw), lambda i: (layer, 0, 0)),
        const2(gq), const2(gk), const2(bd),
    ]
    args = [sink, h, w_win, gq, gk, bd]
    y_spec = pl.BlockSpec((None, tm, qw), lambda i: (i, 0, 0))
    y_shape = jax.ShapeDtypeStruct((nt, tm, qw), BF16)
    if cache is None:
        past = 0
        kv_spec = pl.BlockSpec((None, tm, kw), lambda i: (i, 0, 0))
        kv_shape = jax.ShapeDtypeStruct((nt, tm, kw), F32)
        out_specs = [y_spec, kv_spec, kv_spec]
        out_shape = [y_shape, kv_shape, kv_shape]
        scratch = []
    else:
        cos, sin = rope
        ck, cv = cache
        past = ck.shape[2]
        cspec = pl.BlockSpec((None, None, past, kw), lambda i: (i, layer, 0, 0))
        in_specs += [const2(cos), const2(sin), cspec, cspec]
        args += [cos, sin, ck, cv]
        out_specs = [y_spec]
        out_shape = [y_shape]
        scratch = [pltpu.VMEM((seq + 2 * BLOCK, kw), BF16), pltpu.VMEM((seq + 2 * BLOCK, kw), BF16)]
    return pl.pallas_call(
        functools.partial(_win_kernel, group=group, seq=seq, past=past, layer=layer),
        grid=(nt,),
        in_specs=in_specs,
        out_specs=out_specs,
        out_shape=out_shape,
        scratch_shapes=scratch,
        compiler_params=_cparams(("parallel",)),
        name=name,
    )(*args)


def _ret_kernel(*refs, group, seq, has_state):
    if has_state:
        h_ref, w_ref, df_ref, db_ref, s0f_ref, s0b_ref, y_ref, dec_s = refs
    else:
        h_ref, w_ref, df_ref, db_ref, y_ref, sf_ref, sb_ref, dec_s = refs
    qk_w = RET_HEADS * RET_DK
    v_w = RET_HEADS * RET_DV
    proj = jnp.dot(h_ref[...], w_ref[...], preferred_element_type=F32)
    q = proj[:, :qk_w]
    k = proj[:, qk_w:2 * qk_w] * (RET_DK ** -0.5)
    v = proj[:, 2 * qk_w:2 * qk_w + v_w]
    rg = proj[:, 2 * qk_w + v_w:]
    lgf_all = _log_gamma(df_ref[...])
    lgb_all = _log_gamma(db_ref[...])
    tq = min(seq, ATT_Q_TILE)
    n_q = seq // tq

    @pl.when(pl.program_id(0) == 0)
    def _():
        width = 2 * seq - tq
        rel = (_row((tq, width)) - _lane((tq, width)) + (seq - tq)).astype(F32)
        for hd in range(RET_HEADS):
            lgf = lgf_all[hd:hd + 1, 0:1]
            lgb = lgb_all[hd:hd + 1, 0:1]
            dec_s[hd] = (jnp.where(rel >= 0, jnp.exp(jnp.maximum(rel, 0.0) * lgf), 0.0)
                         + jnp.where(rel <= 0, jnp.exp(jnp.maximum(-rel, 0.0) * lgb), 0.0))

    lane = _lane((seq, LANE))
    t_full = _row((seq, LANE)).astype(F32)
    for hd in range(RET_HEADS):
        slab = slice((hd // 2) * LANE, (hd // 2 + 1) * LANE)
        vsl = slice(hd * RET_DV, (hd + 1) * RET_DV)
        half = hd % 2
        lgf = lgf_all[hd:hd + 1, 0:1]
        lgb = lgb_all[hd:hd + 1, 0:1]
        for g in range(group):
            rows = slice(g * seq, (g + 1) * seq)
            qm = jnp.where(lane >= RET_DK if half == 1 else lane < RET_DK, q[rows, slab], 0.0)
            k128 = k[rows, slab]
            kb = k128.astype(BF16)
            vb = v[rows, vsl].astype(BF16)
            if has_state:
                q_f = qm * jnp.exp((t_full + 1.0) * lgf)
                q_b = qm * jnp.exp((seq - t_full) * lgb)
                o_state = (_bdot(q_f, s0f_ref[hd // 2]) + _bdot(q_b, s0b_ref[hd // 2]))
            else:
                k_f = k128 * jnp.exp((seq - 1.0 - t_full) * lgf)
                k_b = k128 * jnp.exp(t_full * lgb)
                s_f = _bdot(k_f.T, vb)
                s_b = _bdot(k_b.T, vb)
                sf_ref[g, hd] = s_f[half * RET_DK:(half + 1) * RET_DK]
                sb_ref[g, hd] = s_b[half * RET_DK:(half + 1) * RET_DK]
            for qi in range(n_q):
                qrows = slice(qi * tq, (qi + 1) * tq)
                s = _bdot_nt(qm[qrows], kb)
                off = (n_q - 1 - qi) * tq
                dec = dec_s[hd, :, off:off + seq]
                o = jnp.dot((s * dec).astype(BF16), vb, preferred_element_type=F32)
                if has_state:
                    o = o + o_state[qrows]
                r0 = g * seq + qi * tq
                y = _rms(o, None) * _silu(rg[r0:r0 + tq, vsl])
                y_ref[r0:r0 + tq, vsl] = y.astype(BF16)


def _ret_call(h, w_ret, layer, dec_f, dec_b, group, seq, state=None, name="ret"):
    nt, tm = h.shape[0], h.shape[1]
    v_w = RET_HEADS * RET_DV
    const2 = lambda a: pl.BlockSpec(a.shape, lambda i: (0, 0))
    in_specs = [
        pl.BlockSpec((None, tm, D_MODEL), lambda i: (i, 0, 0)),
        pl.BlockSpec((None, D_MODEL, w_ret.shape[2]), lambda i: (layer, 0, 0)),
        const2(dec_f), const2(dec_b),
    ]
    args = [h, w_ret, dec_f, dec_b]
    y_spec = pl.BlockSpec((None, tm, v_w), lambda i: (i, 0, 0))
    y_shape = jax.ShapeDtypeStruct((nt, tm, v_w), BF16)
    if state is None:
        s_spec = pl.BlockSpec((group, RET_HEADS, RET_DK, RET_DV), lambda i: (i, 0, 0, 0))
        s_shape = jax.ShapeDtypeStruct((nt * group, RET_HEADS, RET_DK, RET_DV), F32)
        out_specs = [y_spec, s_spec, s_spec]
        out_shape = [y_shape, s_shape, s_shape]
    else:
        s0f, s0b = state
        sspec = pl.BlockSpec((None, None, 2, LANE, RET_DV), lambda i: (i, layer, 0, 0, 0))
        in_specs += [sspec, sspec]
        args += [s0f, s0b]
        out_specs = [y_spec]
        out_shape = [y_shape]
    return pl.pallas_call(
        functools.partial(_ret_kernel, group=group, seq=seq, has_state=state is not None),
        grid=(nt,),
        in_specs=in_specs,
        out_specs=out_specs,
        out_shape=out_shape,
        scratch_shapes=[pltpu.VMEM((RET_HEADS, min(seq, ATT_Q_TILE), 2 * seq - min(seq, ATT_Q_TILE)), F32)],
        compiler_params=_cparams(("arbitrary",)),
        name=name,
    )(*args)


def _merge_kernel(x_ref, h_ref, mod_ref, y0_ref, y1_ref, y2_ref, y3_ref, wg_ref, wb_ref, wo_ref,
                  o_ref, merged_s):
    h = h_ref[...]
    ys = [r[...] for r in (y0_ref, y1_ref, y2_ref, y3_ref)]
    for c in range(D_MODEL // MERGE_TILE):
        cols = slice(c * MERGE_TILE, (c + 1) * MERGE_TILE)
        acc = None
        for b in range(N_BRANCH):
            gate = jax.nn.sigmoid(jnp.dot(h, wg_ref[:, b * D_MODEL + c * MERGE_TILE:b * D_MODEL + (c + 1) * MERGE_TILE],
                                          preferred_element_type=F32))
            part = gate * jnp.dot(ys[b], wb_ref[b, :, cols], preferred_element_type=F32)
            acc = part if acc is None else acc + part
        merged_s[:, cols] = acc.astype(BF16)
    mixed = jnp.dot(merged_s[...], wo_ref[...], preferred_element_type=F32)
    o_ref[...] = x_ref[...] + mod_ref[5:6] * mixed


def _merge_call(x, h, mod, mod_row, ys, w_gate, w_branch, w_out, layer, name):
    nt, tm, _ = x.shape
    tok = pl.BlockSpec((None, tm, D_MODEL), lambda i: (i, 0, 0))
    ysp = pl.BlockSpec((None, tm, BRANCH_W), lambda i: (i, 0, 0))
    resident = pl.Buffered(1)
    return pl.pallas_call(
        _merge_kernel,
        grid=(nt,),
        in_specs=[
            tok, tok,
            pl.BlockSpec((None, N_MOD, D_MODEL), lambda i: (mod_row(i), 0, 0)),
            ysp, ysp, ysp, ysp,
            pl.BlockSpec((None, D_MODEL, N_BRANCH * D_MODEL), lambda i: (layer, 0, 0), pipeline_mode=resident),
            pl.BlockSpec((None, N_BRANCH, BRANCH_W, D_MODEL), lambda i: (layer, 0, 0, 0), pipeline_mode=resident),
            pl.BlockSpec((None, D_MODEL, D_MODEL), lambda i: (layer, 0, 0), pipeline_mode=resident),
        ],
        out_specs=tok,
        out_shape=jax.ShapeDtypeStruct(x.shape, F32),
        scratch_shapes=[pltpu.VMEM((tm, D_MODEL), BF16)],
        compiler_params=_cparams(("parallel",)),
        name=name,
    )(x, h, mod, *ys, w_gate, w_branch, w_out)


def _dft_tables(seq):
    n_fft = 2 * seq
    idx = np.arange(seq, dtype=np.int64)
    ang = 2.0 * np.pi * ((idx[:, None] * idx[None, :]) % n_fft).astype(np.float64) / n_fft
    out = []
    for m in (np.cos(ang), np.sin(ang)):
        hi = jnp.asarray(m, F32).astype(BF16)
        lo = (jnp.asarray(m, F32) - hi.astype(F32)).astype(BF16)
        out += [hi, lo]
    return out


def _hyena_tables(seq):
    t = np.arange(seq, dtype=np.float32) / np.float32(seq)
    f = np.arange(1, HY_BANDS + 1, dtype=np.float32)
    ang = np.float32(2.0 * math.pi) * t[:, None] * f[None, :]
    feats = np.zeros((seq, LANE), np.float32)
    feats[:, 0] = t
    feats[:, 1:1 + HY_BANDS] = np.sin(ang)
    feats[:, 1 + HY_BANDS:HY_EMB] = np.cos(ang)
    min_decay = math.log(HY_TARGET) / HY_DECAY_LONG_PCT
    max_decay = math.log(HY_TARGET) / HY_DECAY_SHORT_PCT
    deltas = np.linspace(min_decay, max_decay, HY_W, dtype=np.float32)
    window = np.exp(-t[:, None] * np.abs(deltas)[None, :]).astype(np.float32)
    return jnp.asarray(feats), jnp.asarray(window)


def _rope_tables(seq):
    pos = np.arange(seq)
    row = (pos // GRID_W).astype(np.float32)
    col = (pos % GRID_W).astype(np.float32)
    inv_freq = (np.float32(ROPE_BASE) ** (-np.arange(ROPE_PAIRS, dtype=np.float32) / np.float32(ROPE_PAIRS)))
    lane = np.arange(LANE)
    in_head = lane % HEAD_DIM
    use_col = in_head >= HEAD_DIM // 2
    pair = in_head % ROPE_PAIRS
    second = (in_head % (2 * ROPE_PAIRS)) >= ROPE_PAIRS
    p = np.where(use_col[None, :], col[:, None], row[:, None]).astype(np.float32)
    ang = (p * inv_freq[pair][None, :]).astype(np.float32)
    cos = np.cos(ang).astype(np.float32)
    sin = np.sin(ang).astype(np.float32)
    sin = np.where(second[None, :], sin, -sin)
    return jnp.asarray(cos), jnp.asarray(sin)


def _block_diag_ones(width, chunk):
    idx = np.arange(width) // chunk
    return jnp.asarray((idx[:, None] == idx[None, :]).astype(np.float32)).astype(BF16)


def kernel(x_prompt, x_sample, c, cache_diff_k, cache_diff_v, cache_win_k, cache_win_v, state_ret_f, state_ret_b, c_ctx, norm_ffa, norm_mix, norm_ffb, w_ada, b_ada, w_ffa_in, w_ffa_out, w_ffb_in, w_ffb_out, w_in, hy_conv_w, hy_conv_b, hy_f_w1, hy_f_b1, hy_f_w2, hy_f_b2, hy_f_w3, hy_skip, diff_q_norm, diff_k_norm, diff_lambda, diff_subln, win_q_norm, win_k_norm, win_sink, ret_decay_f, ret_decay_b, w_branch, w_out):
    batch, seq, _ = x_prompt.shape
    dec_batch, dec_seq, _ = x_sample.shape
    past = cache_diff_k.shape[2]
    ctx_group = TOKEN_TILE // seq
    assert TOKEN_TILE % seq == 0 and batch % ctx_group == 0 and dec_seq == TOKEN_TILE
    n_ctx = batch // ctx_group

    n_rows = 16
    cond = jnp.zeros((n_rows, D_MODEL), F32).at[0].set(c_ctx).at[1:1 + dec_batch].set(c)
    mod = _ada_call(cond, w_ada, b_ada).reshape(DEPTH, n_rows, N_MOD, D_MODEL)

    bf = lambda a: a.astype(BF16)
    w_ffa_in_b, w_ffa_out_b, w_ffb_in_b, w_ffb_out_b = bf(w_ffa_in), bf(w_ffa_out), bf(w_ffb_in), bf(w_ffb_out)
    w_hy = bf(w_in[:, :, COL_HY:COL_DIFF])
    w_diff = bf(w_in[:, :, COL_DIFF:COL_WIN])
    w_win = bf(w_in[:, :, COL_WIN:COL_RET])
    w_ret = bf(w_in[:, :, COL_RET:COL_GATE])
    w_gate = bf(w_in[:, :, COL_GATE:])
    w_branch_b, w_out_b = bf(w_branch), bf(w_out)

    bd64 = _block_diag_ones(2 * DIFF_HEADS * HEAD_DIM, HEAD_DIM)
    rope = _rope_tables(dec_seq)
    tables = {s: (_hyena_tables(s), _dft_tables(s)) for s in (seq, dec_seq)}

    ck_d = cache_diff_k.reshape(dec_batch, DEPTH, past, -1)
    cv_d = cache_diff_v.reshape(dec_batch, DEPTH, past, -1)
    ck_w = cache_win_k.reshape(dec_batch, DEPTH, past, -1)
    cv_w = cache_win_v.reshape(dec_batch, DEPTH, past, -1)
    s0_f = state_ret_f.reshape(dec_batch, DEPTH, 2, LANE, RET_DV)
    s0_b = state_ret_b.reshape(dec_batch, DEPTH, 2, LANE, RET_DV)

    pad_rows = lambda a, n: jnp.pad(a, ((0, n - a.shape[0]), (0, 0)))
    tile_lanes = lambda a, n: jnp.tile(a.reshape(1, -1), (1, n))

    dense = lambda a: a.reshape(-1, DENSE_TILE, a.shape[-1])
    y_ctx = x_prompt
    y_lat = x_sample
    ctx_row = lambda i: 0
    lat_row = lambda i: i // (dec_seq // DENSE_TILE) + 1
    st = {k: [] for k in ("dk", "dv", "wk", "wv", "rf", "rb")}

    for l in range(DEPTH):
        lam_init = 0.8 - 0.6 * math.exp(-0.3 * l)
        gain = lambda a: a[l].reshape(1, -1)
        gq_d, gk_d = tile_lanes(diff_q_norm[l], 2 * DIFF_HEADS), tile_lanes(diff_k_norm[l], 2 * DIFF_HEADS)
        gs_d = tile_lanes(diff_subln[l], DIFF_HEADS)
        gq_w, gk_w = tile_lanes(win_q_norm[l], WIN_Q_HEADS), tile_lanes(win_k_norm[l], WIN_KV_HEADS)
        dec_f = jnp.broadcast_to(ret_decay_f[l][:, None], (RET_HEADS, LANE))
        dec_b = jnp.broadcast_to(ret_decay_b[l][:, None], (RET_HEADS, LANE))
        w1 = jnp.pad(hy_f_w1[l], ((0, LANE - HY_EMB), (0, LANE - HY_FH)))
        b1 = jnp.pad(hy_f_b1[l].reshape(1, -1), ((0, 0), (0, LANE - HY_FH)))
        w2 = jnp.pad(hy_f_w2[l], ((0, LANE - HY_FH), (0, LANE - HY_FH)))
        b2 = jnp.pad(hy_f_b2[l].reshape(1, -1), ((0, 0), (0, LANE - HY_FH)))
        w3 = pad_rows(hy_f_w3[l], LANE)
        cw, cb, skip = hy_conv_w[l], hy_conv_b[l].reshape(1, -1), hy_skip[l].reshape(1, -1)

        for is_lat in (False, True):
            x = y_lat if is_lat else y_ctx
            s_len = dec_seq if is_lat else seq
            group = 1 if is_lat else ctx_group
            mod_row = lat_row if is_lat else ctx_row
            tag = f"{'lat' if is_lat else 'ctx'}{l}"
            (feats, window), (c_hi, c_lo, s_hi, s_lo) = tables[s_len]

            x, h = _ffn_call(dense(x), mod[l], mod_row, gain(norm_ffa), gain(norm_mix), w_ffa_in_b, w_ffa_out_b,
                             l, 0, f"ffn_a_{tag}")
            h = h.reshape(-1, TOKEN_TILE, D_MODEL)

            kr, ki, kn = _hy_filter_call(s_len, feats, w1, b1, w2, b2, w3, window, c_hi, c_lo, s_hi, s_lo)
            y_hy = _hyena_call(h, w_hy, l, cw, cb, kr, ki, kn, skip, c_hi, s_hi, group, s_len, f"hyena_{tag}")
            if is_lat:
                (y_diff,) = _diff_call(h, w_diff, l, gq_d, gk_d, diff_lambda[l], gs_d, bd64, lam_init,
                                       group, s_len, rope=rope, cache=(ck_d, cv_d), name=f"diff_{tag}")
                (y_win,) = _win_call(h, w_win, l, win_sink, gq_w, gk_w, bd64, group, s_len,
                                     rope=rope, cache=(ck_w, cv_w), name=f"win_{tag}")
                (y_ret,) = _ret_call(h, w_ret, l, dec_f, dec_b, group, s_len, state=(s0_f, s0_b),
                                     name=f"ret_{tag}")
            else:
                y_diff, k_d, v_d = _diff_call(h, w_diff, l, gq_d, gk_d, diff_lambda[l], gs_d, bd64,
                                              lam_init, group, s_len, name=f"diff_{tag}")
                y_win, k_w, v_w = _win_call(h, w_win, l, win_sink, gq_w, gk_w, bd64, group, s_len,
                                            name=f"win_{tag}")
                y_ret, s_f, s_b = _ret_call(h, w_ret, l, dec_f, dec_b, group, s_len, name=f"ret_{tag}")
                st["dk"].append(k_d.reshape(batch, seq, DIFF_HEADS, 2, HEAD_DIM))
                st["dv"].append(v_d.reshape(batch, seq, DIFF_HEADS, 2 * HEAD_DIM))
                st["wk"].append(k_w.reshape(batch, seq, WIN_KV_HEADS, HEAD_DIM))
                st["wv"].append(v_w.reshape(batch, seq, WIN_KV_HEADS, HEAD_DIM))
                st["rf"].append(s_f)
                st["rb"].append(s_b)

            x = _merge_call(x, dense(h), mod[l], mod_row, tuple(dense(y) for y in (y_hy, y_diff, y_win, y_ret)),
                            w_gate, w_branch_b, w_out_b, l, f"merge_{tag}")
            x, _ = _ffn_call(x, mod[l], mod_row, gain(norm_ffb), None, w_ffb_in_b, w_ffb_out_b,
                             l, 6, f"ffn_b_{tag}")
            if is_lat:
                y_lat = x
            else:
                y_ctx = x

    stack = lambda xs: jnp.stack(xs, axis=1)
    return (y_ctx.reshape(batch, seq, D_MODEL), y_lat.reshape(dec_batch, dec_seq, D_MODEL),
            stack(st["dk"]), stack(st["dv"]), stack(st["wk"]),
            stack(st["wv"]), stack(st["rf"]), stack(st["rb"]))
```

```python
import functools
import math

import jax
import jax.numpy as jnp
import numpy as np
from jax import lax
from jax.experimental import pallas as pl
from jax.experimental.pallas import tpu as pltpu

F32 = jnp.float32
BF16 = jnp.bfloat16

D_MODEL = 1024
DEPTH = 2
GRID_W = 64
HEAD_DIM = 64
ROPE_PAIRS = HEAD_DIM // 4
ROPE_BASE = 10000.0
EPS = 1e-6
NEG_INF = -1e30
LOG2_E = math.log2(math.e)
D_FF = 2816
N_MOD = 9
N_BRANCH = 4
BRANCH_W = 512

HY_W = BRANCH_W
HY_BANDS = 16
HY_EMB = 1 + 2 * HY_BANDS
HY_FH = 64
HY_SIN_W = 1.0
HY_TARGET = 1e-2
HY_DECAY_SHORT_PCT = 0.3
HY_DECAY_LONG_PCT = 1.5

DIFF_HEADS = 4
WIN_Q_HEADS = 8
WIN_KV_HEADS = 2
WIN_GROUP = WIN_Q_HEADS // WIN_KV_HEADS
WINDOW = 128
BLOCK = 128
RET_HEADS = 4
RET_DK = 64
RET_DV = 128

COL_HY = 0
COL_DIFF = 3 * HY_W
COL_WIN = COL_DIFF + 3 * 512
COL_RET = COL_WIN + 512 + 128 + 128
COL_GATE = COL_RET + 256 + 256 + 512 + 512
IN_COLS = COL_GATE + N_BRANCH * D_MODEL

LANE = 128
TOKEN_TILE = 1024
DENSE_TILE = 512
FF_TILE = 256
MERGE_TILE = 256
ADA_TILE = 1152
ATT_Q_TILE = 256
VMEM_LIMIT = 56 * 1024 * 1024


def _cparams(sem):
    return pltpu.CompilerParams(dimension_semantics=sem, vmem_limit_bytes=VMEM_LIMIT)


def _bdot(a, b):
    return jnp.dot(a.astype(BF16), b.astype(BF16), preferred_element_type=F32)


def _bdot_nt(a, b):
    return lax.dot_general(a.astype(BF16), b.astype(BF16), (((1,), (1,)), ((), ())),
                           preferred_element_type=F32)


def _split(a):
    hi = a.astype(BF16)
    lo = (a - hi.astype(F32)).astype(BF16)
    return hi, lo


def _dot3(a, b):
    ah, al = _split(a)
    bh, bl = _split(b)
    d = functools.partial(jnp.dot, preferred_element_type=F32)
    return d(ah, bh) + d(ah, bl) + d(al, bh)


def _dot3_pre(ah, al, b):
    bh, bl = _split(b)
    d = functools.partial(jnp.dot, preferred_element_type=F32)
    return d(ah, bh) + d(ah, bl) + d(al, bh)


def _chunk_mean(sq, bd, n):
    hi, lo = _split(sq)
    d = functools.partial(jnp.dot, preferred_element_type=F32)
    return (d(hi, bd) + d(lo, bd)) * (1.0 / n)


def _rms(x, gain):
    y = x * lax.rsqrt(jnp.mean(x * x, axis=-1, keepdims=True) + EPS)
    return y if gain is None else y * gain


def _silu(x):
    return x * jax.nn.sigmoid(x)


def _lane(shape):
    return lax.broadcasted_iota(jnp.int32, shape, 1)


def _row(shape):
    return lax.broadcasted_iota(jnp.int32, shape, 0)


def _rope128(x, cos, sin):
    partner = jnp.where((_lane(x.shape) & 31) < 16,
                        pltpu.roll(x, LANE - ROPE_PAIRS, axis=1),
                        pltpu.roll(x, ROPE_PAIRS, axis=1))
    return x * cos + partner * sin


def _cast_weights_once(w_refs, w_s):
    @pl.when(pl.program_id(0) == 0)
    def _():
        off = 0
        for r in w_refs:
            w_s[:, off:off + r.shape[1]] = r[...].astype(BF16)
            off += r.shape[1]


def _w_in_spec(layer, col, width):
    assert col % width == 0
    return pl.BlockSpec((None, D_MODEL, width), lambda i: (layer, 0, col // width),
                        pipeline_mode=pl.Buffered(1))


def _log_gamma(decay):
    x = -decay
    return -(jnp.maximum(x, 0.0) + jnp.log1p(jnp.exp(-jnp.abs(x))))


def _ada_kernel(cond_ref, w_ref, b_ref, o_ref):
    s = _silu(cond_ref[...])
    o_ref[...] = _bdot(s, w_ref[...]) + b_ref[...]


def _ada_call(cond, w_ada, b_ada):
    rows = cond.shape[0]
    n_out = N_MOD * D_MODEL
    return pl.pallas_call(
        _ada_kernel,
        grid=(DEPTH, n_out // ADA_TILE),
        in_specs=[
            pl.BlockSpec((rows, D_MODEL), lambda l, j: (0, 0)),
            pl.BlockSpec((None, D_MODEL, ADA_TILE), lambda l, j: (l, 0, j)),
            pl.BlockSpec((None, 1, ADA_TILE), lambda l, j: (l, 0, j)),
        ],
        out_specs=pl.BlockSpec((None, rows, ADA_TILE), lambda l, j: (l, 0, j)),
        out_shape=jax.ShapeDtypeStruct((DEPTH, rows, n_out), F32),
        compiler_params=_cparams(("parallel", "parallel")),
        name="ada_mod",
    )(cond, w_ada, b_ada.reshape(DEPTH, 1, n_out))


def _ffn_kernel(*refs, mod_base, emit_next):
    if emit_next:
        x_ref, mod_ref, gain_ref, gain2_ref, wi_ref, wo_ref, y_ref, h2_ref, act_s = refs
    else:
        x_ref, mod_ref, gain_ref, wi_ref, wo_ref, y_ref, act_s = refs
    m = mod_ref[...]
    x = x_ref[...]
    h = (_rms(x, gain_ref[...]) * (1.0 + m[mod_base + 1:mod_base + 2]) + m[mod_base:mod_base + 1]).astype(BF16)
    for c in range(D_FF // FF_TILE):
        a = jnp.dot(h, wi_ref[:, c * FF_TILE:(c + 1) * FF_TILE], preferred_element_type=F32)
        g = jnp.dot(h, wi_ref[:, D_FF + c * FF_TILE:D_FF + (c + 1) * FF_TILE], preferred_element_type=F32)
        act_s[:, c * FF_TILE:(c + 1) * FF_TILE] = (_silu(a) * g).astype(BF16)
    out = jnp.dot(act_s[...], wo_ref[...], preferred_element_type=F32)
    y = x + 0.5 * m[mod_base + 2:mod_base + 3] * out
    y_ref[...] = y
    if emit_next:
        h2 = _rms(y, gain2_ref[...]) * (1.0 + m[mod_base + 4:mod_base + 5]) + m[mod_base + 3:mod_base + 4]
        h2_ref[...] = h2.astype(BF16)


def _ffn_call(x, mod, mod_row, gain, gain2, w_in, w_out, layer, mod_base, name):
    nt, tm, _ = x.shape
    emit_next = gain2 is not None
    tok = pl.BlockSpec((None, tm, D_MODEL), lambda i: (i, 0, 0))
    vec = pl.BlockSpec((1, D_MODEL), lambda i: (0, 0))
    in_specs = [tok, pl.BlockSpec((None, N_MOD, D_MODEL), lambda i: (mod_row(i), 0, 0)), vec]
    args = [x, mod, gain]
    if emit_next:
        in_specs.append(vec)
        args.append(gain2)
    in_specs += [
        pl.BlockSpec((None, D_MODEL, 2 * D_FF), lambda i: (layer, 0, 0), pipeline_mode=pl.Buffered(1)),
        pl.BlockSpec((None, D_FF, D_MODEL), lambda i: (layer, 0, 0), pipeline_mode=pl.Buffered(1)),
    ]
    args += [w_in, w_out]
    out_specs = [tok]
    out_shape = [jax.ShapeDtypeStruct(x.shape, F32)]
    if emit_next:
        out_specs.append(tok)
        out_shape.append(jax.ShapeDtypeStruct(x.shape, BF16))
    res = pl.pallas_call(
        functools.partial(_ffn_kernel, mod_base=mod_base, emit_next=emit_next),
        grid=(nt,),
        in_specs=in_specs,
        out_specs=out_specs,
        out_shape=out_shape,
        scratch_shapes=[pltpu.VMEM((tm, D_FF), BF16)],
        compiler_params=_cparams(("parallel",)),
        name=name,
    )(*args)
    return res if emit_next else (res[0], None)


def _hy_filter_kernel(feats_ref, w1_ref, b1_ref, w2_ref, b2_ref, w3_ref, win_ref,
                      ch_ref, cl_ref, sh_ref, sl_ref, kr_ref, ki_ref, kn_ref, *, seq):
    n_fft = 2 * seq
    z = jnp.sin(HY_SIN_W * (_dot3(feats_ref[...], w1_ref[...]) + b1_ref[...]))
    z = jnp.sin(HY_SIN_W * (_dot3(z, w2_ref[...]) + b2_ref[...]))
    zz = _dot3(z, w3_ref[...])
    win = win_ref[...]
    hf = zz[:, :HY_W] * win
    hb = zz[:, HY_W:] * win
    norm = (jnp.sum(jnp.abs(hf), axis=0, keepdims=True)
            + jnp.sum(jnp.abs(hb), axis=0, keepdims=True))
    hf = hf / norm
    hb = hb / norm
    row = _row(hf.shape)
    hb0 = jnp.where(row == 0, 0.0, hb)
    even = hf + hb0
    odd = hb0 - hf
    wk = jnp.where(row == 0, 1.0 / n_fft, 2.0 / n_fft)
    kr_ref[...] = _dot3_pre(ch_ref[...], cl_ref[...], even) * wk
    ki_ref[...] = _dot3_pre(sh_ref[...], sl_ref[...], odd) * wk
    sgn = jnp.where((row & 1) == 0, 1.0, -1.0)
    kn_ref[...] = jnp.sum(even * sgn, axis=0, keepdims=True) * (1.0 / n_fft)


def _hy_filter_call(seq, feats, w1, b1, w2, b2, w3, win, ch, cl, sh, sl):
    args = (feats, w1, b1, w2, b2, w3, win, ch, cl, sh, sl)
    return pl.pallas_call(
        functools.partial(_hy_filter_kernel, seq=seq),
        out_shape=[jax.ShapeDtypeStruct((seq, HY_W), F32),
                   jax.ShapeDtypeStruct((seq, HY_W), F32),
                   jax.ShapeDtypeStruct((1, HY_W), F32)],
        compiler_params=pltpu.CompilerParams(vmem_limit_bytes=VMEM_LIMIT),
        name=f"hyena_filter_{seq}",
    )(*args)


def _hyena_kernel(h_ref, w_ref, cw_ref, cb_ref, kr_ref, ki_ref, kn_ref, skip_ref,
                  c_ref, s_ref, y_ref, w_s, *, group, seq):
    _cast_weights_once((w_ref,), w_s)
    proj = jnp.dot(h_ref[...], w_s[...], preferred_element_type=F32)
    cw = cw_ref[...]
    cmat = c_ref[...]
    smat = s_ref[...]
    kr = kr_ref[...]
    ki = ki_ref[...]
    row = _row((seq, 3 * HY_W))
    row_w = _row((seq, HY_W))
    sgn = jnp.where((row_w & 1) == 0, 1.0, -1.0)
    for g in range(group):
        hy = proj[g * seq:(g + 1) * seq]
        prev = jnp.where(row == 0, 0.0, pltpu.roll(hy, 1, axis=0))
        nxt = jnp.where(row == seq - 1, 0.0, pltpu.roll(hy, seq - 1, axis=0))
        u = prev * cw[0:1] + hy * cw[1:2] + nxt * cw[2:3] + cb_ref[...]
        v, x0, x1 = u[:, :HY_W], u[:, HY_W:2 * HY_W], u[:, 2 * HY_W:]
        z = v * x1
        zb = z.astype(BF16)
        zr = jnp.dot(cmat, zb, preferred_element_type=F32)
        zs = jnp.dot(smat, zb, preferred_element_type=F32)
        yr = (zr * kr + zs * ki).astype(BF16)
        yi = (zr * ki - zs * kr).astype(BF16)
        nyq = jnp.sum(z * sgn, axis=0, keepdims=True) * kn_ref[...]
        conv = (jnp.dot(cmat, yr, preferred_element_type=F32)
                - jnp.dot(smat, yi, preferred_element_type=F32) + sgn * nyq)
        y_ref[g * seq:(g + 1) * seq, :] = (x0 * (conv + skip_ref[...] * z)).astype(BF16)


def _hyena_call(h, w_in, layer, cw, cb, kr, ki, kn, skip, cmat, smat, group, seq, name):
    nt, tm, _ = h.shape
    const2 = lambda a: pl.BlockSpec(a.shape, lambda i: (0, 0))
    return pl.pallas_call(
        functools.partial(_hyena_kernel, group=group, seq=seq),
        grid=(nt,),
        in_specs=[
            pl.BlockSpec((None, tm, D_MODEL), lambda i: (i, 0, 0)),
            _w_in_spec(layer, COL_HY, 3 * HY_W),
            const2(cw), const2(cb), const2(kr), const2(ki), const2(kn), const2(skip),
            const2(cmat), const2(smat),
        ],
        out_specs=pl.BlockSpec((None, tm, HY_W), lambda i: (i, 0, 0)),
        out_shape=jax.ShapeDtypeStruct((nt, tm, HY_W), BF16),
        scratch_shapes=[pltpu.VMEM((D_MODEL, 3 * HY_W), BF16)],
        compiler_params=_cparams(("arbitrary",)),
        name=name,
    )(h, w_in, cw, cb, kr, ki, kn, skip, cmat, smat)


def _diff_kernel(*refs, group, seq, past, lam_init):
    has_cache = past > 0
    if has_cache:
        (h_ref, w_ref, gq_ref, gk_ref, lam_ref, gs_ref, bd_ref, cos_ref, sin_ref,
         ck_ref, cv_ref, y_ref, w_s, k_s, v_s) = refs
    else:
        (h_ref, w_ref, gq_ref, gk_ref, lam_ref, gs_ref, bd_ref, y_ref, ko_ref, vo_ref, w_s) = refs
    width = 2 * DIFF_HEADS * HEAD_DIM
    _cast_weights_once((w_ref,), w_s)
    proj = jnp.dot(h_ref[...], w_s[...], preferred_element_type=F32)
    q, k, v = proj[:, :width], proj[:, width:2 * width], proj[:, 2 * width:]
    bd = bd_ref[...]
    q = q * lax.rsqrt(_chunk_mean(q * q, bd, HEAD_DIM) + EPS) * gq_ref[...]
    k = k * lax.rsqrt(_chunk_mean(k * k, bd, HEAD_DIM) + EPS) * gk_ref[...]
    dl = lam_ref[...]
    lam = (jnp.exp(jnp.sum(dl[0:1] * dl[1:2], axis=1, keepdims=True))
           - jnp.exp(jnp.sum(dl[2:3] * dl[3:4], axis=1, keepdims=True)) + lam_init)
    ones = jnp.ones((seq, LANE), BF16)
    if has_cache:
        cos, sin = cos_ref[...], sin_ref[...]
        q = jnp.concatenate([_rope128(q[:, s * LANE:(s + 1) * LANE], cos, sin)
                             for s in range(width // LANE)], axis=1)
        k = jnp.concatenate([_rope128(k[:, s * LANE:(s + 1) * LANE], cos, sin)
                             for s in range(width // LANE)], axis=1)
        k_s[0:seq, :] = k.astype(BF16)
        k_s[seq:seq + past, :] = ck_ref[...].astype(BF16)
        for hh in range(DIFF_HEADS):
            v_s[0:seq, 2 * hh * LANE:(2 * hh + 1) * LANE] = v[:, hh * LANE:(hh + 1) * LANE].astype(BF16)
            v_s[seq:seq + past, 2 * hh * LANE:(2 * hh + 1) * LANE] = (
                cv_ref[:, hh * LANE:(hh + 1) * LANE].astype(BF16))
            v_s[:, (2 * hh + 1) * LANE:(2 * hh + 2) * LANE] = jnp.ones((seq + past, LANE), BF16)
    else:
        ko_ref[...] = k
        vo_ref[...] = v
    q = q * (HEAD_DIM ** -0.5 * LOG2_E)
    tq = min(seq, ATT_Q_TILE)
    lane = _lane((tq, LANE))
    for g in range(group):
        for hh in range(DIFF_HEADS):
            sl = slice(hh * LANE, (hh + 1) * LANE)
            if has_cache:
                keys, vals = k_s[:, sl], v_s[:, 2 * hh * LANE:(2 * hh + 2) * LANE]
            else:
                keys = k[g * seq:(g + 1) * seq, sl].astype(BF16)
                vals = jnp.concatenate([v[g * seq:(g + 1) * seq, sl].astype(BF16), ones], axis=1)
            for qi in range(seq // tq):
                r0 = g * seq + qi * tq
                q128 = q[r0:r0 + tq, sl]
                qs = jnp.concatenate([jnp.where(lane < HEAD_DIM, q128, 0.0),
                                      jnp.where(lane >= HEAD_DIM, q128, 0.0)], axis=0)
                s = _bdot_nt(qs, keys)
                e = jnp.exp2(s - jnp.max(s, axis=-1, keepdims=True)).astype(BF16)
                oe = jnp.dot(e, vals, preferred_element_type=F32)
                r = 1.0 / oe[:, LANE:LANE + 1]
                o = oe[:tq, :LANE] * r[:tq] - oe[tq:, :LANE] * (lam * r[tq:])
                y = _rms(o, gs_ref[:, sl]) * (1.0 - lam_init)
                y_ref[r0:r0 + tq, sl] = y.astype(BF16)


def _diff_call(h, w_in, layer, gq, gk, lam, gs, bd, lam_init, group, seq,
               rope=None, cache=None, name="diff"):
    nt, tm = h.shape[0], h.shape[1]
    width = 2 * DIFF_HEADS * HEAD_DIM
    const2 = lambda a: pl.BlockSpec(a.shape, lambda i: (0, 0))
    in_specs = [
        pl.BlockSpec((None, tm, D_MODEL), lambda i: (i, 0, 0)),
        _w_in_spec(layer, COL_DIFF, 3 * width),
        const2(gq), const2(gk), const2(lam), const2(gs), const2(bd),
    ]
    args = [h, w_in, gq, gk, lam, gs, bd]
    y_spec = pl.BlockSpec((None, tm, width), lambda i: (i, 0, 0))
    y_shape = jax.ShapeDtypeStruct((nt, tm, width), BF16)
    scratch = [pltpu.VMEM((D_MODEL, 3 * width), BF16)]
    if cache is None:
        past = 0
        out_specs = [y_spec, y_spec, y_spec]
        out_shape = [y_shape, jax.ShapeDtypeStruct((nt, tm, width), F32),
                     jax.ShapeDtypeStruct((nt, tm, width), F32)]
    else:
        cos, sin = rope
        ck, cv = cache
        past = ck.shape[2]
        cspec = pl.BlockSpec((None, None, past, width), lambda i: (i, layer, 0, 0))
        in_specs += [const2(cos), const2(sin), cspec, cspec]
        args += [cos, sin, ck, cv]
        out_specs = [y_spec]
        out_shape = [y_shape]
        scratch += [pltpu.VMEM((seq + past, width), BF16), pltpu.VMEM((seq + past, 2 * width), BF16)]
    return pl.pallas_call(
        functools.partial(_diff_kernel, group=group, seq=seq, past=past, lam_init=lam_init),
        grid=(nt,),
        in_specs=in_specs,
        out_specs=out_specs,
        out_shape=out_shape,
        scratch_shapes=scratch,
        compiler_params=_cparams(("arbitrary",)),
        name=name,
    )(*args)


def _win_heads(q, hk, lane):
    out = []
    for gq in range(WIN_GROUP):
        j = hk * WIN_GROUP + gq
        slab = q[:, (j // 2) * LANE:(j // 2 + 1) * LANE]
        if j % 2 != hk:
            slab = pltpu.roll(slab, HEAD_DIM, axis=1)
        out.append(jnp.where(lane >= HEAD_DIM if hk == 1 else lane < HEAD_DIM, slab, 0.0))
    return out


def _win_place(o_heads, hk, lane):
    slabs = []
    for pair in range(WIN_GROUP // 2):
        halves = []
        for gq in (2 * pair, 2 * pair + 1):
            j = hk * WIN_GROUP + gq
            o = o_heads[gq]
            if j % 2 != hk:
                o = pltpu.roll(o, HEAD_DIM, axis=1)
            halves.append(o)
        slabs.append(jnp.where(lane < HEAD_DIM, halves[0], halves[1]))
    return slabs


def _sink_col(sink_ref, layer, hk, rows_per_head):
    rows = WIN_GROUP * rows_per_head
    r = _row((rows, 1))
    col = jnp.full((rows, 1), sink_ref[layer, hk * WIN_GROUP], F32)
    for gq in range(1, WIN_GROUP):
        col = jnp.where(r >= gq * rows_per_head, sink_ref[layer, hk * WIN_GROUP + gq], col)
    return col


def _win_kernel(*refs, group, seq, past, layer):
    has_cache = past > 0
    if has_cache:
        (sink_ref, h_ref, w_ref, gq_ref, gk_ref, bd_ref, cos_ref, sin_ref, ck_ref, cv_ref,
         y_ref, w_s, k_s, v_s) = refs
    else:
        (sink_ref, h_ref, w_ref, gq_ref, gk_ref, bd_ref, y_ref, ko_ref, vo_ref, w_s) = refs
    qw = WIN_Q_HEADS * HEAD_DIM
    kw = WIN_KV_HEADS * HEAD_DIM
    _cast_weights_once((w_ref,), w_s)
    proj = jnp.dot(h_ref[...], w_s[...], preferred_element_type=F32)
    q, k, v = proj[:, :qw], proj[:, qw:qw + kw], proj[:, qw + kw:]
    bd = bd_ref[...]
    q = q * lax.rsqrt(_chunk_mean(q * q, bd, HEAD_DIM) + EPS) * gq_ref[...]
    k = k * lax.rsqrt(_chunk_mean(k * k, bd[:kw, :kw], HEAD_DIM) + EPS) * gk_ref[...]
    scale = HEAD_DIM ** -0.5 * LOG2_E
    if not has_cache:
        ko_ref[...] = k
        vo_ref[...] = v
        q = q * scale
        lane = _lane((seq, LANE))
        ones = jnp.ones((seq, LANE), BF16)
        for g in range(group):
            rows = slice(g * seq, (g + 1) * seq)
            kb = k[rows].astype(BF16)
            vb = jnp.concatenate([v[rows].astype(BF16), ones], axis=1)
            qg = q[rows]
            for hk in range(WIN_KV_HEADS):
                qs = jnp.concatenate(_win_heads(qg, hk, lane), axis=0)
                s = _bdot_nt(qs, kb)
                sink = _sink_col(sink_ref, layer, hk, seq) * LOG2_E
                m = jnp.maximum(jnp.max(s, axis=-1, keepdims=True), sink)
                oe = jnp.dot(jnp.exp2(s - m).astype(BF16), vb, preferred_element_type=F32)
                o = oe[:, :LANE] * (1.0 / (oe[:, LANE:LANE + 1] + jnp.exp2(sink - m)))
                slabs = _win_place([o[gq * seq:(gq + 1) * seq] for gq in range(WIN_GROUP)], hk, lane)
                for pair, slab in enumerate(slabs):
                    c0 = (hk * 2 + pair) * LANE
                    y_ref[rows, c0:c0 + LANE] = slab.astype(BF16)
        return

    cos, sin = cos_ref[...], sin_ref[...]
    q = jnp.concatenate([_rope128(q[:, s * LANE:(s + 1) * LANE], cos, sin)
                         for s in range(qw // LANE)], axis=1) * scale
    k = _rope128(k, cos, sin)
    zpad = jnp.zeros((BLOCK, kw), BF16)
    k_s[0:BLOCK, :] = zpad
    k_s[BLOCK:BLOCK + seq, :] = k.astype(BF16)
    k_s[BLOCK + seq:2 * BLOCK + seq, :] = zpad
    v_s[0:BLOCK, 0:kw] = zpad
    v_s[BLOCK:BLOCK + seq, 0:kw] = v.astype(BF16)
    v_s[BLOCK + seq:2 * BLOCK + seq, 0:kw] = zpad
    v_s[:, kw:2 * kw] = jnp.ones((seq + 2 * BLOCK, kw), BF16)
    ckb = ck_ref[...].astype(BF16)
    cvb = jnp.concatenate([cv_ref[...].astype(BF16), jnp.ones((past, kw), BF16)], axis=1)
    lane_l = _lane((seq, LANE))
    lane_b = _lane((BLOCK, LANE))
    nb = seq // BLOCK
    rows_q = WIN_GROUP * BLOCK
    kk = _lane((rows_q, 3 * BLOCK))
    qi = _row((rows_q, 3 * BLOCK)) & (BLOCK - 1)
    rel = kk - BLOCK - qi
    in_window = (rel <= WINDOW) & (rel >= -WINDOW)
    for hk in range(WIN_KV_HEADS):
        heads = _win_heads(q, hk, lane_l)
        sink = _sink_col(sink_ref, layer, hk, BLOCK) * LOG2_E
        for n in range(nb):
            qs = jnp.concatenate([hd[n * BLOCK:(n + 1) * BLOCK] for hd in heads], axis=0)
            tk = kk + (n - 1) * BLOCK
            valid = in_window & (tk >= 0) & (tk < seq)
            sb = jnp.where(valid, _bdot_nt(qs, k_s[n * BLOCK:(n + 3) * BLOCK, :]), NEG_INF)
            sc = _bdot_nt(qs, ckb)
            m = jnp.maximum(jnp.maximum(jnp.max(sb, axis=-1, keepdims=True),
                                        jnp.max(sc, axis=-1, keepdims=True)), sink)
            eb = jnp.exp2(sb - m).astype(BF16)
            ec = jnp.exp2(sc - m).astype(BF16)
            oe = (jnp.dot(eb, v_s[n * BLOCK:(n + 3) * BLOCK, :], preferred_element_type=F32)
                  + jnp.dot(ec, cvb, preferred_element_type=F32))
            o = oe[:, :LANE] * (1.0 / (oe[:, LANE:LANE + 1] + jnp.exp2(sink - m)))
            slabs = _win_place([o[gq * BLOCK:(gq + 1) * BLOCK] for gq in range(WIN_GROUP)], hk, lane_b)
            for pair, slab in enumerate(slabs):
                c0 = (hk * 2 + pair) * LANE
                y_ref[n * BLOCK:(n + 1) * BLOCK, c0:c0 + LANE] = slab.astype(BF16)


def _win_call(h, w_in, layer, sink, gq, gk, bd, group, seq, rope=None, cache=None, name="win"):
    nt, tm = h.shape[0], h.shape[1]
    qw = WIN_Q_HEADS * HEAD_DIM
    kw = WIN_KV_HEADS * HEAD_DIM
    const2 = lambda a: pl.BlockSpec(a.shape, lambda i: (0, 0))
    in_specs = [
        pl.BlockSpec(memory_space=pltpu.SMEM),
        pl.BlockSpec((None, tm, D_MODEL), lambda i: (i, 0, 0)),
        _w_in_spec(layer, COL_WIN, qw + 2 * kw),
        const2(gq), const2(gk), const2(bd),
    ]
    args = [sink, h, w_in, gq, gk, bd]
    y_spec = pl.BlockSpec((None, tm, qw), lambda i: (i, 0, 0))
    y_shape = jax.ShapeDtypeStruct((nt, tm, qw), BF16)
    scratch = [pltpu.VMEM((D_MODEL, qw + 2 * kw), BF16)]
    if cache is None:
        past = 0
        kv_spec = pl.BlockSpec((None, tm, kw), lambda i: (i, 0, 0))
        kv_shape = jax.ShapeDtypeStruct((nt, tm, kw), F32)
        out_specs = [y_spec, kv_spec, kv_spec]
        out_shape = [y_shape, kv_shape, kv_shape]
    else:
        cos, sin = rope
        ck, cv = cache
        past = ck.shape[2]
        cspec = pl.BlockSpec((None, None, past, kw), lambda i: (i, layer, 0, 0))
        in_specs += [const2(cos), const2(sin), cspec, cspec]
        args += [cos, sin, ck, cv]
        out_specs = [y_spec]
        out_shape = [y_shape]
        scratch += [pltpu.VMEM((seq + 2 * BLOCK, kw), BF16), pltpu.VMEM((seq + 2 * BLOCK, 2 * kw), BF16)]
    return pl.pallas_call(
        functools.partial(_win_kernel, group=group, seq=seq, past=past, layer=layer),
        grid=(nt,),
        in_specs=in_specs,
        out_specs=out_specs,
        out_shape=out_shape,
        scratch_shapes=scratch,
        compiler_params=_cparams(("arbitrary",)),
        name=name,
    )(*args)


def _ret_kernel(*refs, group, seq, has_state):
    if has_state:
        h_ref, wa_ref, wb_ref, df_ref, db_ref, s0f_ref, s0b_ref, y_ref, w_s, dec_s = refs
    else:
        h_ref, wa_ref, wb_ref, df_ref, db_ref, y_ref, sf_ref, sb_ref, w_s, dec_s = refs
    qk_w = RET_HEADS * RET_DK
    v_w = RET_HEADS * RET_DV
    _cast_weights_once((wa_ref, wb_ref), w_s)
    proj = jnp.dot(h_ref[...], w_s[...], preferred_element_type=F32)
    q = proj[:, :qk_w]
    k = proj[:, qk_w:2 * qk_w] * (RET_DK ** -0.5)
    v = proj[:, 2 * qk_w:2 * qk_w + v_w]
    rg = proj[:, 2 * qk_w + v_w:]
    lgf_all = _log_gamma(df_ref[...])
    lgb_all = _log_gamma(db_ref[...])
    tq = min(seq, ATT_Q_TILE)
    n_q = seq // tq

    @pl.when(pl.program_id(0) == 0)
    def _():
        width = 2 * seq - tq
        rel = (_row((tq, width)) - _lane((tq, width)) + (seq - tq)).astype(F32)
        for hd in range(RET_HEADS):
            lgf = lgf_all[hd:hd + 1, 0:1]
            lgb = lgb_all[hd:hd + 1, 0:1]
            dec_s[hd] = (jnp.where(rel >= 0, jnp.exp(jnp.maximum(rel, 0.0) * lgf), 0.0)
                         + jnp.where(rel <= 0, jnp.exp(jnp.maximum(-rel, 0.0) * lgb), 0.0))

    lane = _lane((seq, LANE))
    t_full = _row((seq, LANE)).astype(F32)
    for hd in range(RET_HEADS):
        slab = slice((hd // 2) * LANE, (hd // 2 + 1) * LANE)
        vsl = slice(hd * RET_DV, (hd + 1) * RET_DV)
        half = hd % 2
        lgf = lgf_all[hd:hd + 1, 0:1]
        lgb = lgb_all[hd:hd + 1, 0:1]
        for g in range(group):
            rows = slice(g * seq, (g + 1) * seq)
            qm = jnp.where(lane >= RET_DK if half == 1 else lane < RET_DK, q[rows, slab], 0.0)
            k128 = k[rows, slab]
            kb = k128.astype(BF16)
            vb = v[rows, vsl].astype(BF16)
            if has_state:
                q_f = qm * jnp.exp((t_full + 1.0) * lgf)
                q_b = qm * jnp.exp((seq - t_full) * lgb)
                o_state = (_bdot(q_f, s0f_ref[hd // 2]) + _bdot(q_b, s0b_ref[hd // 2]))
            else:
                k_f = k128 * jnp.exp((seq - 1.0 - t_full) * lgf)
                k_b = k128 * jnp.exp(t_full * lgb)
                s_f = _bdot(k_f.T, vb)
                s_b = _bdot(k_b.T, vb)
                sf_ref[g, hd] = s_f[half * RET_DK:(half + 1) * RET_DK]
                sb_ref[g, hd] = s_b[half * RET_DK:(half + 1) * RET_DK]
            for qi in range(n_q):
                qrows = slice(qi * tq, (qi + 1) * tq)
                s = _bdot_nt(qm[qrows], kb)
                off = (n_q - 1 - qi) * tq
                dec = dec_s[hd, :, off:off + seq]
                o = jnp.dot((s * dec).astype(BF16), vb, preferred_element_type=F32)
                if has_state:
                    o = o + o_state[qrows]
                r0 = g * seq + qi * tq
                y = _rms(o, None) * _silu(rg[r0:r0 + tq, vsl])
                y_ref[r0:r0 + tq, vsl] = y.astype(BF16)


def _ret_call(h, w_in, layer, dec_f, dec_b, group, seq, state=None, name="ret"):
    nt, tm = h.shape[0], h.shape[1]
    v_w = RET_HEADS * RET_DV
    ret_cols = COL_GATE - COL_RET
    const2 = lambda a: pl.BlockSpec(a.shape, lambda i: (0, 0))
    in_specs = [
        pl.BlockSpec((None, tm, D_MODEL), lambda i: (i, 0, 0)),
        _w_in_spec(layer, COL_RET, ret_cols // 2),
        _w_in_spec(layer, COL_RET + ret_cols // 2, ret_cols // 2),
        const2(dec_f), const2(dec_b),
    ]
    args = [h, w_in, w_in, dec_f, dec_b]
    y_spec = pl.BlockSpec((None, tm, v_w), lambda i: (i, 0, 0))
    y_shape = jax.ShapeDtypeStruct((nt, tm, v_w), BF16)
    if state is None:
        s_spec = pl.BlockSpec((group, RET_HEADS, RET_DK, RET_DV), lambda i: (i, 0, 0, 0))
        s_shape = jax.ShapeDtypeStruct((nt * group, RET_HEADS, RET_DK, RET_DV), F32)
        out_specs = [y_spec, s_spec, s_spec]
        out_shape = [y_shape, s_shape, s_shape]
    else:
        s0f, s0b = state
        sspec = pl.BlockSpec((None, None, 2, LANE, RET_DV), lambda i: (i, layer, 0, 0, 0))
        in_specs += [sspec, sspec]
        args += [s0f, s0b]
        out_specs = [y_spec]
        out_shape = [y_shape]
    return pl.pallas_call(
        functools.partial(_ret_kernel, group=group, seq=seq, has_state=state is not None),
        grid=(nt,),
        in_specs=in_specs,
        out_specs=out_specs,
        out_shape=out_shape,
        scratch_shapes=[pltpu.VMEM((D_MODEL, ret_cols), BF16),
                        pltpu.VMEM((RET_HEADS, min(seq, ATT_Q_TILE), 2 * seq - min(seq, ATT_Q_TILE)), F32)],
        compiler_params=_cparams(("arbitrary",)),
        name=name,
    )(*args)


def _merge_kernel(x_ref, h_ref, mod_ref, y0_ref, y1_ref, y2_ref, y3_ref, wg_ref, wb_ref, wo_ref,
                  o_ref, merged_s):
    h = h_ref[...]
    ys = [r[...] for r in (y0_ref, y1_ref, y2_ref, y3_ref)]
    for c in range(D_MODEL // MERGE_TILE):
        cols = slice(c * MERGE_TILE, (c + 1) * MERGE_TILE)
        acc = None
        for b in range(N_BRANCH):
            gate = jax.nn.sigmoid(jnp.dot(h, wg_ref[:, b * D_MODEL + c * MERGE_TILE:b * D_MODEL + (c + 1) * MERGE_TILE],
                                          preferred_element_type=F32))
            part = gate * jnp.dot(ys[b], wb_ref[b, :, cols], preferred_element_type=F32)
            acc = part if acc is None else acc + part
        merged_s[:, cols] = acc.astype(BF16)
    mixed = jnp.dot(merged_s[...], wo_ref[...], preferred_element_type=F32)
    o_ref[...] = x_ref[...] + mod_ref[5:6] * mixed


def _merge_call(x, h, mod, mod_row, ys, w_gate, w_branch, w_out, layer, name):
    nt, tm, _ = x.shape
    tok = pl.BlockSpec((None, tm, D_MODEL), lambda i: (i, 0, 0))
    ysp = pl.BlockSpec((None, tm, BRANCH_W), lambda i: (i, 0, 0))
    resident = pl.Buffered(1)
    return pl.pallas_call(
        _merge_kernel,
        grid=(nt,),
        in_specs=[
            tok, tok,
            pl.BlockSpec((None, N_MOD, D_MODEL), lambda i: (mod_row(i), 0, 0)),
            ysp, ysp, ysp, ysp,
            pl.BlockSpec((None, D_MODEL, N_BRANCH * D_MODEL), lambda i: (layer, 0, 0), pipeline_mode=resident),
            pl.BlockSpec((None, N_BRANCH, BRANCH_W, D_MODEL), lambda i: (layer, 0, 0, 0), pipeline_mode=resident),
            pl.BlockSpec((None, D_MODEL, D_MODEL), lambda i: (layer, 0, 0), pipeline_mode=resident),
        ],
        out_specs=tok,
        out_shape=jax.ShapeDtypeStruct(x.shape, F32),
        scratch_shapes=[pltpu.VMEM((tm, D_MODEL), BF16)],
        compiler_params=_cparams(("parallel",)),
        name=name,
    )(x, h, mod, *ys, w_gate, w_branch, w_out)


def _dft_tables(seq):
    n_fft = 2 * seq
    idx = np.arange(seq, dtype=np.int64)
    ang = 2.0 * np.pi * ((idx[:, None] * idx[None, :]) % n_fft).astype(np.float64) / n_fft
    out = []
    for m in (np.cos(ang), np.sin(ang)):
        hi = jnp.asarray(m, F32).astype(BF16)
        lo = (jnp.asarray(m, F32) - hi.astype(F32)).astype(BF16)
        out += [hi, lo]
    return out


def _hyena_tables(seq):
    t = np.arange(seq, dtype=np.float32) / np.float32(seq)
    f = np.arange(1, HY_BANDS + 1, dtype=np.float32)
    ang = np.float32(2.0 * math.pi) * t[:, None] * f[None, :]
    feats = np.zeros((seq, LANE), np.float32)
    feats[:, 0] = t
    feats[:, 1:1 + HY_BANDS] = np.sin(ang)
    feats[:, 1 + HY_BANDS:HY_EMB] = np.cos(ang)
    min_decay = math.log(HY_TARGET) / HY_DECAY_LONG_PCT
    max_decay = math.log(HY_TARGET) / HY_DECAY_SHORT_PCT
    deltas = np.linspace(min_decay, max_decay, HY_W, dtype=np.float32)
    window = np.exp(-t[:, None] * np.abs(deltas)[None, :]).astype(np.float32)
    return jnp.asarray(feats), jnp.asarray(window)


def _rope_tables(seq):
    pos = np.arange(seq)
    row = (pos // GRID_W).astype(np.float32)
    col = (pos % GRID_W).astype(np.float32)
    inv_freq = (np.float32(ROPE_BASE) ** (-np.arange(ROPE_PAIRS, dtype=np.float32) / np.float32(ROPE_PAIRS)))
    lane = np.arange(LANE)
    in_head = lane % HEAD_DIM
    use_col = in_head >= HEAD_DIM // 2
    pair = in_head % ROPE_PAIRS
    second = (in_head % (2 * ROPE_PAIRS)) >= ROPE_PAIRS
    p = np.where(use_col[None, :], col[:, None], row[:, None]).astype(np.float32)
    ang = (p * inv_freq[pair][None, :]).astype(np.float32)
    cos = np.cos(ang).astype(np.float32)
    sin = np.sin(ang).astype(np.float32)
    sin = np.where(second[None, :], sin, -sin)
    return jnp.asarray(cos), jnp.asarray(sin)


def _block_diag_ones(width, chunk):
    idx = np.arange(width) // chunk
    return jnp.asarray((idx[:, None] == idx[None, :]).astype(np.float32)).astype(BF16)


def kernel(x_prompt, x_sample, c, cache_diff_k, cache_diff_v, cache_win_k, cache_win_v, state_ret_f, state_ret_b, c_ctx, norm_ffa, norm_mix, norm_ffb, w_ada, b_ada, w_ffa_in, w_ffa_out, w_ffb_in, w_ffb_out, w_in, hy_conv_w, hy_conv_b, hy_f_w1, hy_f_b1, hy_f_w2, hy_f_b2, hy_f_w3, hy_skip, diff_q_norm, diff_k_norm, diff_lambda, diff_subln, win_q_norm, win_k_norm, win_sink, ret_decay_f, ret_decay_b, w_branch, w_out):
    batch, seq, _ = x_prompt.shape
    dec_batch, dec_seq, _ = x_sample.shape
    past = cache_diff_k.shape[2]
    ctx_group = TOKEN_TILE // seq
    assert TOKEN_TILE % seq == 0 and batch % ctx_group == 0 and dec_seq == TOKEN_TILE
    n_ctx = batch // ctx_group

    n_rows = 16
    cond = jnp.zeros((n_rows, D_MODEL), F32).at[0].set(c_ctx).at[1:1 + dec_batch].set(c)
    mod = _ada_call(cond, w_ada, b_ada).reshape(DEPTH, n_rows, N_MOD, D_MODEL)

    bf = lambda a: a.astype(BF16)
    w_ffa_in_b, w_ffa_out_b, w_ffb_in_b, w_ffb_out_b = bf(w_ffa_in), bf(w_ffa_out), bf(w_ffb_in), bf(w_ffb_out)
    w_gate = bf(w_in[:, :, COL_GATE:])
    w_branch_b, w_out_b = bf(w_branch), bf(w_out)

    bd64 = _block_diag_ones(2 * DIFF_HEADS * HEAD_DIM, HEAD_DIM)
    rope = _rope_tables(dec_seq)
    tables = {s: (_hyena_tables(s), _dft_tables(s)) for s in (seq, dec_seq)}

    ck_d = cache_diff_k.reshape(dec_batch, DEPTH, past, -1)
    cv_d = cache_diff_v.reshape(dec_batch, DEPTH, past, -1)
    ck_w = cache_win_k.reshape(dec_batch, DEPTH, past, -1)
    cv_w = cache_win_v.reshape(dec_batch, DEPTH, past, -1)
    s0_f = state_ret_f.reshape(dec_batch, DEPTH, 2, LANE, RET_DV)
    s0_b = state_ret_b.reshape(dec_batch, DEPTH, 2, LANE, RET_DV)

    pad_rows = lambda a, n: jnp.pad(a, ((0, n - a.shape[0]), (0, 0)))
    tile_lanes = lambda a, n: jnp.tile(a.reshape(1, -1), (1, n))

    dense = lambda a: a.reshape(-1, DENSE_TILE, a.shape[-1])
    y_ctx = x_prompt
    y_lat = x_sample
    ctx_row = lambda i: 0
    lat_row = lambda i: i // (dec_seq // DENSE_TILE) + 1
    st = {k: [] for k in ("dk", "dv", "wk", "wv", "rf", "rb")}

    for l in range(DEPTH):
        lam_init = 0.8 - 0.6 * math.exp(-0.3 * l)
        gain = lambda a: a[l].reshape(1, -1)
        gq_d, gk_d = tile_lanes(diff_q_norm[l], 2 * DIFF_HEADS), tile_lanes(diff_k_norm[l], 2 * DIFF_HEADS)
        gs_d = tile_lanes(diff_subln[l], DIFF_HEADS)
        gq_w, gk_w = tile_lanes(win_q_norm[l], WIN_Q_HEADS), tile_lanes(win_k_norm[l], WIN_KV_HEADS)
        dec_f = jnp.broadcast_to(ret_decay_f[l][:, None], (RET_HEADS, LANE))
        dec_b = jnp.broadcast_to(ret_decay_b[l][:, None], (RET_HEADS, LANE))
        w1 = jnp.pad(hy_f_w1[l], ((0, LANE - HY_EMB), (0, LANE - HY_FH)))
        b1 = jnp.pad(hy_f_b1[l].reshape(1, -1), ((0, 0), (0, LANE - HY_FH)))
        w2 = jnp.pad(hy_f_w2[l], ((0, LANE - HY_FH), (0, LANE - HY_FH)))
        b2 = jnp.pad(hy_f_b2[l].reshape(1, -1), ((0, 0), (0, LANE - HY_FH)))
        w3 = pad_rows(hy_f_w3[l], LANE)
        cw, cb, skip = hy_conv_w[l], hy_conv_b[l].reshape(1, -1), hy_skip[l].reshape(1, -1)

        for is_lat in (False, True):
            x = y_lat if is_lat else y_ctx
            s_len = dec_seq if is_lat else seq
            group = 1 if is_lat else ctx_group
            mod_row = lat_row if is_lat else ctx_row
            tag = f"{'lat' if is_lat else 'ctx'}{l}"
            (feats, window), (c_hi, c_lo, s_hi, s_lo) = tables[s_len]

            x, h = _ffn_call(dense(x), mod[l], mod_row, gain(norm_ffa), gain(norm_mix), w_ffa_in_b, w_ffa_out_b,
                             l, 0, f"ffn_a_{tag}")
            h = h.reshape(-1, TOKEN_TILE, D_MODEL)

            kr, ki, kn = _hy_filter_call(s_len, feats, w1, b1, w2, b2, w3, window, c_hi, c_lo, s_hi, s_lo)
            y_hy = _hyena_call(h, w_in, l, cw, cb, kr, ki, kn, skip, c_hi, s_hi, group, s_len, f"hyena_{tag}")
            if is_lat:
                (y_diff,) = _diff_call(h, w_in, l, gq_d, gk_d, diff_lambda[l], gs_d, bd64, lam_init,
                                       group, s_len, rope=rope, cache=(ck_d, cv_d), name=f"diff_{tag}")
                (y_win,) = _win_call(h, w_in, l, win_sink, gq_w, gk_w, bd64, group, s_len,
                                     rope=rope, cache=(ck_w, cv_w), name=f"win_{tag}")
                (y_ret,) = _ret_call(h, w_in, l, dec_f, dec_b, group, s_len, state=(s0_f, s0_b),
                                     name=f"ret_{tag}")
            else:
                y_diff, k_d, v_d = _diff_call(h, w_in, l, gq_d, gk_d, diff_lambda[l], gs_d, bd64,
                                              lam_init, group, s_len, name=f"diff_{tag}")
                y_win, k_w, v_w = _win_call(h, w_in, l, win_sink, gq_w, gk_w, bd64, group, s_len,
                                            name=f"win_{tag}")
                y_ret, s_f, s_b = _ret_call(h, w_in, l, dec_f, dec_b, group, s_len, name=f"ret_{tag}")
                st["dk"].append(k_d.reshape(batch, seq, DIFF_HEADS, 2, HEAD_DIM))
                st["dv"].append(v_d.reshape(batch, seq, DIFF_HEADS, 2 * HEAD_DIM))
                st["wk"].append(k_w.reshape(batch, seq, WIN_KV_HEADS, HEAD_DIM))
                st["wv"].append(v_w.reshape(batch, seq, WIN_KV_HEADS, HEAD_DIM))
                st["rf"].append(s_f)
                st["rb"].append(s_b)

            x = _merge_call(x, dense(h), mod[l], mod_row, tuple(dense(y) for y in (y_hy, y_diff, y_win, y_ret)),
                            w_gate, w_branch_b, w_out_b, l, f"merge_{tag}")
            x, _ = _ffn_call(x, mod[l], mod_row, gain(norm_ffb), None, w_ffb_in_b, w_ffb_out_b,
                             l, 6, f"ffn_b_{tag}")
            if is_lat:
                y_lat = x
            else:
                y_ctx = x

    stack = lambda xs: jnp.stack(xs, axis=1)
    return (y_ctx.reshape(batch, seq, D_MODEL), y_lat.reshape(dec_batch, dec_seq, D_MODEL),
            stack(st["dk"]), stack(st["dv"]), stack(st["wk"]),
            stack(st["wv"]), stack(st["rf"]), stack(st["rb"]))
```

```python
import functools
import math

import jax
import jax.numpy as jnp
import numpy as np
from jax import lax
from jax.experimental import pallas as pl
from jax.experimental.pallas import tpu as pltpu

F32 = jnp.float32
BF16 = jnp.bfloat16

D_MODEL = 1024
DEPTH = 2
GRID_W = 64
HEAD_DIM = 64
ROPE_PAIRS = HEAD_DIM // 4
ROPE_BASE = 10000.0
EPS = 1e-6
NEG_INF = -1e30
LOG2_E = math.log2(math.e)
D_FF = 2816
N_MOD = 9
N_BRANCH = 4
BRANCH_W = 512

HY_W = BRANCH_W
HY_BANDS = 16
HY_EMB = 1 + 2 * HY_BANDS
HY_FH = 64
HY_SIN_W = 1.0
HY_TARGET = 1e-2
HY_DECAY_SHORT_PCT = 0.3
HY_DECAY_LONG_PCT = 1.5

DIFF_HEADS = 4
WIN_Q_HEADS = 8
WIN_KV_HEADS = 2
WIN_GROUP = WIN_Q_HEADS // WIN_KV_HEADS
WINDOW = 128
BLOCK = 128
RET_HEADS = 4
RET_DK = 64
RET_DV = 128

COL_HY = 0
COL_DIFF = 3 * HY_W
COL_WIN = COL_DIFF + 3 * 512
COL_RET = COL_WIN + 512 + 128 + 128
COL_GATE = COL_RET + 256 + 256 + 512 + 512
IN_COLS = COL_GATE + N_BRANCH * D_MODEL

LANE = 128
TOKEN_TILE = 1024
DENSE_TILE = 512
FF_TILE = 256
MERGE_TILE = 256
ADA_TILE = 1152
ATT_Q_TILE = 256
VMEM_LIMIT = 56 * 1024 * 1024


def _cparams(sem):
    return pltpu.CompilerParams(dimension_semantics=sem, vmem_limit_bytes=VMEM_LIMIT)


def _bdot(a, b):
    return jnp.dot(a.astype(BF16), b.astype(BF16), preferred_element_type=F32)


def _bdot_nt(a, b):
    return lax.dot_general(a.astype(BF16), b.astype(BF16), (((1,), (1,)), ((), ())),
                           preferred_element_type=F32)


def _split(a):
    hi = a.astype(BF16)
    lo = (a - hi.astype(F32)).astype(BF16)
    return hi, lo


def _dot3(a, b):
    ah, al = _split(a)
    bh, bl = _split(b)
    d = functools.partial(jnp.dot, preferred_element_type=F32)
    return d(ah, bh) + d(ah, bl) + d(al, bh)


def _dot3_pre(ah, al, b):
    bh, bl = _split(b)
    d = functools.partial(jnp.dot, preferred_element_type=F32)
    return d(ah, bh) + d(ah, bl) + d(al, bh)


def _chunk_mean(sq, bd, n):
    hi, lo = _split(sq)
    d = functools.partial(jnp.dot, preferred_element_type=F32)
    return (d(hi, bd) + d(lo, bd)) * (1.0 / n)


def _rms(x, gain):
    y = x * lax.rsqrt(jnp.mean(x * x, axis=-1, keepdims=True) + EPS)
    return y if gain is None else y * gain


def _silu(x):
    return x * jax.nn.sigmoid(x)


def _lane(shape):
    return lax.broadcasted_iota(jnp.int32, shape, 1)


def _row(shape):
    return lax.broadcasted_iota(jnp.int32, shape, 0)


def _rope128(x, cos, sin):
    partner = jnp.where((_lane(x.shape) & 31) < 16,
                        pltpu.roll(x, LANE - ROPE_PAIRS, axis=1),
                        pltpu.roll(x, ROPE_PAIRS, axis=1))
    return x * cos + partner * sin


def _cast_weights_once(w_refs, w_s):
    @pl.when(pl.program_id(0) == 0)
    def _():
        off = 0
        for r in w_refs:
            w_s[:, off:off + r.shape[1]] = r[...].astype(BF16)
            off += r.shape[1]


def _w_in_spec(layer, col, width):
    assert col % width == 0
    return pl.BlockSpec((None, D_MODEL, width), lambda i: (layer, 0, col // width),
                        pipeline_mode=pl.Buffered(1))


def _log_gamma(decay):
    x = -decay
    return -(jnp.maximum(x, 0.0) + jnp.log1p(jnp.exp(-jnp.abs(x))))


def _ada_kernel(cond_ref, w_ref, b_ref, o_ref):
    s = _silu(cond_ref[...])
    o_ref[...] = _bdot(s, w_ref[...]) + b_ref[...]


def _ada_call(cond, w_ada, b_ada):
    rows = cond.shape[0]
    n_out = N_MOD * D_MODEL
    return pl.pallas_call(
        _ada_kernel,
        grid=(DEPTH, n_out // ADA_TILE),
        in_specs=[
            pl.BlockSpec((rows, D_MODEL), lambda l, j: (0, 0)),
            pl.BlockSpec((None, D_MODEL, ADA_TILE), lambda l, j: (l, 0, j)),
            pl.BlockSpec((None, 1, ADA_TILE), lambda l, j: (l, 0, j)),
        ],
        out_specs=pl.BlockSpec((None, rows, ADA_TILE), lambda l, j: (l, 0, j)),
        out_shape=jax.ShapeDtypeStruct((DEPTH, rows, n_out), F32),
        compiler_params=_cparams(("parallel", "parallel")),
        name="ada_mod",
    )(cond, w_ada, b_ada.reshape(DEPTH, 1, n_out))


def _ffn_kernel(*refs, mod_base, emit_next):
    if emit_next:
        x_ref, mod_ref, gain_ref, gain2_ref, wi_ref, wo_ref, y_ref, h2_ref, act_s = refs
    else:
        x_ref, mod_ref, gain_ref, wi_ref, wo_ref, y_ref, act_s = refs
    m = mod_ref[...]
    x = x_ref[...]
    h = (_rms(x, gain_ref[...]) * (1.0 + m[mod_base + 1:mod_base + 2]) + m[mod_base:mod_base + 1]).astype(BF16)
    for c in range(D_FF // FF_TILE):
        a = jnp.dot(h, wi_ref[:, c * FF_TILE:(c + 1) * FF_TILE], preferred_element_type=F32)
        g = jnp.dot(h, wi_ref[:, D_FF + c * FF_TILE:D_FF + (c + 1) * FF_TILE], preferred_element_type=F32)
        act_s[:, c * FF_TILE:(c + 1) * FF_TILE] = (_silu(a) * g).astype(BF16)
    out = jnp.dot(act_s[...], wo_ref[...], preferred_element_type=F32)
    y = x + 0.5 * m[mod_base + 2:mod_base + 3] * out
    y_ref[...] = y
    if emit_next:
        h2 = _rms(y, gain2_ref[...]) * (1.0 + m[mod_base + 4:mod_base + 5]) + m[mod_base + 3:mod_base + 4]
        h2_ref[...] = h2.astype(BF16)


def _ffn_call(x, mod, mod_row, gain, gain2, w_in, w_out, layer, mod_base, name):
    nt, tm, _ = x.shape
    emit_next = gain2 is not None
    tok = pl.BlockSpec((None, tm, D_MODEL), lambda i: (i, 0, 0))
    vec = pl.BlockSpec((1, D_MODEL), lambda i: (0, 0))
    in_specs = [tok, pl.BlockSpec((None, N_MOD, D_MODEL), lambda i: (mod_row(i), 0, 0)), vec]
    args = [x, mod, gain]
    if emit_next:
        in_specs.append(vec)
        args.append(gain2)
    in_specs += [
        pl.BlockSpec((None, D_MODEL, 2 * D_FF), lambda i: (layer, 0, 0), pipeline_mode=pl.Buffered(1)),
        pl.BlockSpec((None, D_FF, D_MODEL), lambda i: (layer, 0, 0), pipeline_mode=pl.Buffered(1)),
    ]
    args += [w_in, w_out]
    out_specs = [tok]
    out_shape = [jax.ShapeDtypeStruct(x.shape, F32)]
    if emit_next:
        out_specs.append(tok)
        out_shape.append(jax.ShapeDtypeStruct(x.shape, BF16))
    res = pl.pallas_call(
        functools.partial(_ffn_kernel, mod_base=mod_base, emit_next=emit_next),
        grid=(nt,),
        in_specs=in_specs,
        out_specs=out_specs,
        out_shape=out_shape,
        scratch_shapes=[pltpu.VMEM((tm, D_FF), BF16)],
        compiler_params=_cparams(("parallel",)),
        name=name,
    )(*args)
    return res if emit_next else (res[0], None)


def _hy_filter_kernel(feats_ref, w1_ref, b1_ref, w2_ref, b2_ref, w3_ref, win_ref,
                      ch_ref, cl_ref, sh_ref, sl_ref, kr_ref, ki_ref, kn_ref, *, seq):
    n_fft = 2 * seq
    z = jnp.sin(HY_SIN_W * (_dot3(feats_ref[...], w1_ref[...]) + b1_ref[...]))
    z = jnp.sin(HY_SIN_W * (_dot3(z, w2_ref[...]) + b2_ref[...]))
    zz = _dot3(z, w3_ref[...])
    win = win_ref[...]
    hf = zz[:, :HY_W] * win
    hb = zz[:, HY_W:] * win
    norm = (jnp.sum(jnp.abs(hf), axis=0, keepdims=True)
            + jnp.sum(jnp.abs(hb), axis=0, keepdims=True))
    hf = hf / norm
    hb = hb / norm
    row = _row(hf.shape)
    hb0 = jnp.where(row == 0, 0.0, hb)
    even = hf + hb0
    odd = hb0 - hf
    wk = jnp.where(row == 0, 1.0 / n_fft, 2.0 / n_fft)
    kr_ref[...] = _dot3_pre(ch_ref[...], cl_ref[...], even) * wk
    ki_ref[...] = _dot3_pre(sh_ref[...], sl_ref[...], odd) * wk
    sgn = jnp.where((row & 1) == 0, 1.0, -1.0)
    kn_ref[...] = jnp.sum(even * sgn, axis=0, keepdims=True) * (1.0 / n_fft)


def _hy_filter_call(seq, feats, w1, b1, w2, b2, w3, win, ch, cl, sh, sl):
    args = (feats, w1, b1, w2, b2, w3, win, ch, cl, sh, sl)
    return pl.pallas_call(
        functools.partial(_hy_filter_kernel, seq=seq),
        out_shape=[jax.ShapeDtypeStruct((seq, HY_W), F32),
                   jax.ShapeDtypeStruct((seq, HY_W), F32),
                   jax.ShapeDtypeStruct((1, HY_W), F32)],
        compiler_params=pltpu.CompilerParams(vmem_limit_bytes=VMEM_LIMIT),
        name=f"hyena_filter_{seq}",
    )(*args)


def _hyena_kernel(h_ref, w_ref, cw_ref, cb_ref, kr_ref, ki_ref, kn_ref, skip_ref,
                  c_ref, s_ref, y_ref, w_s, *, group, seq):
    _cast_weights_once((w_ref,), w_s)
    proj = jnp.dot(h_ref[...], w_s[...], preferred_element_type=F32)
    cw = cw_ref[...]
    cmat = c_ref[...]
    smat = s_ref[...]
    kr = kr_ref[...]
    ki = ki_ref[...]
    row = _row((seq, 3 * HY_W))
    row_w = _row((seq, HY_W))
    sgn = jnp.where((row_w & 1) == 0, 1.0, -1.0)
    for g in range(group):
        hy = proj[g * seq:(g + 1) * seq]
        prev = jnp.where(row == 0, 0.0, pltpu.roll(hy, 1, axis=0))
        nxt = jnp.where(row == seq - 1, 0.0, pltpu.roll(hy, seq - 1, axis=0))
        u = prev * cw[0:1] + hy * cw[1:2] + nxt * cw[2:3] + cb_ref[...]
        v, x0, x1 = u[:, :HY_W], u[:, HY_W:2 * HY_W], u[:, 2 * HY_W:]
        z = v * x1
        zb = z.astype(BF16)
        zr = jnp.dot(cmat, zb, preferred_element_type=F32)
        zs = jnp.dot(smat, zb, preferred_element_type=F32)
        yr = (zr * kr + zs * ki).astype(BF16)
        yi = (zr * ki - zs * kr).astype(BF16)
        nyq = jnp.sum(z * sgn, axis=0, keepdims=True) * kn_ref[...]
        conv = (jnp.dot(cmat, yr, preferred_element_type=F32)
                - jnp.dot(smat, yi, preferred_element_type=F32) + sgn * nyq)
        y_ref[g * seq:(g + 1) * seq, :] = (x0 * (conv + skip_ref[...] * z)).astype(BF16)


def _hyena_call(h, w_in, layer, cw, cb, kr, ki, kn, skip, cmat, smat, group, seq, name):
    nt, tm, _ = h.shape
    const2 = lambda a: pl.BlockSpec(a.shape, lambda i: (0, 0))
    return pl.pallas_call(
        functools.partial(_hyena_kernel, group=group, seq=seq),
        grid=(nt,),
        in_specs=[
            pl.BlockSpec((None, tm, D_MODEL), lambda i: (i, 0, 0)),
            _w_in_spec(layer, COL_HY, 3 * HY_W),
            const2(cw), const2(cb), const2(kr), const2(ki), const2(kn), const2(skip),
            const2(cmat), const2(smat),
        ],
        out_specs=pl.BlockSpec((None, tm, HY_W), lambda i: (i, 0, 0)),
        out_shape=jax.ShapeDtypeStruct((nt, tm, HY_W), BF16),
        scratch_shapes=[pltpu.VMEM((D_MODEL, 3 * HY_W), BF16)],
        compiler_params=_cparams(("arbitrary",)),
        name=name,
    )(h, w_in, cw, cb, kr, ki, kn, skip, cmat, smat)


def _diff_kernel(*refs, group, seq, past, lam_init):
    has_cache = past > 0
    if has_cache:
        (h_ref, w_ref, gq_ref, gk_ref, lam_ref, gs_ref, bd_ref, cos_ref, sin_ref,
         ck_ref, cv_ref, y_ref, w_s, k_s, v_s) = refs
    else:
        (h_ref, w_ref, gq_ref, gk_ref, lam_ref, gs_ref, bd_ref, y_ref, ko_ref, vo_ref, w_s) = refs
    width = 2 * DIFF_HEADS * HEAD_DIM
    _cast_weights_once((w_ref,), w_s)
    proj = jnp.dot(h_ref[...], w_s[...], preferred_element_type=F32)
    q, k, v = proj[:, :width], proj[:, width:2 * width], proj[:, 2 * width:]
    bd = bd_ref[...]
    q = q * lax.rsqrt(_chunk_mean(q * q, bd, HEAD_DIM) + EPS) * gq_ref[...]
    k = k * lax.rsqrt(_chunk_mean(k * k, bd, HEAD_DIM) + EPS) * gk_ref[...]
    dl = lam_ref[...]
    lam = (jnp.exp(jnp.sum(dl[0:1] * dl[1:2], axis=1, keepdims=True))
           - jnp.exp(jnp.sum(dl[2:3] * dl[3:4], axis=1, keepdims=True)) + lam_init)
    ones = jnp.ones((seq, LANE), BF16)
    if has_cache:
        cos, sin = cos_ref[...], sin_ref[...]
        q = jnp.concatenate([_rope128(q[:, s * LANE:(s + 1) * LANE], cos, sin)
                             for s in range(width // LANE)], axis=1)
        k = jnp.concatenate([_rope128(k[:, s * LANE:(s + 1) * LANE], cos, sin)
                             for s in range(width // LANE)], axis=1)
        k_s[0:seq, :] = k.astype(BF16)
        k_s[seq:seq + past, :] = ck_ref[...].astype(BF16)
        for hh in range(DIFF_HEADS):
            v_s[0:seq, 2 * hh * LANE:(2 * hh + 1) * LANE] = v[:, hh * LANE:(hh + 1) * LANE].astype(BF16)
            v_s[seq:seq + past, 2 * hh * LANE:(2 * hh + 1) * LANE] = (
                cv_ref[:, hh * LANE:(hh + 1) * LANE].astype(BF16))
            v_s[:, (2 * hh + 1) * LANE:(2 * hh + 2) * LANE] = jnp.ones((seq + past, LANE), BF16)
    else:
        ko_ref[...] = k
        vo_ref[...] = v
    q = q * (HEAD_DIM ** -0.5 * LOG2_E)
    tq = min(seq, ATT_Q_TILE)
    lane = _lane((tq, LANE))
    for g in range(group):
        for hh in range(DIFF_HEADS):
            sl = slice(hh * LANE, (hh + 1) * LANE)
            if has_cache:
                keys, vals = k_s[:, sl], v_s[:, 2 * hh * LANE:(2 * hh + 2) * LANE]
            else:
                keys = k[g * seq:(g + 1) * seq, sl].astype(BF16)
                vals = jnp.concatenate([v[g * seq:(g + 1) * seq, sl].astype(BF16), ones], axis=1)
            for qi in range(seq // tq):
                r0 = g * seq + qi * tq
                q128 = q[r0:r0 + tq, sl]
                qs = jnp.concatenate([jnp.where(lane < HEAD_DIM, q128, 0.0),
                                      jnp.where(lane >= HEAD_DIM, q128, 0.0)], axis=0)
                s = _bdot_nt(qs, keys)
                e = jnp.exp2(s - jnp.max(s, axis=-1, keepdims=True)).astype(BF16)
                oe = jnp.dot(e, vals, preferred_element_type=F32)
                r = 1.0 / oe[:, LANE:]
                o = oe[:tq, :LANE] * r[:tq] - oe[tq:, :LANE] * (lam * r[tq:])
                y = _rms(o, gs_ref[:, sl]) * (1.0 - lam_init)
                y_ref[r0:r0 + tq, sl] = y.astype(BF16)


def _diff_call(h, w_in, layer, gq, gk, lam, gs, bd, lam_init, group, seq,
               rope=None, cache=None, name="diff"):
    nt, tm = h.shape[0], h.shape[1]
    width = 2 * DIFF_HEADS * HEAD_DIM
    const2 = lambda a: pl.BlockSpec(a.shape, lambda i: (0, 0))
    in_specs = [
        pl.BlockSpec((None, tm, D_MODEL), lambda i: (i, 0, 0)),
        _w_in_spec(layer, COL_DIFF, 3 * width),
        const2(gq), const2(gk), const2(lam), const2(gs), const2(bd),
    ]
    args = [h, w_in, gq, gk, lam, gs, bd]
    y_spec = pl.BlockSpec((None, tm, width), lambda i: (i, 0, 0))
    y_shape = jax.ShapeDtypeStruct((nt, tm, width), BF16)
    scratch = [pltpu.VMEM((D_MODEL, 3 * width), BF16)]
    if cache is None:
        past = 0
        out_specs = [y_spec, y_spec, y_spec]
        out_shape = [y_shape, jax.ShapeDtypeStruct((nt, tm, width), F32),
                     jax.ShapeDtypeStruct((nt, tm, width), F32)]
    else:
        cos, sin = rope
        ck, cv = cache
        past = ck.shape[2]
        cspec = pl.BlockSpec((None, None, past, width), lambda i: (i, layer, 0, 0))
        in_specs += [const2(cos), const2(sin), cspec, cspec]
        args += [cos, sin, ck, cv]
        out_specs = [y_spec]
        out_shape = [y_shape]
        scratch += [pltpu.VMEM((seq + past, width), BF16), pltpu.VMEM((seq + past, 2 * width), BF16)]
    return pl.pallas_call(
        functools.partial(_diff_kernel, group=group, seq=seq, past=past, lam_init=lam_init),
        grid=(nt,),
        in_specs=in_specs,
        out_specs=out_specs,
        out_shape=out_shape,
        scratch_shapes=scratch,
        compiler_params=_cparams(("arbitrary",)),
        name=name,
    )(*args)


def _win_heads(q, hk, lane):
    out = []
    for gq in range(WIN_GROUP):
        j = hk * WIN_GROUP + gq
        slab = q[:, (j // 2) * LANE:(j // 2 + 1) * LANE]
        if j % 2 != hk:
            slab = pltpu.roll(slab, HEAD_DIM, axis=1)
        out.append(jnp.where(lane >= HEAD_DIM if hk == 1 else lane < HEAD_DIM, slab, 0.0))
    return out


def _win_place(o_heads, hk, lane):
    slabs = []
    for pair in range(WIN_GROUP // 2):
        halves = []
        for gq in (2 * pair, 2 * pair + 1):
            j = hk * WIN_GROUP + gq
            o = o_heads[gq]
            if j % 2 != hk:
                o = pltpu.roll(o, HEAD_DIM, axis=1)
            halves.append(o)
        slabs.append(jnp.where(lane < HEAD_DIM, halves[0], halves[1]))
    return slabs


def _sink_col(sink_ref, layer, hk, rows_per_head):
    rows = WIN_GROUP * rows_per_head
    r = _row((rows, 1))
    col = jnp.full((rows, 1), sink_ref[layer, hk * WIN_GROUP], F32)
    for gq in range(1, WIN_GROUP):
        col = jnp.where(r >= gq * rows_per_head, sink_ref[layer, hk * WIN_GROUP + gq], col)
    return col


def _win_kernel(*refs, group, seq, past, layer):
    has_cache = past > 0
    if has_cache:
        (sink_ref, h_ref, w_ref, gq_ref, gk_ref, bd_ref, cos_ref, sin_ref, ck_ref, cv_ref,
         y_ref, w_s, k_s, v_s) = refs
    else:
        (sink_ref, h_ref, w_ref, gq_ref, gk_ref, bd_ref, y_ref, ko_ref, vo_ref, w_s) = refs
    qw = WIN_Q_HEADS * HEAD_DIM
    kw = WIN_KV_HEADS * HEAD_DIM
    _cast_weights_once((w_ref,), w_s)
    proj = jnp.dot(h_ref[...], w_s[...], preferred_element_type=F32)
    q, k, v = proj[:, :qw], proj[:, qw:qw + kw], proj[:, qw + kw:]
    bd = bd_ref[...]
    q = q * lax.rsqrt(_chunk_mean(q * q, bd, HEAD_DIM) + EPS) * gq_ref[...]
    k = k * lax.rsqrt(_chunk_mean(k * k, bd[:kw, :kw], HEAD_DIM) + EPS) * gk_ref[...]
    scale = HEAD_DIM ** -0.5 * LOG2_E
    if not has_cache:
        ko_ref[...] = k
        vo_ref[...] = v
        q = q * scale
        lane = _lane((seq, LANE))
        ones = jnp.ones((seq, LANE), BF16)
        for g in range(group):
            rows = slice(g * seq, (g + 1) * seq)
            kb = k[rows].astype(BF16)
            vb = jnp.concatenate([v[rows].astype(BF16), ones], axis=1)
            qg = q[rows]
            for hk in range(WIN_KV_HEADS):
                qs = jnp.concatenate(_win_heads(qg, hk, lane), axis=0)
                s = _bdot_nt(qs, kb)
                sink = _sink_col(sink_ref, layer, hk, seq) * LOG2_E
                m = jnp.maximum(jnp.max(s, axis=-1, keepdims=True), sink)
                oe = jnp.dot(jnp.exp2(s - m).astype(BF16), vb, preferred_element_type=F32)
                o = oe[:, :LANE] * (1.0 / (oe[:, LANE:] + jnp.exp2(sink - m)))
                slabs = _win_place([o[gq * seq:(gq + 1) * seq] for gq in range(WIN_GROUP)], hk, lane)
                for pair, slab in enumerate(slabs):
                    c0 = (hk * 2 + pair) * LANE
                    y_ref[rows, c0:c0 + LANE] = slab.astype(BF16)
        return

    cos, sin = cos_ref[...], sin_ref[...]
    q = jnp.concatenate([_rope128(q[:, s * LANE:(s + 1) * LANE], cos, sin)
                         for s in range(qw // LANE)], axis=1) * scale
    k = _rope128(k, cos, sin)
    zpad = jnp.zeros((BLOCK, kw), BF16)
    k_s[0:BLOCK, :] = zpad
    k_s[BLOCK:BLOCK + seq, :] = k.astype(BF16)
    k_s[BLOCK + seq:2 * BLOCK + seq, :] = zpad
    v_s[0:BLOCK, 0:kw] = zpad
    v_s[BLOCK:BLOCK + seq, 0:kw] = v.astype(BF16)
    v_s[BLOCK + seq:2 * BLOCK + seq, 0:kw] = zpad
    v_s[:, kw:2 * kw] = jnp.ones((seq + 2 * BLOCK, kw), BF16)
    ckb = ck_ref[...].astype(BF16)
    cvb = jnp.concatenate([cv_ref[...].astype(BF16), jnp.ones((past, kw), BF16)], axis=1)
    lane_l = _lane((seq, LANE))
    lane_b = _lane((BLOCK, LANE))
    nb = seq // BLOCK
    rows_q = WIN_GROUP * BLOCK
    kk = _lane((rows_q, 3 * BLOCK))
    qi = _row((rows_q, 3 * BLOCK)) & (BLOCK - 1)
    rel = kk - BLOCK - qi
    in_window = (rel <= WINDOW) & (rel >= -WINDOW)
    for hk in range(WIN_KV_HEADS):
        heads = _win_heads(q, hk, lane_l)
        sink = _sink_col(sink_ref, layer, hk, BLOCK) * LOG2_E
        for n in range(nb):
            qs = jnp.concatenate([hd[n * BLOCK:(n + 1) * BLOCK] for hd in heads], axis=0)
            tk = kk + (n - 1) * BLOCK
            valid = in_window & (tk >= 0) & (tk < seq)
            sb = jnp.where(valid, _bdot_nt(qs, k_s[n * BLOCK:(n + 3) * BLOCK, :]), NEG_INF)
            sc = _bdot_nt(qs, ckb)
            m = jnp.maximum(jnp.maximum(jnp.max(sb, axis=-1, keepdims=True),
                                        jnp.max(sc, axis=-1, keepdims=True)), sink)
            eb = jnp.exp2(sb - m).astype(BF16)
            ec = jnp.exp2(sc - m).astype(BF16)
            oe = (jnp.dot(eb, v_s[n * BLOCK:(n + 3) * BLOCK, :], preferred_element_type=F32)
                  + jnp.dot(ec, cvb, preferred_element_type=F32))
            o = oe[:, :LANE] * (1.0 / (oe[:, LANE:] + jnp.exp2(sink - m)))
            slabs = _win_place([o[gq * BLOCK:(gq + 1) * BLOCK] for gq in range(WIN_GROUP)], hk, lane_b)
            for pair, slab in enumerate(slabs):
                c0 = (hk * 2 + pair) * LANE
                y_ref[n * BLOCK:(n + 1) * BLOCK, c0:c0 + LANE] = slab.astype(BF16)


def _win_call(h, w_in, layer, sink, gq, gk, bd, group, seq, rope=None, cache=None, name="win"):
    nt, tm = h.shape[0], h.shape[1]
    qw = WIN_Q_HEADS * HEAD_DIM
    kw = WIN_KV_HEADS * HEAD_DIM
    const2 = lambda a: pl.BlockSpec(a.shape, lambda i: (0, 0))
    in_specs = [
        pl.BlockSpec(memory_space=pltpu.SMEM),
        pl.BlockSpec((None, tm, D_MODEL), lambda i: (i, 0, 0)),
        _w_in_spec(layer, COL_WIN, qw + 2 * kw),
        const2(gq), const2(gk), const2(bd),
    ]
    args = [sink, h, w_in, gq, gk, bd]
    y_spec = pl.BlockSpec((None, tm, qw), lambda i: (i, 0, 0))
    y_shape = jax.ShapeDtypeStruct((nt, tm, qw), BF16)
    scratch = [pltpu.VMEM((D_MODEL, qw + 2 * kw), BF16)]
    if cache is None:
        past = 0
        kv_spec = pl.BlockSpec((None, tm, kw), lambda i: (i, 0, 0))
        kv_shape = jax.ShapeDtypeStruct((nt, tm, kw), F32)
        out_specs = [y_spec, kv_spec, kv_spec]
        out_shape = [y_shape, kv_shape, kv_shape]
    else:
        cos, sin = rope
        ck, cv = cache
        past = ck.shape[2]
        cspec = pl.BlockSpec((None, None, past, kw), lambda i: (i, layer, 0, 0))
        in_specs += [const2(cos), const2(sin), cspec, cspec]
        args += [cos, sin, ck, cv]
        out_specs = [y_spec]
        out_shape = [y_shape]
        scratch += [pltpu.VMEM((seq + 2 * BLOCK, kw), BF16), pltpu.VMEM((seq + 2 * BLOCK, 2 * kw), BF16)]
    return pl.pallas_call(
        functools.partial(_win_kernel, group=group, seq=seq, past=past, layer=layer),
        grid=(nt,),
        in_specs=in_specs,
        out_specs=out_specs,
        out_shape=out_shape,
        scratch_shapes=scratch,
        compiler_params=_cparams(("arbitrary",)),
        name=name,
    )(*args)


def _ret_kernel(*refs, group, seq, has_state):
    if has_state:
        h_ref, wa_ref, wb_ref, df_ref, db_ref, s0f_ref, s0b_ref, y_ref, w_s, dec_s = refs
    else:
        h_ref, wa_ref, wb_ref, df_ref, db_ref, y_ref, sf_ref, sb_ref, w_s, dec_s = refs
    qk_w = RET_HEADS * RET_DK
    v_w = RET_HEADS * RET_DV
    _cast_weights_once((wa_ref, wb_ref), w_s)
    proj = jnp.dot(h_ref[...], w_s[...], preferred_element_type=F32)
    q = proj[:, :qk_w]
    k = proj[:, qk_w:2 * qk_w] * (RET_DK ** -0.5)
    v = proj[:, 2 * qk_w:2 * qk_w + v_w]
    rg = proj[:, 2 * qk_w + v_w:]
    lgf_all = _log_gamma(df_ref[...])
    lgb_all = _log_gamma(db_ref[...])
    tq = min(seq, ATT_Q_TILE)
    n_q = seq // tq

    @pl.when(pl.program_id(0) == 0)
    def _():
        width = 2 * seq - tq
        rel = (_row((tq, width)) - _lane((tq, width)) + (seq - tq)).astype(F32)
        for hd in range(RET_HEADS):
            lgf = lgf_all[hd:hd + 1, 0:1]
            lgb = lgb_all[hd:hd + 1, 0:1]
            dec_s[hd] = (jnp.where(rel >= 0, jnp.exp(jnp.maximum(rel, 0.0) * lgf), 0.0)
                         + jnp.where(rel <= 0, jnp.exp(jnp.maximum(-rel, 0.0) * lgb), 0.0))

    lane = _lane((seq, LANE))
    t_full = _row((seq, LANE)).astype(F32)
    for hd in range(RET_HEADS):
        slab = slice((hd // 2) * LANE, (hd // 2 + 1) * LANE)
        vsl = slice(hd * RET_DV, (hd + 1) * RET_DV)
        half = hd % 2
        lgf = lgf_all[hd:hd + 1, 0:1]
        lgb = lgb_all[hd:hd + 1, 0:1]
        for g in range(group):
            rows = slice(g * seq, (g + 1) * seq)
            qm = jnp.where(lane >= RET_DK if half == 1 else lane < RET_DK, q[rows, slab], 0.0)
            k128 = k[rows, slab]
            kb = k128.astype(BF16)
            vb = v[rows, vsl].astype(BF16)
            if has_state:
                q_f = qm * jnp.exp((t_full + 1.0) * lgf)
                q_b = qm * jnp.exp((seq - t_full) * lgb)
                o_state = (_bdot(q_f, s0f_ref[hd // 2]) + _bdot(q_b, s0b_ref[hd // 2]))
            else:
                k_f = k128 * jnp.exp((seq - 1.0 - t_full) * lgf)
                k_b = k128 * jnp.exp(t_full * lgb)
                s_f = _bdot(k_f.T, vb)
                s_b = _bdot(k_b.T, vb)
                sf_ref[g, hd] = s_f[half * RET_DK:(half + 1) * RET_DK]
                sb_ref[g, hd] = s_b[half * RET_DK:(half + 1) * RET_DK]
            for qi in range(n_q):
                qrows = slice(qi * tq, (qi + 1) * tq)
                s = _bdot_nt(qm[qrows], kb)
                off = (n_q - 1 - qi) * tq
                dec = dec_s[hd, :, off:off + seq]
                o = jnp.dot((s * dec).astype(BF16), vb, preferred_element_type=F32)
                if has_state:
                    o = o + o_state[qrows]
                r0 = g * seq + qi * tq
                y = _rms(o, None) * _silu(rg[r0:r0 + tq, vsl])
                y_ref[r0:r0 + tq, vsl] = y.astype(BF16)


def _ret_call(h, w_in, layer, dec_f, dec_b, group, seq, state=None, name="ret"):
    nt, tm = h.shape[0], h.shape[1]
    v_w = RET_HEADS * RET_DV
    ret_cols = COL_GATE - COL_RET
    const2 = lambda a: pl.BlockSpec(a.shape, lambda i: (0, 0))
    in_specs = [
        pl.BlockSpec((None, tm, D_MODEL), lambda i: (i, 0, 0)),
        _w_in_spec(layer, COL_RET, ret_cols // 2),
        _w_in_spec(layer, COL_RET + ret_cols // 2, ret_cols // 2),
        const2(dec_f), const2(dec_b),
    ]
    args = [h, w_in, w_in, dec_f, dec_b]
    y_spec = pl.BlockSpec((None, tm, v_w), lambda i: (i, 0, 0))
    y_shape = jax.ShapeDtypeStruct((nt, tm, v_w), BF16)
    if state is None:
        s_spec = pl.BlockSpec((group, RET_HEADS, RET_DK, RET_DV), lambda i: (i, 0, 0, 0))
        s_shape = jax.ShapeDtypeStruct((nt * group, RET_HEADS, RET_DK, RET_DV), F32)
        out_specs = [y_spec, s_spec, s_spec]
        out_shape = [y_shape, s_shape, s_shape]
    else:
        s0f, s0b = state
        sspec = pl.BlockSpec((None, None, 2, LANE, RET_DV), lambda i: (i, layer, 0, 0, 0))
        in_specs += [sspec, sspec]
        args += [s0f, s0b]
        out_specs = [y_spec]
        out_shape = [y_shape]
    return pl.pallas_call(
        functools.partial(_ret_kernel, group=group, seq=seq, has_state=state is not None),
        grid=(nt,),
        in_specs=in_specs,
        out_specs=out_specs,
        out_shape=out_shape,
        scratch_shapes=[pltpu.VMEM((D_MODEL, ret_cols), BF16),
                        pltpu.VMEM((RET_HEADS, min(seq, ATT_Q_TILE), 2 * seq - min(seq, ATT_Q_TILE)), F32)],
        compiler_params=_cparams(("arbitrary",)),
        name=name,
    )(*args)


def _merge_kernel(x_ref, h_ref, mod_ref, y0_ref, y1_ref, y2_ref, y3_ref, wg_ref, wb_ref, wo_ref,
                  o_ref, merged_s):
    h = h_ref[...]
    ys = [r[...] for r in (y0_ref, y1_ref, y2_ref, y3_ref)]
    for c in range(D_MODEL // MERGE_TILE):
        cols = slice(c * MERGE_TILE, (c + 1) * MERGE_TILE)
        acc = None
        for b in range(N_BRANCH):
            gate = jax.nn.sigmoid(jnp.dot(h, wg_ref[:, b * D_MODEL + c * MERGE_TILE:b * D_MODEL + (c + 1) * MERGE_TILE],
                                          preferred_element_type=F32))
            part = gate * jnp.dot(ys[b], wb_ref[b, :, cols], preferred_element_type=F32)
            acc = part if acc is None else acc + part
        merged_s[:, cols] = acc.astype(BF16)
    mixed = jnp.dot(merged_s[...], wo_ref[...], preferred_element_type=F32)
    o_ref[...] = x_ref[...] + mod_ref[5:6] * mixed


def _merge_call(x, h, mod, mod_row, ys, w_gate, w_branch, w_out, layer, name):
    nt, tm, _ = x.shape
    tok = pl.BlockSpec((None, tm, D_MODEL), lambda i: (i, 0, 0))
    ysp = pl.BlockSpec((None, tm, BRANCH_W), lambda i: (i, 0, 0))
    resident = pl.Buffered(1)
    return pl.pallas_call(
        _merge_kernel,
        grid=(nt,),
        in_specs=[
            tok, tok,
            pl.BlockSpec((None, N_MOD, D_MODEL), lambda i: (mod_row(i), 0, 0)),
            ysp, ysp, ysp, ysp,
            pl.BlockSpec((None, D_MODEL, N_BRANCH * D_MODEL), lambda i: (layer, 0, 0), pipeline_mode=resident),
            pl.BlockSpec((None, N_BRANCH, BRANCH_W, D_MODEL), lambda i: (layer, 0, 0, 0), pipeline_mode=resident),
            pl.BlockSpec((None, D_MODEL, D_MODEL), lambda i: (layer, 0, 0), pipeline_mode=resident),
        ],
        out_specs=tok,
        out_shape=jax.ShapeDtypeStruct(x.shape, F32),
        scratch_shapes=[pltpu.VMEM((tm, D_MODEL), BF16)],
        compiler_params=_cparams(("parallel",)),
        name=name,
    )(x, h, mod, *ys, w_gate, w_branch, w_out)


def _dft_tables(seq):
    n_fft = 2 * seq
    idx = np.arange(seq, dtype=np.int64)
    ang = 2.0 * np.pi * ((idx[:, None] * idx[None, :]) % n_fft).astype(np.float64) / n_fft
    out = []
    for m in (np.cos(ang), np.sin(ang)):
        hi = jnp.asarray(m, F32).astype(BF16)
        lo = (jnp.asarray(m, F32) - hi.astype(F32)).astype(BF16)
        out += [hi, lo]
    return out


def _hyena_tables(seq):
    t = np.arange(seq, dtype=np.float32) / np.float32(seq)
    f = np.arange(1, HY_BANDS + 1, dtype=np.float32)
    ang = np.float32(2.0 * math.pi) * t[:, None] * f[None, :]
    feats = np.zeros((seq, LANE), np.float32)
    feats[:, 0] = t
    feats[:, 1:1 + HY_BANDS] = np.sin(ang)
    feats[:, 1 + HY_BANDS:HY_EMB] = np.cos(ang)
    min_decay = math.log(HY_TARGET) / HY_DECAY_LONG_PCT
    max_decay = math.log(HY_TARGET) / HY_DECAY_SHORT_PCT
    deltas = np.linspace(min_decay, max_decay, HY_W, dtype=np.float32)
    window = np.exp(-t[:, None] * np.abs(deltas)[None, :]).astype(np.float32)
    return jnp.asarray(feats), jnp.asarray(window)


def _rope_tables(seq):
    pos = np.arange(seq)
    row = (pos // GRID_W).astype(np.float32)
    col = (pos % GRID_W).astype(np.float32)
    inv_freq = (np.float32(ROPE_BASE) ** (-np.arange(ROPE_PAIRS, dtype=np.float32) / np.float32(ROPE_PAIRS)))
    lane = np.arange(LANE)
    in_head = lane % HEAD_DIM
    use_col = in_head >= HEAD_DIM // 2
    pair = in_head % ROPE_PAIRS
    second = (in_head % (2 * ROPE_PAIRS)) >= ROPE_PAIRS
    p = np.where(use_col[None, :], col[:, None], row[:, None]).astype(np.float32)
    ang = (p * inv_freq[pair][None, :]).astype(np.float32)
    cos = np.cos(ang).astype(np.float32)
    sin = np.sin(ang).astype(np.float32)
    sin = np.where(second[None, :], sin, -sin)
    return jnp.asarray(cos), jnp.asarray(sin)


def _block_diag_ones(width, chunk):
    idx = np.arange(width) // chunk
    return jnp.asarray((idx[:, None] == idx[None, :]).astype(np.float32)).astype(BF16)


def kernel(x_prompt, x_sample, c, cache_diff_k, cache_diff_v, cache_win_k, cache_win_v, state_ret_f, state_ret_b, c_ctx, norm_ffa, norm_mix, norm_ffb, w_ada, b_ada, w_ffa_in, w_ffa_out, w_ffb_in, w_ffb_out, w_in, hy_conv_w, hy_conv_b, hy_f_w1, hy_f_b1, hy_f_w2, hy_f_b2, hy_f_w3, hy_skip, diff_q_norm, diff_k_norm, diff_lambda, diff_subln, win_q_norm, win_k_norm, win_sink, ret_decay_f, ret_decay_b, w_branch, w_out):
    batch, seq, _ = x_prompt.shape
    dec_batch, dec_seq, _ = x_sample.shape
    past = cache_diff_k.shape[2]
    ctx_group = TOKEN_TILE // seq
    assert TOKEN_TILE % seq == 0 and batch % ctx_group == 0 and dec_seq == TOKEN_TILE
    n_ctx = batch // ctx_group

    n_rows = 16
    cond = jnp.zeros((n_rows, D_MODEL), F32).at[0].set(c_ctx).at[1:1 + dec_batch].set(c)
    mod = _ada_call(cond, w_ada, b_ada).reshape(DEPTH, n_rows, N_MOD, D_MODEL)

    bf = lambda a: a.astype(BF16)
    w_ffa_in_b, w_ffa_out_b, w_ffb_in_b, w_ffb_out_b = bf(w_ffa_in), bf(w_ffa_out), bf(w_ffb_in), bf(w_ffb_out)
    w_gate = bf(w_in[:, :, COL_GATE:])
    w_branch_b, w_out_b = bf(w_branch), bf(w_out)

    bd64 = _block_diag_ones(2 * DIFF_HEADS * HEAD_DIM, HEAD_DIM)
    rope = _rope_tables(dec_seq)
    tables = {s: (_hyena_tables(s), _dft_tables(s)) for s in (seq, dec_seq)}

    ck_d = cache_diff_k.reshape(dec_batch, DEPTH, past, -1)
    cv_d = cache_diff_v.reshape(dec_batch, DEPTH, past, -1)
    ck_w = cache_win_k.reshape(dec_batch, DEPTH, past, -1)
    cv_w = cache_win_v.reshape(dec_batch, DEPTH, past, -1)
    s0_f = state_ret_f.reshape(dec_batch, DEPTH, 2, LANE, RET_DV)
    s0_b = state_ret_b.reshape(dec_batch, DEPTH, 2, LANE, RET_DV)

    pad_rows = lambda a, n: jnp.pad(a, ((0, n - a.shape[0]), (0, 0)))
    tile_lanes = lambda a, n: jnp.tile(a.reshape(1, -1), (1, n))

    dense = lambda a: a.reshape(-1, DENSE_TILE, a.shape[-1])
    y_ctx = x_prompt
    y_lat = x_sample
    ctx_row = lambda i: 0
    lat_row = lambda i: i // (dec_seq // DENSE_TILE) + 1
    st = {k: [] for k in ("dk", "dv", "wk", "wv", "rf", "rb")}

    for l in range(DEPTH):
        lam_init = 0.8 - 0.6 * math.exp(-0.3 * l)
        gain = lambda a: a[l].reshape(1, -1)
        gq_d, gk_d = tile_lanes(diff_q_norm[l], 2 * DIFF_HEADS), tile_lanes(diff_k_norm[l], 2 * DIFF_HEADS)
        gs_d = tile_lanes(diff_subln[l], DIFF_HEADS)
        gq_w, gk_w = tile_lanes(win_q_norm[l], WIN_Q_HEADS), tile_lanes(win_k_norm[l], WIN_KV_HEADS)
        dec_f = jnp.broadcast_to(ret_decay_f[l][:, None], (RET_HEADS, LANE))
        dec_b = jnp.broadcast_to(ret_decay_b[l][:, None], (RET_HEADS, LANE))
        w1 = jnp.pad(hy_f_w1[l], ((0, LANE - HY_EMB), (0, LANE - HY_FH)))
        b1 = jnp.pad(hy_f_b1[l].reshape(1, -1), ((0, 0), (0, LANE - HY_FH)))
        w2 = jnp.pad(hy_f_w2[l], ((0, LANE - HY_FH), (0, LANE - HY_FH)))
        b2 = jnp.pad(hy_f_b2[l].reshape(1, -1), ((0, 0), (0, LANE - HY_FH)))
        w3 = pad_rows(hy_f_w3[l], LANE)
        cw, cb, skip = hy_conv_w[l], hy_conv_b[l].reshape(1, -1), hy_skip[l].reshape(1, -1)

        for is_lat in (False, True):
            x = y_lat if is_lat else y_ctx
            s_len = dec_seq if is_lat else seq
            group = 1 if is_lat else ctx_group
            mod_row = lat_row if is_lat else ctx_row
            tag = f"{'lat' if is_lat else 'ctx'}{l}"
            (feats, window), (c_hi, c_lo, s_hi, s_lo) = tables[s_len]

            x, h = _ffn_call(dense(x), mod[l], mod_row, gain(norm_ffa), gain(norm_mix), w_ffa_in_b, w_ffa_out_b,
                             l, 0, f"ffn_a_{tag}")
            h = h.reshape(-1, TOKEN_TILE, D_MODEL)

            kr, ki, kn = _hy_filter_call(s_len, feats, w1, b1, w2, b2, w3, window, c_hi, c_lo, s_hi, s_lo)
            y_hy = _hyena_call(h, w_in, l, cw, cb, kr, ki, kn, skip, c_hi, s_hi, group, s_len, f"hyena_{tag}")
            if is_lat:
                (y_diff,) = _diff_call(h, w_in, l, gq_d, gk_d, diff_lambda[l], gs_d, bd64, lam_init,
                                       group, s_len, rope=rope, cache=(ck_d, cv_d), name=f"diff_{tag}")
                (y_win,) = _win_call(h, w_in, l, win_sink, gq_w, gk_w, bd64, group, s_len,
                                     rope=rope, cache=(ck_w, cv_w), name=f"win_{tag}")
                (y_ret,) = _ret_call(h, w_in, l, dec_f, dec_b, group, s_len, state=(s0_f, s0_b),
                                     name=f"ret_{tag}")
            else:
                y_diff, k_d, v_d = _diff_call(h, w_in, l, gq_d, gk_d, diff_lambda[l], gs_d, bd64,
                                              lam_init, group, s_len, name=f"diff_{tag}")
                y_win, k_w, v_w = _win_call(h, w_in, l, win_sink, gq_w, gk_w, bd64, group, s_len,
                                            name=f"win_{tag}")
                y_ret, s_f, s_b = _ret_call(h, w_in, l, dec_f, dec_b, group, s_len, name=f"ret_{tag}")
                st["dk"].append(k_d.reshape(batch, seq, DIFF_HEADS, 2, HEAD_DIM))
                st["dv"].append(v_d.reshape(batch, seq, DIFF_HEADS, 2 * HEAD_DIM))
                st["wk"].append(k_w.reshape(batch, seq, WIN_KV_HEADS, HEAD_DIM))
                st["wv"].append(v_w.reshape(batch, seq, WIN_KV_HEADS, HEAD_DIM))
                st["rf"].append(s_f)
                st["rb"].append(s_b)

            x = _merge_call(x, dense(h), mod[l], mod_row, tuple(dense(y) for y in (y_hy, y_diff, y_win, y_ret)),
                            w_gate, w_branch_b, w_out_b, l, f"merge_{tag}")
            x, _ = _ffn_call(x, mod[l], mod_row, gain(norm_ffb), None, w_ffb_in_b, w_ffb_out_b,
                             l, 6, f"ffn_b_{tag}")
            if is_lat:
                y_lat = x
            else:
                y_ctx = x

    stack = lambda xs: jnp.stack(xs, axis=1)
    return (y_ctx.reshape(batch, seq, D_MODEL), y_lat.reshape(dec_batch, dec_seq, D_MODEL),
            stack(st["dk"]), stack(st["dv"]), stack(st["wk"]),
            stack(st["wv"]), stack(st["rf"]), stack(st["rb"]))
```

```python
import functools
import math

import jax
import jax.numpy as jnp
import numpy as np
from jax import lax
from jax.experimental import pallas as pl
from jax.experimental.pallas import tpu as pltpu

F32 = jnp.float32
BF16 = jnp.bfloat16

D_MODEL = 1024
DEPTH = 2
GRID_W = 64
HEAD_DIM = 64
ROPE_PAIRS = HEAD_DIM // 4
ROPE_BASE = 10000.0
EPS = 1e-6
NEG_INF = -1e30
LOG2_E = math.log2(math.e)
D_FF = 2816
N_MOD = 9
N_BRANCH = 4
BRANCH_W = 512

HY_W = BRANCH_W
HY_BANDS = 16
HY_EMB = 1 + 2 * HY_BANDS
HY_FH = 64
HY_SIN_W = 1.0
HY_TARGET = 1e-2
HY_DECAY_SHORT_PCT = 0.3
HY_DECAY_LONG_PCT = 1.5

DIFF_HEADS = 4
WIN_Q_HEADS = 8
WIN_KV_HEADS = 2
WIN_GROUP = WIN_Q_HEADS // WIN_KV_HEADS
WINDOW = 128
BLOCK = 128
RET_HEADS = 4
RET_DK = 64
RET_DV = 128

COL_HY = 0
COL_DIFF = 3 * HY_W
COL_WIN = COL_DIFF + 3 * 512
COL_RET = COL_WIN + 512 + 128 + 128
COL_GATE = COL_RET + 256 + 256 + 512 + 512
IN_COLS = COL_GATE + N_BRANCH * D_MODEL

LANE = 128
TOKEN_TILE = 1024
DENSE_TILE = 512
FF_TILE = 256
FFN_ROW_SPLIT = 2
MERGE_TILE = 256
ADA_TILE = 1152
ATT_Q_TILE = 256
VMEM_LIMIT = 56 * 1024 * 1024


def _cparams(sem):
    return pltpu.CompilerParams(dimension_semantics=sem, vmem_limit_bytes=VMEM_LIMIT)


def _bdot(a, b):
    return jnp.dot(a.astype(BF16), b.astype(BF16), preferred_element_type=F32)


def _bdot_nt(a, b):
    return lax.dot_general(a.astype(BF16), b.astype(BF16), (((1,), (1,)), ((), ())),
                           preferred_element_type=F32)


def _split(a):
    hi = a.astype(BF16)
    lo = (a - hi.astype(F32)).astype(BF16)
    return hi, lo


def _dot3(a, b):
    ah, al = _split(a)
    bh, bl = _split(b)
    d = functools.partial(jnp.dot, preferred_element_type=F32)
    return d(ah, bh) + d(ah, bl) + d(al, bh)


def _dot3_pre(ah, al, b):
    bh, bl = _split(b)
    d = functools.partial(jnp.dot, preferred_element_type=F32)
    return d(ah, bh) + d(ah, bl) + d(al, bh)


def _chunk_mean(sq, bd, n):
    hi, lo = _split(sq)
    d = functools.partial(jnp.dot, preferred_element_type=F32)
    return (d(hi, bd) + d(lo, bd)) * (1.0 / n)


def _rms_heads(x, gain):
    first = _lane((x.shape[0], LANE)) < HEAD_DIM
    out = []
    for s in range(x.shape[1] // LANE):
        xs = x[:, s * LANE:(s + 1) * LANE]
        sq = xs * xs
        lo = jnp.sum(jnp.where(first, sq, 0.0), axis=-1, keepdims=True)
        hi = jnp.sum(jnp.where(first, 0.0, sq), axis=-1, keepdims=True)
        r_lo = lax.rsqrt(lo * (1.0 / HEAD_DIM) + EPS)
        r_hi = lax.rsqrt(hi * (1.0 / HEAD_DIM) + EPS)
        out.append(xs * jnp.where(first, r_lo, r_hi))
    return jnp.concatenate(out, axis=1) * gain


def _rms(x, gain):
    y = x * lax.rsqrt(jnp.mean(x * x, axis=-1, keepdims=True) + EPS)
    return y if gain is None else y * gain


def _silu(x):
    return x * jax.nn.sigmoid(x)


def _lane(shape):
    return lax.broadcasted_iota(jnp.int32, shape, 1)


def _row(shape):
    return lax.broadcasted_iota(jnp.int32, shape, 0)


def _rope128(x, cos, sin):
    partner = jnp.where((_lane(x.shape) & 31) < 16,
                        pltpu.roll(x, LANE - ROPE_PAIRS, axis=1),
                        pltpu.roll(x, ROPE_PAIRS, axis=1))
    return x * cos + partner * sin


def _cast_weights_once(w_refs, w_s):
    @pl.when(pl.program_id(0) == 0)
    def _():
        off = 0
        for r in w_refs:
            w_s[:, off:off + r.shape[1]] = r[...].astype(BF16)
            off += r.shape[1]


def _w_in_spec(layer, col, width):
    assert col % width == 0
    return pl.BlockSpec((None, D_MODEL, width), lambda i: (layer, 0, col // width),
                        pipeline_mode=pl.Buffered(1))


def _log_gamma(decay):
    x = -decay
    return -(jnp.maximum(x, 0.0) + jnp.log1p(jnp.exp(-jnp.abs(x))))


def _ada_kernel(cond_ref, w_ref, b_ref, o_ref):
    s = _silu(cond_ref[...])
    o_ref[...] = _bdot(s, w_ref[...]) + b_ref[...]


def _ada_call(cond, w_ada, b_ada):
    rows = cond.shape[0]
    n_out = N_MOD * D_MODEL
    return pl.pallas_call(
        _ada_kernel,
        grid=(DEPTH, n_out // ADA_TILE),
        in_specs=[
            pl.BlockSpec((rows, D_MODEL), lambda l, j: (0, 0)),
            pl.BlockSpec((None, D_MODEL, ADA_TILE), lambda l, j: (l, 0, j)),
            pl.BlockSpec((None, 1, ADA_TILE), lambda l, j: (l, 0, j)),
        ],
        out_specs=pl.BlockSpec((None, rows, ADA_TILE), lambda l, j: (l, 0, j)),
        out_shape=jax.ShapeDtypeStruct((DEPTH, rows, n_out), F32),
        compiler_params=_cparams(("parallel", "parallel")),
        name="ada_mod",
    )(cond, w_ada, b_ada.reshape(DEPTH, 1, n_out))


def _ffn_kernel(*refs, mod_base, emit_next):
    if emit_next:
        x_ref, mod_ref, gain_ref, gain2_ref, wi_ref, wo_ref, y_ref, h2_ref, act_s = refs
    else:
        x_ref, mod_ref, gain_ref, wi_ref, wo_ref, y_ref, act_s = refs
    m = mod_ref[...]
    rows_per = x_ref.shape[0] // FFN_ROW_SPLIT
    for part in range(FFN_ROW_SPLIT):
        rows = slice(part * rows_per, (part + 1) * rows_per)
        x = x_ref[rows, :]
        h = (_rms(x, gain_ref[...]) * (1.0 + m[mod_base + 1:mod_base + 2]) + m[mod_base:mod_base + 1]).astype(BF16)
        for c in range(D_FF // FF_TILE):
            a = jnp.dot(h, wi_ref[:, c * FF_TILE:(c + 1) * FF_TILE], preferred_element_type=F32)
            g = jnp.dot(h, wi_ref[:, D_FF + c * FF_TILE:D_FF + (c + 1) * FF_TILE], preferred_element_type=F32)
            act_s[rows, c * FF_TILE:(c + 1) * FF_TILE] = (_silu(a) * g).astype(BF16)
        out = jnp.dot(act_s[rows, :], wo_ref[...], preferred_element_type=F32)
        y = x + 0.5 * m[mod_base + 2:mod_base + 3] * out
        y_ref[rows, :] = y
        if emit_next:
            h2 = _rms(y, gain2_ref[...]) * (1.0 + m[mod_base + 4:mod_base + 5]) + m[mod_base + 3:mod_base + 4]
            h2_ref[rows, :] = h2.astype(BF16)


def _ffn_call(x, mod, mod_row, gain, gain2, w_in, w_out, layer, mod_base, name):
    nt, tm, _ = x.shape
    emit_next = gain2 is not None
    tok = pl.BlockSpec((None, tm, D_MODEL), lambda i: (i, 0, 0))
    vec = pl.BlockSpec((1, D_MODEL), lambda i: (0, 0))
    in_specs = [tok, pl.BlockSpec((None, N_MOD, D_MODEL), lambda i: (mod_row(i), 0, 0)), vec]
    args = [x, mod, gain]
    if emit_next:
        in_specs.append(vec)
        args.append(gain2)
    in_specs += [
        pl.BlockSpec((None, D_MODEL, 2 * D_FF), lambda i: (layer, 0, 0), pipeline_mode=pl.Buffered(1)),
        pl.BlockSpec((None, D_FF, D_MODEL), lambda i: (layer, 0, 0), pipeline_mode=pl.Buffered(1)),
    ]
    args += [w_in, w_out]
    out_specs = [tok]
    out_shape = [jax.ShapeDtypeStruct(x.shape, F32)]
    if emit_next:
        out_specs.append(tok)
        out_shape.append(jax.ShapeDtypeStruct(x.shape, BF16))
    res = pl.pallas_call(
        functools.partial(_ffn_kernel, mod_base=mod_base, emit_next=emit_next),
        grid=(nt,),
        in_specs=in_specs,
        out_specs=out_specs,
        out_shape=out_shape,
        scratch_shapes=[pltpu.VMEM((tm, D_FF), BF16)],
        compiler_params=_cparams(("parallel",)),
        name=name,
    )(*args)
    return res if emit_next else (res[0], None)


def _hy_filter_kernel(feats_ref, w1_ref, b1_ref, w2_ref, b2_ref, w3_ref, win_ref,
                      ch_ref, cl_ref, sh_ref, sl_ref, kr_ref, ki_ref, kn_ref, *, seq):
    n_fft = 2 * seq
    z = jnp.sin(HY_SIN_W * (_dot3(feats_ref[...], w1_ref[...]) + b1_ref[...]))
    z = jnp.sin(HY_SIN_W * (_dot3(z, w2_ref[...]) + b2_ref[...]))
    zz = _dot3(z, w3_ref[...])
    win = win_ref[...]
    hf = zz[:, :HY_W] * win
    hb = zz[:, HY_W:] * win
    norm = (jnp.sum(jnp.abs(hf), axis=0, keepdims=True)
            + jnp.sum(jnp.abs(hb), axis=0, keepdims=True))
    hf = hf / norm
    hb = hb / norm
    row = _row(hf.shape)
    hb0 = jnp.where(row == 0, 0.0, hb)
    even = hf + hb0
    odd = hb0 - hf
    wk = jnp.where(row == 0, 1.0 / n_fft, 2.0 / n_fft)
    kr_ref[...] = _dot3_pre(ch_ref[...], cl_ref[...], even) * wk
    ki_ref[...] = _dot3_pre(sh_ref[...], sl_ref[...], odd) * wk
    sgn = jnp.where((row & 1) == 0, 1.0, -1.0)
    kn_ref[...] = jnp.sum(even * sgn, axis=0, keepdims=True) * (1.0 / n_fft)


def _hy_filter_call(seq, feats, w1, b1, w2, b2, w3, win, ch, cl, sh, sl):
    args = (feats, w1, b1, w2, b2, w3, win, ch, cl, sh, sl)
    return pl.pallas_call(
        functools.partial(_hy_filter_kernel, seq=seq),
        out_shape=[jax.ShapeDtypeStruct((seq, HY_W), F32),
                   jax.ShapeDtypeStruct((seq, HY_W), F32),
                   jax.ShapeDtypeStruct((1, HY_W), F32)],
        compiler_params=pltpu.CompilerParams(vmem_limit_bytes=VMEM_LIMIT),
        name=f"hyena_filter_{seq}",
    )(*args)


def _hyena_kernel(h_ref, w_ref, cw_ref, cb_ref, kr_ref, ki_ref, kn_ref, skip_ref,
                  c_ref, s_ref, y_ref, w_s, *, group, seq):
    _cast_weights_once((w_ref,), w_s)
    proj = jnp.dot(h_ref[...], w_s[...], preferred_element_type=F32)
    cw = cw_ref[...]
    cmat = c_ref[...]
    smat = s_ref[...]
    kr = kr_ref[...]
    ki = ki_ref[...]
    row = _row((seq, 3 * HY_W))
    row_w = _row((seq, HY_W))
    sgn = jnp.where((row_w & 1) == 0, 1.0, -1.0)
    for g in range(group):
        hy = proj[g * seq:(g + 1) * seq]
        prev = jnp.where(row == 0, 0.0, pltpu.roll(hy, 1, axis=0))
        nxt = jnp.where(row == seq - 1, 0.0, pltpu.roll(hy, seq - 1, axis=0))
        u = prev * cw[0:1] + hy * cw[1:2] + nxt * cw[2:3] + cb_ref[...]
        v, x0, x1 = u[:, :HY_W], u[:, HY_W:2 * HY_W], u[:, 2 * HY_W:]
        z = v * x1
        zb = z.astype(BF16)
        zr = jnp.dot(cmat, zb, preferred_element_type=F32)
        zs = jnp.dot(smat, zb, preferred_element_type=F32)
        yr = (zr * kr + zs * ki).astype(BF16)
        yi = (zr * ki - zs * kr).astype(BF16)
        nyq = jnp.sum(z * sgn, axis=0, keepdims=True) * kn_ref[...]
        conv = (jnp.dot(cmat, yr, preferred_element_type=F32)
                - jnp.dot(smat, yi, preferred_element_type=F32) + sgn * nyq)
        y_ref[g * seq:(g + 1) * seq, :] = (x0 * (conv + skip_ref[...] * z)).astype(BF16)


def _hyena_call(h, w_in, layer, cw, cb, kr, ki, kn, skip, cmat, smat, group, seq, name):
    nt, tm, _ = h.shape
    const2 = lambda a: pl.BlockSpec(a.shape, lambda i: (0, 0))
    return pl.pallas_call(
        functools.partial(_hyena_kernel, group=group, seq=seq),
        grid=(nt,),
        in_specs=[
            pl.BlockSpec((None, tm, D_MODEL), lambda i: (i, 0, 0)),
            _w_in_spec(layer, COL_HY, 3 * HY_W),
            const2(cw), const2(cb), const2(kr), const2(ki), const2(kn), const2(skip),
            const2(cmat), const2(smat),
        ],
        out_specs=pl.BlockSpec((None, tm, HY_W), lambda i: (i, 0, 0)),
        out_shape=jax.ShapeDtypeStruct((nt, tm, HY_W), BF16),
        scratch_shapes=[pltpu.VMEM((D_MODEL, 3 * HY_W), BF16)],
        compiler_params=_cparams(("arbitrary",)),
        name=name,
    )(h, w_in, cw, cb, kr, ki, kn, skip, cmat, smat)


def _diff_kernel(*refs, group, seq, past, lam_init):
    has_cache = past > 0
    if has_cache:
        (h_ref, w_ref, gq_ref, gk_ref, lam_ref, gs_ref, bd_ref, cos_ref, sin_ref,
         ck_ref, cv_ref, y_ref, w_s, k_s, v_s) = refs
    else:
        (h_ref, w_ref, gq_ref, gk_ref, lam_ref, gs_ref, bd_ref, y_ref, ko_ref, vo_ref, w_s) = refs
    width = 2 * DIFF_HEADS * HEAD_DIM
    _cast_weights_once((w_ref,), w_s)
    proj = jnp.dot(h_ref[...], w_s[...], preferred_element_type=F32)
    q, k, v = proj[:, :width], proj[:, width:2 * width], proj[:, 2 * width:]
    q = _rms_heads(q, gq_ref[...])
    k = _rms_heads(k, gk_ref[...])
    dl = lam_ref[...]
    lam = (jnp.exp(jnp.sum(dl[0:1] * dl[1:2], axis=1, keepdims=True))
           - jnp.exp(jnp.sum(dl[2:3] * dl[3:4], axis=1, keepdims=True)) + lam_init)
    ones = jnp.ones((seq, LANE), BF16)
    if has_cache:
        cos, sin = cos_ref[...], sin_ref[...]
        q = jnp.concatenate([_rope128(q[:, s * LANE:(s + 1) * LANE], cos, sin)
                             for s in range(width // LANE)], axis=1)
        k = jnp.concatenate([_rope128(k[:, s * LANE:(s + 1) * LANE], cos, sin)
                             for s in range(width // LANE)], axis=1)
        k_s[0:seq, :] = k.astype(BF16)
        k_s[seq:seq + past, :] = ck_ref[...].astype(BF16)
        for hh in range(DIFF_HEADS):
            v_s[0:seq, 2 * hh * LANE:(2 * hh + 1) * LANE] = v[:, hh * LANE:(hh + 1) * LANE].astype(BF16)
            v_s[seq:seq + past, 2 * hh * LANE:(2 * hh + 1) * LANE] = (
                cv_ref[:, hh * LANE:(hh + 1) * LANE].astype(BF16))
            v_s[:, (2 * hh + 1) * LANE:(2 * hh + 2) * LANE] = jnp.ones((seq + past, LANE), BF16)
    else:
        ko_ref[...] = k
        vo_ref[...] = v
    q = q * (HEAD_DIM ** -0.5 * LOG2_E)
    tq = min(seq, ATT_Q_TILE)
    lane = _lane((tq, LANE))
    for g in range(group):
        for hh in range(DIFF_HEADS):
            sl = slice(hh * LANE, (hh + 1) * LANE)
            if has_cache:
                keys, vals = k_s[:, sl], v_s[:, 2 * hh * LANE:(2 * hh + 2) * LANE]
            else:
                keys = k[g * seq:(g + 1) * seq, sl].astype(BF16)
                vals = jnp.concatenate([v[g * seq:(g + 1) * seq, sl].astype(BF16), ones], axis=1)
            for qi in range(seq // tq):
                r0 = g * seq + qi * tq
                q128 = q[r0:r0 + tq, sl]
                qs = jnp.concatenate([jnp.where(lane < HEAD_DIM, q128, 0.0),
                                      jnp.where(lane >= HEAD_DIM, q128, 0.0)], axis=0)
                s = _bdot_nt(qs, keys)
                e = jnp.exp2(s - jnp.max(s, axis=-1, keepdims=True)).astype(BF16)
                oe = jnp.dot(e, vals, preferred_element_type=F32)
                r = 1.0 / oe[:, LANE:]
                o = oe[:tq, :LANE] * r[:tq] - oe[tq:, :LANE] * (lam * r[tq:])
                y = _rms(o, gs_ref[:, sl]) * (1.0 - lam_init)
                y_ref[r0:r0 + tq, sl] = y.astype(BF16)


def _diff_call(h, w_in, layer, gq, gk, lam, gs, bd, lam_init, group, seq,
               rope=None, cache=None, name="diff"):
    nt, tm = h.shape[0], h.shape[1]
    width = 2 * DIFF_HEADS * HEAD_DIM
    const2 = lambda a: pl.BlockSpec(a.shape, lambda i: (0, 0))
    in_specs = [
        pl.BlockSpec((None, tm, D_MODEL), lambda i: (i, 0, 0)),
        _w_in_spec(layer, COL_DIFF, 3 * width),
        const2(gq), const2(gk), const2(lam), const2(gs), const2(bd),
    ]
    args = [h, w_in, gq, gk, lam, gs, bd]
    y_spec = pl.BlockSpec((None, tm, width), lambda i: (i, 0, 0))
    y_shape = jax.ShapeDtypeStruct((nt, tm, width), BF16)
    scratch = [pltpu.VMEM((D_MODEL, 3 * width), BF16)]
    if cache is None:
        past = 0
        out_specs = [y_spec, y_spec, y_spec]
        out_shape = [y_shape, jax.ShapeDtypeStruct((nt, tm, width), F32),
                     jax.ShapeDtypeStruct((nt, tm, width), F32)]
    else:
        cos, sin = rope
        ck, cv = cache
        past = ck.shape[2]
        cspec = pl.BlockSpec((None, None, past, width), lambda i: (i, layer, 0, 0))
        in_specs += [const2(cos), const2(sin), cspec, cspec]
        args += [cos, sin, ck, cv]
        out_specs = [y_spec]
        out_shape = [y_shape]
        scratch += [pltpu.VMEM((seq + past, width), BF16), pltpu.VMEM((seq + past, 2 * width), BF16)]
    return pl.pallas_call(
        functools.partial(_diff_kernel, group=group, seq=seq, past=past, lam_init=lam_init),
        grid=(nt,),
        in_specs=in_specs,
        out_specs=out_specs,
        out_shape=out_shape,
        scratch_shapes=scratch,
        compiler_params=_cparams(("arbitrary",)),
        name=name,
    )(*args)


def _win_heads(q, hk, lane):
    out = []
    for gq in range(WIN_GROUP):
        j = hk * WIN_GROUP + gq
        slab = q[:, (j // 2) * LANE:(j // 2 + 1) * LANE]
        if j % 2 != hk:
            slab = pltpu.roll(slab, HEAD_DIM, axis=1)
        out.append(jnp.where(lane >= HEAD_DIM if hk == 1 else lane < HEAD_DIM, slab, 0.0))
    return out


def _win_place(o_heads, hk, lane):
    slabs = []
    for pair in range(WIN_GROUP // 2):
        halves = []
        for gq in (2 * pair, 2 * pair + 1):
            j = hk * WIN_GROUP + gq
            o = o_heads[gq]
            if j % 2 != hk:
                o = pltpu.roll(o, HEAD_DIM, axis=1)
            halves.append(o)
        slabs.append(jnp.where(lane < HEAD_DIM, halves[0], halves[1]))
    return slabs


def _sink_col(sink_ref, layer, hk, rows_per_head):
    rows = WIN_GROUP * rows_per_head
    r = _row((rows, 1))
    col = jnp.full((rows, 1), sink_ref[layer, hk * WIN_GROUP], F32)
    for gq in range(1, WIN_GROUP):
        col = jnp.where(r >= gq * rows_per_head, sink_ref[layer, hk * WIN_GROUP + gq], col)
    return col


def _win_kernel(*refs, group, seq, past, layer):
    has_cache = past > 0
    if has_cache:
        (sink_ref, h_ref, w_ref, gq_ref, gk_ref, bd_ref, cos_ref, sin_ref, ck_ref, cv_ref,
         y_ref, w_s, k_s, v_s) = refs
    else:
        (sink_ref, h_ref, w_ref, gq_ref, gk_ref, bd_ref, y_ref, ko_ref, vo_ref, w_s) = refs
    qw = WIN_Q_HEADS * HEAD_DIM
    kw = WIN_KV_HEADS * HEAD_DIM
    _cast_weights_once((w_ref,), w_s)
    proj = jnp.dot(h_ref[...], w_s[...], preferred_element_type=F32)
    q, k, v = proj[:, :qw], proj[:, qw:qw + kw], proj[:, qw + kw:]
    q = _rms_heads(q, gq_ref[...])
    k = _rms_heads(k, gk_ref[...])
    scale = HEAD_DIM ** -0.5 * LOG2_E
    if not has_cache:
        ko_ref[...] = k
        vo_ref[...] = v
        q = q * scale
        lane = _lane((seq, LANE))
        ones = jnp.ones((seq, LANE), BF16)
        for g in range(group):
            rows = slice(g * seq, (g + 1) * seq)
            kb = k[rows].astype(BF16)
            vb = jnp.concatenate([v[rows].astype(BF16), ones], axis=1)
            qg = q[rows]
            for hk in range(WIN_KV_HEADS):
                qs = jnp.concatenate(_win_heads(qg, hk, lane), axis=0)
                s = _bdot_nt(qs, kb)
                sink = _sink_col(sink_ref, layer, hk, seq) * LOG2_E
                m = jnp.maximum(jnp.max(s, axis=-1, keepdims=True), sink)
                oe = jnp.dot(jnp.exp2(s - m).astype(BF16), vb, preferred_element_type=F32)
                o = oe[:, :LANE] * (1.0 / (oe[:, LANE:] + jnp.exp2(sink - m)))
                slabs = _win_place([o[gq * seq:(gq + 1) * seq] for gq in range(WIN_GROUP)], hk, lane)
                for pair, slab in enumerate(slabs):
                    c0 = (hk * 2 + pair) * LANE
                    y_ref[rows, c0:c0 + LANE] = slab.astype(BF16)
        return

    cos, sin = cos_ref[...], sin_ref[...]
    q = jnp.concatenate([_rope128(q[:, s * LANE:(s + 1) * LANE], cos, sin)
                         for s in range(qw // LANE)], axis=1) * scale
    k = _rope128(k, cos, sin)
    zpad = jnp.zeros((BLOCK, kw), BF16)
    k_s[0:BLOCK, :] = zpad
    k_s[BLOCK:BLOCK + seq, :] = k.astype(BF16)
    k_s[BLOCK + seq:2 * BLOCK + seq, :] = zpad
    v_s[0:BLOCK, 0:kw] = zpad
    v_s[BLOCK:BLOCK + seq, 0:kw] = v.astype(BF16)
    v_s[BLOCK + seq:2 * BLOCK + seq, 0:kw] = zpad
    v_s[:, kw:2 * kw] = jnp.ones((seq + 2 * BLOCK, kw), BF16)
    ckb = ck_ref[...].astype(BF16)
    cvb = jnp.concatenate([cv_ref[...].astype(BF16), jnp.ones((past, kw), BF16)], axis=1)
    lane_l = _lane((seq, LANE))
    lane_b = _lane((BLOCK, LANE))
    nb = seq // BLOCK
    rows_q = WIN_GROUP * BLOCK
    kk = _lane((rows_q, 3 * BLOCK))
    qi = _row((rows_q, 3 * BLOCK)) & (BLOCK - 1)
    rel = kk - BLOCK - qi
    bias_mid = jnp.where(rel <= WINDOW, jnp.where(rel >= -WINDOW, 0.0, NEG_INF), NEG_INF)
    bias = {(False, False): bias_mid,
            (True, False): jnp.where(kk >= BLOCK, bias_mid, NEG_INF),
            (False, True): jnp.where(kk < 2 * BLOCK, bias_mid, NEG_INF)}
    bias[(True, True)] = jnp.where(kk < 2 * BLOCK, bias[(True, False)], NEG_INF)
    for hk in range(WIN_KV_HEADS):
        heads = _win_heads(q, hk, lane_l)
        sink = _sink_col(sink_ref, layer, hk, BLOCK) * LOG2_E
        for n in range(nb):
            qs = jnp.concatenate([hd[n * BLOCK:(n + 1) * BLOCK] for hd in heads], axis=0)
            sb = _bdot_nt(qs, k_s[n * BLOCK:(n + 3) * BLOCK, :]) + bias[(n == 0, n == nb - 1)]
            sc = _bdot_nt(qs, ckb)
            m = jnp.maximum(jnp.maximum(jnp.max(sb, axis=-1, keepdims=True),
                                        jnp.max(sc, axis=-1, keepdims=True)), sink)
            eb = jnp.exp2(sb - m).astype(BF16)
            ec = jnp.exp2(sc - m).astype(BF16)
            oe = (jnp.dot(eb, v_s[n * BLOCK:(n + 3) * BLOCK, :], preferred_element_type=F32)
                  + jnp.dot(ec, cvb, preferred_element_type=F32))
            o = oe[:, :LANE] * (1.0 / (oe[:, LANE:] + jnp.exp2(sink - m)))
            slabs = _win_place([o[gq * BLOCK:(gq + 1) * BLOCK] for gq in range(WIN_GROUP)], hk, lane_b)
            for pair, slab in enumerate(slabs):
                c0 = (hk * 2 + pair) * LANE
                y_ref[n * BLOCK:(n + 1) * BLOCK, c0:c0 + LANE] = slab.astype(BF16)


def _win_call(h, w_in, layer, sink, gq, gk, bd, group, seq, rope=None, cache=None, name="win"):
    nt, tm = h.shape[0], h.shape[1]
    qw = WIN_Q_HEADS * HEAD_DIM
    kw = WIN_KV_HEADS * HEAD_DIM
    const2 = lambda a: pl.BlockSpec(a.shape, lambda i: (0, 0))
    in_specs = [
        pl.BlockSpec(memory_space=pltpu.SMEM),
        pl.BlockSpec((None, tm, D_MODEL), lambda i: (i, 0, 0)),
        _w_in_spec(layer, COL_WIN, qw + 2 * kw),
        const2(gq), const2(gk), const2(bd),
    ]
    args = [sink, h, w_in, gq, gk, bd]
    y_spec = pl.BlockSpec((None, tm, qw), lambda i: (i, 0, 0))
    y_shape = jax.ShapeDtypeStruct((nt, tm, qw), BF16)
    scratch = [pltpu.VMEM((D_MODEL, qw + 2 * kw), BF16)]
    if cache is None:
        past = 0
        kv_spec = pl.BlockSpec((None, tm, kw), lambda i: (i, 0, 0))
        kv_shape = jax.ShapeDtypeStruct((nt, tm, kw), F32)
        out_specs = [y_spec, kv_spec, kv_spec]
        out_shape = [y_shape, kv_shape, kv_shape]
    else:
        cos, sin = rope
        ck, cv = cache
        past = ck.shape[2]
        cspec = pl.BlockSpec((None, None, past, kw), lambda i: (i, layer, 0, 0))
        in_specs += [const2(cos), const2(sin), cspec, cspec]
        args += [cos, sin, ck, cv]
        out_specs = [y_spec]
        out_shape = [y_shape]
        scratch += [pltpu.VMEM((seq + 2 * BLOCK, kw), BF16), pltpu.VMEM((seq + 2 * BLOCK, 2 * kw), BF16)]
    return pl.pallas_call(
        functools.partial(_win_kernel, group=group, seq=seq, past=past, layer=layer),
        grid=(nt,),
        in_specs=in_specs,
        out_specs=out_specs,
        out_shape=out_shape,
        scratch_shapes=scratch,
        compiler_params=_cparams(("arbitrary",)),
        name=name,
    )(*args)


def _ret_kernel(*refs, group, seq, has_state):
    if has_state:
        h_ref, wa_ref, wb_ref, df_ref, db_ref, s0f_ref, s0b_ref, y_ref, w_s, dec_s = refs
    else:
        h_ref, wa_ref, wb_ref, df_ref, db_ref, y_ref, sf_ref, sb_ref, w_s, dec_s = refs
    qk_w = RET_HEADS * RET_DK
    v_w = RET_HEADS * RET_DV
    _cast_weights_once((wa_ref, wb_ref), w_s)
    proj = jnp.dot(h_ref[...], w_s[...], preferred_element_type=F32)
    q = proj[:, :qk_w]
    k = proj[:, qk_w:2 * qk_w] * (RET_DK ** -0.5)
    v = proj[:, 2 * qk_w:2 * qk_w + v_w]
    rg = proj[:, 2 * qk_w + v_w:]
    lgf_all = _log_gamma(df_ref[...])
    lgb_all = _log_gamma(db_ref[...])
    tq = min(seq, ATT_Q_TILE)
    n_q = seq // tq

    @pl.when(pl.program_id(0) == 0)
    def _():
        width = 2 * seq - tq
        rel = (_row((tq, width)) - _lane((tq, width)) + (seq - tq)).astype(F32)
        for hd in range(RET_HEADS):
            lgf = lgf_all[hd:hd + 1, 0:1]
            lgb = lgb_all[hd:hd + 1, 0:1]
            dec_s[hd] = (jnp.where(rel >= 0, jnp.exp(jnp.maximum(rel, 0.0) * lgf), 0.0)
                         + jnp.where(rel <= 0, jnp.exp(jnp.maximum(-rel, 0.0) * lgb), 0.0))

    lane = _lane((seq, LANE))
    t_full = _row((seq, LANE)).astype(F32)
    for hd in range(RET_HEADS):
        slab = slice((hd // 2) * LANE, (hd // 2 + 1) * LANE)
        vsl = slice(hd * RET_DV, (hd + 1) * RET_DV)
        half = hd % 2
        lgf = lgf_all[hd:hd + 1, 0:1]
        lgb = lgb_all[hd:hd + 1, 0:1]
        for g in range(group):
            rows = slice(g * seq, (g + 1) * seq)
            qm = jnp.where(lane >= RET_DK if half == 1 else lane < RET_DK, q[rows, slab], 0.0)
            k128 = k[rows, slab]
            kb = k128.astype(BF16)
            vb = v[rows, vsl].astype(BF16)
            if has_state:
                q_f = qm * jnp.exp((t_full + 1.0) * lgf)
                q_b = qm * jnp.exp((seq - t_full) * lgb)
                o_state = (_bdot(q_f, s0f_ref[hd // 2]) + _bdot(q_b, s0b_ref[hd // 2]))
            else:
                k_f = k128 * jnp.exp((seq - 1.0 - t_full) * lgf)
                k_b = k128 * jnp.exp(t_full * lgb)
                s_f = _bdot(k_f.T, vb)
                s_b = _bdot(k_b.T, vb)
                sf_ref[g, hd] = s_f[half * RET_DK:(half + 1) * RET_DK]
                sb_ref[g, hd] = s_b[half * RET_DK:(half + 1) * RET_DK]
            for qi in range(n_q):
                qrows = slice(qi * tq, (qi + 1) * tq)
                s = _bdot_nt(qm[qrows], kb)
                off = (n_q - 1 - qi) * tq
                dec = dec_s[hd, :, off:off + seq]
                o = jnp.dot((s * dec).astype(BF16), vb, preferred_element_type=F32)
                if has_state:
                    o = o + o_state[qrows]
                r0 = g * seq + qi * tq
                y = _rms(o, None) * _silu(rg[r0:r0 + tq, vsl])
                y_ref[r0:r0 + tq, vsl] = y.astype(BF16)


def _ret_call(h, w_in, layer, dec_f, dec_b, group, seq, state=None, name="ret"):
    nt, tm = h.shape[0], h.shape[1]
    v_w = RET_HEADS * RET_DV
    ret_cols = COL_GATE - COL_RET
    const2 = lambda a: pl.BlockSpec(a.shape, lambda i: (0, 0))
    in_specs = [
        pl.BlockSpec((None, tm, D_MODEL), lambda i: (i, 0, 0)),
        _w_in_spec(layer, COL_RET, ret_cols // 2),
        _w_in_spec(layer, COL_RET + ret_cols // 2, ret_cols // 2),
        const2(dec_f), const2(dec_b),
    ]
    args = [h, w_in, w_in, dec_f, dec_b]
    y_spec = pl.BlockSpec((None, tm, v_w), lambda i: (i, 0, 0))
    y_shape = jax.ShapeDtypeStruct((nt, tm, v_w), BF16)
    if state is None:
        s_spec = pl.BlockSpec((group, RET_HEADS, RET_DK, RET_DV), lambda i: (i, 0, 0, 0))
        s_shape = jax.ShapeDtypeStruct((nt * group, RET_HEADS, RET_DK, RET_DV), F32)
        out_specs = [y_spec, s_spec, s_spec]
        out_shape = [y_shape, s_shape, s_shape]
    else:
        s0f, s0b = state
        sspec = pl.BlockSpec((None, None, 2, LANE, RET_DV), lambda i: (i, layer, 0, 0, 0))
        in_specs += [sspec, sspec]
        args += [s0f, s0b]
        out_specs = [y_spec]
        out_shape = [y_shape]
    return pl.pallas_call(
        functools.partial(_ret_kernel, group=group, seq=seq, has_state=state is not None),
        grid=(nt,),
        in_specs=in_specs,
        out_specs=out_specs,
        out_shape=out_shape,
        scratch_shapes=[pltpu.VMEM((D_MODEL, ret_cols), BF16),
                        pltpu.VMEM((RET_HEADS, min(seq, ATT_Q_TILE), 2 * seq - min(seq, ATT_Q_TILE)), F32)],
        compiler_params=_cparams(("arbitrary",)),
        name=name,
    )(*args)


def _merge_kernel(x_ref, h_ref, mod_ref, y0_ref, y1_ref, y2_ref, y3_ref, wg_ref, wb_ref, wo_ref,
                  o_ref, merged_s):
    h = h_ref[...]
    ys = [r[...] for r in (y0_ref, y1_ref, y2_ref, y3_ref)]
    for c in range(D_MODEL // MERGE_TILE):
        cols = slice(c * MERGE_TILE, (c + 1) * MERGE_TILE)
        acc = None
        for b in range(N_BRANCH):
            gate = jax.nn.sigmoid(jnp.dot(h, wg_ref[:, b * D_MODEL + c * MERGE_TILE:b * D_MODEL + (c + 1) * MERGE_TILE],
                                          preferred_element_type=F32))
            part = gate * jnp.dot(ys[b], wb_ref[b, :, cols], preferred_element_type=F32)
            acc = part if acc is None else acc + part
        merged_s[:, cols] = acc.astype(BF16)
    mixed = jnp.dot(merged_s[...], wo_ref[...], preferred_element_type=F32)
    o_ref[...] = x_ref[...] + mod_ref[5:6] * mixed


def _merge_call(x, h, mod, mod_row, ys, w_gate, w_branch, w_out, layer, name):
    nt, tm, _ = x.shape
    tok = pl.BlockSpec((None, tm, D_MODEL), lambda i: (i, 0, 0))
    ysp = pl.BlockSpec((None, tm, BRANCH_W), lambda i: (i, 0, 0))
    resident = pl.Buffered(1)
    return pl.pallas_call(
        _merge_kernel,
        grid=(nt,),
        in_specs=[
            tok, tok,
            pl.BlockSpec((None, N_MOD, D_MODEL), lambda i: (mod_row(i), 0, 0)),
            ysp, ysp, ysp, ysp,
            pl.BlockSpec((None, D_MODEL, N_BRANCH * D_MODEL), lambda i: (layer, 0, 0), pipeline_mode=resident),
            pl.BlockSpec((None, N_BRANCH, BRANCH_W, D_MODEL), lambda i: (layer, 0, 0, 0), pipeline_mode=resident),
            pl.BlockSpec((None, D_MODEL, D_MODEL), lambda i: (layer, 0, 0), pipeline_mode=resident),
        ],
        out_specs=tok,
        out_shape=jax.ShapeDtypeStruct(x.shape, F32),
        scratch_shapes=[pltpu.VMEM((tm, D_MODEL), BF16)],
        compiler_params=_cparams(("parallel",)),
        name=name,
    )(x, h, mod, *ys, w_gate, w_branch, w_out)


def _dft_tables(seq):
    n_fft = 2 * seq
    idx = np.arange(seq, dtype=np.int64)
    ang = 2.0 * np.pi * ((idx[:, None] * idx[None, :]) % n_fft).astype(np.float64) / n_fft
    out = []
    for m in (np.cos(ang), np.sin(ang)):
        hi = jnp.asarray(m, F32).astype(BF16)
        lo = (jnp.asarray(m, F32) - hi.astype(F32)).astype(BF16)
        out += [hi, lo]
    return out


def _hyena_tables(seq):
    t = np.arange(seq, dtype=np.float32) / np.float32(seq)
    f = np.arange(1, HY_BANDS + 1, dtype=np.float32)
    ang = np.float32(2.0 * math.pi) * t[:, None] * f[None, :]
    feats = np.zeros((seq, LANE), np.float32)
    feats[:, 0] = t
    feats[:, 1:1 + HY_BANDS] = np.sin(ang)
    feats[:, 1 + HY_BANDS:HY_EMB] = np.cos(ang)
    min_decay = math.log(HY_TARGET) / HY_DECAY_LONG_PCT
    max_decay = math.log(HY_TARGET) / HY_DECAY_SHORT_PCT
    deltas = np.linspace(min_decay, max_decay, HY_W, dtype=np.float32)
    window = np.exp(-t[:, None] * np.abs(deltas)[None, :]).astype(np.float32)
    return jnp.asarray(feats), jnp.asarray(window)


def _rope_tables(seq):
    pos = np.arange(seq)
    row = (pos // GRID_W).astype(np.float32)
    col = (pos % GRID_W).astype(np.float32)
    inv_freq = (np.float32(ROPE_BASE) ** (-np.arange(ROPE_PAIRS, dtype=np.float32) / np.float32(ROPE_PAIRS)))
    lane = np.arange(LANE)
    in_head = lane % HEAD_DIM
    use_col = in_head >= HEAD_DIM // 2
    pair = in_head % ROPE_PAIRS
    second = (in_head % (2 * ROPE_PAIRS)) >= ROPE_PAIRS
    p = np.where(use_col[None, :], col[:, None], row[:, None]).astype(np.float32)
    ang = (p * inv_freq[pair][None, :]).astype(np.float32)
    cos = np.cos(ang).astype(np.float32)
    sin = np.sin(ang).astype(np.float32)
    sin = np.where(second[None, :], sin, -sin)
    return jnp.asarray(cos), jnp.asarray(sin)


def _block_diag_ones(width, chunk):
    idx = np.arange(width) // chunk
    return jnp.asarray((idx[:, None] == idx[None, :]).astype(np.float32)).astype(BF16)


def kernel(x_prompt, x_sample, c, cache_diff_k, cache_diff_v, cache_win_k, cache_win_v, state_ret_f, state_ret_b, c_ctx, norm_ffa, norm_mix, norm_ffb, w_ada, b_ada, w_ffa_in, w_ffa_out, w_ffb_in, w_ffb_out, w_in, hy_conv_w, hy_conv_b, hy_f_w1, hy_f_b1, hy_f_w2, hy_f_b2, hy_f_w3, hy_skip, diff_q_norm, diff_k_norm, diff_lambda, diff_subln, win_q_norm, win_k_norm, win_sink, ret_decay_f, ret_decay_b, w_branch, w_out):
    batch, seq, _ = x_prompt.shape
    dec_batch, dec_seq, _ = x_sample.shape
    past = cache_diff_k.shape[2]
    ctx_group = TOKEN_TILE // seq
    assert TOKEN_TILE % seq == 0 and batch % ctx_group == 0 and dec_seq == TOKEN_TILE
    n_ctx = batch // ctx_group

    n_rows = 16
    cond = jnp.zeros((n_rows, D_MODEL), F32).at[0].set(c_ctx).at[1:1 + dec_batch].set(c)
    mod = _ada_call(cond, w_ada, b_ada).reshape(DEPTH, n_rows, N_MOD, D_MODEL)

    bf = lambda a: a.astype(BF16)
    w_ffa_in_b, w_ffa_out_b, w_ffb_in_b, w_ffb_out_b = bf(w_ffa_in), bf(w_ffa_out), bf(w_ffb_in), bf(w_ffb_out)
    w_gate = bf(w_in[:, :, COL_GATE:])
    w_branch_b, w_out_b = bf(w_branch), bf(w_out)

    bd64 = _block_diag_ones(2 * DIFF_HEADS * HEAD_DIM, HEAD_DIM)
    rope = _rope_tables(dec_seq)
    tables = {s: (_hyena_tables(s), _dft_tables(s)) for s in (seq, dec_seq)}

    ck_d = cache_diff_k.reshape(dec_batch, DEPTH, past, -1)
    cv_d = cache_diff_v.reshape(dec_batch, DEPTH, past, -1)
    ck_w = cache_win_k.reshape(dec_batch, DEPTH, past, -1)
    cv_w = cache_win_v.reshape(dec_batch, DEPTH, past, -1)
    s0_f = state_ret_f.reshape(dec_batch, DEPTH, 2, LANE, RET_DV)
    s0_b = state_ret_b.reshape(dec_batch, DEPTH, 2, LANE, RET_DV)

    pad_rows = lambda a, n: jnp.pad(a, ((0, n - a.shape[0]), (0, 0)))
    tile_lanes = lambda a, n: jnp.tile(a.reshape(1, -1), (1, n))

    dense = lambda a: a.reshape(-1, DENSE_TILE, a.shape[-1])
    y_ctx = x_prompt
    y_lat = x_sample
    ctx_row = lambda i: 0
    lat_row = lambda i: i // (dec_seq // DENSE_TILE) + 1
    st = {k: [] for k in ("dk", "dv", "wk", "wv", "rf", "rb")}

    for l in range(DEPTH):
        lam_init = 0.8 - 0.6 * math.exp(-0.3 * l)
        gain = lambda a: a[l].reshape(1, -1)
        gq_d, gk_d = tile_lanes(diff_q_norm[l], 2 * DIFF_HEADS), tile_lanes(diff_k_norm[l], 2 * DIFF_HEADS)
        gs_d = tile_lanes(diff_subln[l], DIFF_HEADS)
        gq_w, gk_w = tile_lanes(win_q_norm[l], WIN_Q_HEADS), tile_lanes(win_k_norm[l], WIN_KV_HEADS)
        dec_f = jnp.broadcast_to(ret_decay_f[l][:, None], (RET_HEADS, LANE))
        dec_b = jnp.broadcast_to(ret_decay_b[l][:, None], (RET_HEADS, LANE))
        w1 = jnp.pad(hy_f_w1[l], ((0, LANE - HY_EMB), (0, LANE - HY_FH)))
        b1 = jnp.pad(hy_f_b1[l].reshape(1, -1), ((0, 0), (0, LANE - HY_FH)))
        w2 = jnp.pad(hy_f_w2[l], ((0, LANE - HY_FH), (0, LANE - HY_FH)))
        b2 = jnp.pad(hy_f_b2[l].reshape(1, -1), ((0, 0), (0, LANE - HY_FH)))
        w3 = pad_rows(hy_f_w3[l], LANE)
        cw, cb, skip = hy_conv_w[l], hy_conv_b[l].reshape(1, -1), hy_skip[l].reshape(1, -1)

        for is_lat in (False, True):
            x = y_lat if is_lat else y_ctx
            s_len = dec_seq if is_lat else seq
            group = 1 if is_lat else ctx_group
            mod_row = lat_row if is_lat else ctx_row
            tag = f"{'lat' if is_lat else 'ctx'}{l}"
            (feats, window), (c_hi, c_lo, s_hi, s_lo) = tables[s_len]

            x, h = _ffn_call(dense(x), mod[l], mod_row, gain(norm_ffa), gain(norm_mix), w_ffa_in_b, w_ffa_out_b,
                             l, 0, f"ffn_a_{tag}")
            h = h.reshape(-1, TOKEN_TILE, D_MODEL)

            kr, ki, kn = _hy_filter_call(s_len, feats, w1, b1, w2, b2, w3, window, c_hi, c_lo, s_hi, s_lo)
            y_hy = _hyena_call(h, w_in, l, cw, cb, kr, ki, kn, skip, c_hi, s_hi, group, s_len, f"hyena_{tag}")
            if is_lat:
                (y_diff,) = _diff_call(h, w_in, l, gq_d, gk_d, diff_lambda[l], gs_d, bd64, lam_init,
                                       group, s_len, rope=rope, cache=(ck_d, cv_d), name=f"diff_{tag}")
                (y_win,) = _win_call(h, w_in, l, win_sink, gq_w, gk_w, bd64, group, s_len,
                                     rope=rope, cache=(ck_w, cv_w), name=f"win_{tag}")
                (y_ret,) = _ret_call(h, w_in, l, dec_f, dec_b, group, s_len, state=(s0_f, s0_b),
                                     name=f"ret_{tag}")
            else:
                y_diff, k_d, v_d = _diff_call(h, w_in, l, gq_d, gk_d, diff_lambda[l], gs_d, bd64,
                                              lam_init, group, s_len, name=f"diff_{tag}")
                y_win, k_w, v_w = _win_call(h, w_in, l, win_sink, gq_w, gk_w, bd64, group, s_len,
                                            name=f"win_{tag}")
                y_ret, s_f, s_b = _ret_call(h, w_in, l, dec_f, dec_b, group, s_len, name=f"ret_{tag}")
                st["dk"].append(k_d.reshape(batch, seq, DIFF_HEADS, 2, HEAD_DIM))
                st["dv"].append(v_d.reshape(batch, seq, DIFF_HEADS, 2 * HEAD_DIM))
                st["wk"].append(k_w.reshape(batch, seq, WIN_KV_HEADS, HEAD_DIM))
                st["wv"].append(v_w.reshape(batch, seq, WIN_KV_HEADS, HEAD_DIM))
                st["rf"].append(s_f)
                st["rb"].append(s_b)

            x = _merge_call(x, dense(h), mod[l], mod_row, tuple(dense(y) for y in (y_hy, y_diff, y_win, y_ret)),
                            w_gate, w_branch_b, w_out_b, l, f"merge_{tag}")
            x, _ = _ffn_call(x, mod[l], mod_row, gain(norm_ffb), None, w_ffb_in_b, w_ffb_out_b,
                             l, 6, f"ffn_b_{tag}")
            if is_lat:
                y_lat = x
            else:
                y_ctx = x

    stack = lambda xs: jnp.stack(xs, axis=1)
    return (y_ctx.reshape(batch, seq, D_MODEL), y_lat.reshape(dec_batch, dec_seq, D_MODEL),
            stack(st["dk"]), stack(st["dv"]), stack(st["wk"]),
            stack(st["wv"]), stack(st["rf"]), stack(st["rb"]))
```

```python
import functools
import math

import jax
import jax.numpy as jnp
import numpy as np
from jax import lax
from jax.experimental import pallas as pl
from jax.experimental.pallas import tpu as pltpu

F32 = jnp.float32
BF16 = jnp.bfloat16

D_MODEL = 1024
DEPTH = 2
GRID_W = 64
HEAD_DIM = 64
ROPE_PAIRS = HEAD_DIM // 4
ROPE_BASE = 10000.0
EPS = 1e-6
NEG_INF = -1e30
LOG2_E = math.log2(math.e)
D_FF = 2816
N_MOD = 9
N_BRANCH = 4
BRANCH_W = 512

HY_W = BRANCH_W
HY_BANDS = 16
HY_EMB = 1 + 2 * HY_BANDS
HY_FH = 64
HY_SIN_W = 1.0
HY_TARGET = 1e-2
HY_DECAY_SHORT_PCT = 0.3
HY_DECAY_LONG_PCT = 1.5

DIFF_HEADS = 4
WIN_Q_HEADS = 8
WIN_KV_HEADS = 2
WIN_GROUP = WIN_Q_HEADS // WIN_KV_HEADS
WINDOW = 128
BLOCK = 128
RET_HEADS = 4
RET_DK = 64
RET_DV = 128

COL_HY = 0
COL_DIFF = 3 * HY_W
COL_WIN = COL_DIFF + 3 * 512
COL_RET = COL_WIN + 512 + 128 + 128
COL_GATE = COL_RET + 256 + 256 + 512 + 512
IN_COLS = COL_GATE + N_BRANCH * D_MODEL

LANE = 128
TOKEN_TILE = 1024
FFN_A_TILE = 1024
MERGE_FFN_TILE = 512
FF_TILE = 256
ROW_SPLIT = 2
MERGE_TILE = 256
ADA_TILE = 1152
ATT_Q_TILE = 256
VMEM_LIMIT = 56 * 1024 * 1024


def _cparams(sem):
    return pltpu.CompilerParams(dimension_semantics=sem, vmem_limit_bytes=VMEM_LIMIT)


def _bdot(a, b):
    return jnp.dot(a.astype(BF16), b.astype(BF16), preferred_element_type=F32)


def _bdot_nt(a, b):
    return lax.dot_general(a.astype(BF16), b.astype(BF16), (((1,), (1,)), ((), ())),
                           preferred_element_type=F32)


def _split(a):
    hi = a.astype(BF16)
    lo = (a - hi.astype(F32)).astype(BF16)
    return hi, lo


def _dot3(a, b):
    ah, al = _split(a)
    bh, bl = _split(b)
    d = functools.partial(jnp.dot, preferred_element_type=F32)
    return d(ah, bh) + d(ah, bl) + d(al, bh)


def _dot3_pre(ah, al, b):
    bh, bl = _split(b)
    d = functools.partial(jnp.dot, preferred_element_type=F32)
    return d(ah, bh) + d(ah, bl) + d(al, bh)


def _rms_heads(x, gain):
    first = _lane((x.shape[0], LANE)) < HEAD_DIM
    out = []
    for s in range(x.shape[1] // LANE):
        xs = x[:, s * LANE:(s + 1) * LANE]
        sq = xs * xs
        lo = jnp.sum(jnp.where(first, sq, 0.0), axis=-1, keepdims=True)
        hi = jnp.sum(jnp.where(first, 0.0, sq), axis=-1, keepdims=True)
        r_lo = lax.rsqrt(lo * (1.0 / HEAD_DIM) + EPS)
        r_hi = lax.rsqrt(hi * (1.0 / HEAD_DIM) + EPS)
        out.append(xs * jnp.where(first, r_lo, r_hi))
    return jnp.concatenate(out, axis=1) * gain


def _rms(x, gain):
    y = x * lax.rsqrt(jnp.mean(x * x, axis=-1, keepdims=True) + EPS)
    return y if gain is None else y * gain


def _silu(x):
    return x * jax.nn.sigmoid(x)


def _lane(shape):
    return lax.broadcasted_iota(jnp.int32, shape, 1)


def _row(shape):
    return lax.broadcasted_iota(jnp.int32, shape, 0)


def _rope128(x, cos, sin):
    partner = jnp.where((_lane(x.shape) & 31) < 16,
                        pltpu.roll(x, LANE - ROPE_PAIRS, axis=1),
                        pltpu.roll(x, ROPE_PAIRS, axis=1))
    return x * cos + partner * sin


def _cast_weights_once(w_refs, w_s):
    @pl.when(pl.program_id(0) == 0)
    def _():
        off = 0
        for r in w_refs:
            w_s[:, off:off + r.shape[1]] = r[...].astype(BF16)
            off += r.shape[1]


def _w_in_spec(layer, col, width):
    assert col % width == 0
    return pl.BlockSpec((None, D_MODEL, width), lambda i: (layer, 0, col // width),
                        pipeline_mode=pl.Buffered(1))


def _state_spec(layer, group, tail):
    zeros = (0,) * len(tail)
    if layer == 0:
        return pl.BlockSpec((group, DEPTH) + tail, lambda i: (i, 0) + zeros)
    return pl.BlockSpec((group, None) + tail, lambda i: (i, layer) + zeros)


def _store_state(ref, g, layer, value, *sub):
    if layer == 0:
        ref[(g, 0) + sub] = value
        for later in range(1, DEPTH):
            ref[(g, later) + sub] = jnp.zeros_like(value)
    else:
        ref[(g,) + sub] = value


def _log_gamma(decay):
    x = -decay
    return -(jnp.maximum(x, 0.0) + jnp.log1p(jnp.exp(-jnp.abs(x))))


def _ada_kernel(cond_ref, w_ref, b_ref, o_ref):
    s = _silu(cond_ref[...])
    o_ref[...] = _bdot(s, w_ref[...]) + b_ref[...]


def _ada_call(cond, w_ada, b_ada):
    rows = cond.shape[0]
    n_out = N_MOD * D_MODEL
    return pl.pallas_call(
        _ada_kernel,
        grid=(DEPTH, n_out // ADA_TILE),
        in_specs=[
            pl.BlockSpec((rows, D_MODEL), lambda l, j: (0, 0)),
            pl.BlockSpec((None, D_MODEL, ADA_TILE), lambda l, j: (l, 0, j)),
            pl.BlockSpec((None, 1, ADA_TILE), lambda l, j: (l, 0, j)),
        ],
        out_specs=pl.BlockSpec((None, rows, ADA_TILE), lambda l, j: (l, 0, j)),
        out_shape=jax.ShapeDtypeStruct((DEPTH, rows, n_out), F32),
        compiler_params=_cparams(("parallel", "parallel")),
        name="ada_mod",
    )(cond, w_ada, b_ada.reshape(DEPTH, 1, n_out))


def _swiglu_rows(x, m, mod_base, gain, wi_ref, wo_ref, act_s, rows):
    h = (_rms(x, gain) * (1.0 + m[mod_base + 1:mod_base + 2]) + m[mod_base:mod_base + 1]).astype(BF16)
    for c in range(D_FF // FF_TILE):
        a = jnp.dot(h, wi_ref[:, c * FF_TILE:(c + 1) * FF_TILE], preferred_element_type=F32)
        g = jnp.dot(h, wi_ref[:, D_FF + c * FF_TILE:D_FF + (c + 1) * FF_TILE], preferred_element_type=F32)
        act_s[rows, c * FF_TILE:(c + 1) * FF_TILE] = (_silu(a) * g).astype(BF16)
    out = jnp.dot(act_s[rows, :], wo_ref[...], preferred_element_type=F32)
    return x + 0.5 * m[mod_base + 2:mod_base + 3] * out


def _row_groups(n_rows):
    per = n_rows // ROW_SPLIT
    return [slice(p * per, (p + 1) * per) for p in range(ROW_SPLIT)]


def _ffn_a_kernel(x_ref, mod_ref, gain_ref, gain2_ref, wi_ref, wo_ref, y_ref, h2_ref, act_s):
    m = mod_ref[...]
    for rows in _row_groups(x_ref.shape[0]):
        y = _swiglu_rows(x_ref[rows, :], m, 0, gain_ref[...], wi_ref, wo_ref, act_s, rows)
        y_ref[rows, :] = y
        h2_ref[rows, :] = (_rms(y, gain2_ref[...]) * (1.0 + m[4:5]) + m[3:4]).astype(BF16)


def _resident(block_shape, index_map):
    return pl.BlockSpec(block_shape, index_map, pipeline_mode=pl.Buffered(1))


def _ffn_a_call(x, mod, mod_row, gain, gain2, w_in, w_out, layer, name):
    nt, tm, _ = x.shape
    tok = pl.BlockSpec((None, tm, D_MODEL), lambda i: (i, 0, 0))
    vec = pl.BlockSpec((1, D_MODEL), lambda i: (0, 0))
    return pl.pallas_call(
        _ffn_a_kernel,
        grid=(nt,),
        in_specs=[
            tok, pl.BlockSpec((None, N_MOD, D_MODEL), lambda i: (mod_row(i), 0, 0)), vec, vec,
            _resident((None, D_MODEL, 2 * D_FF), lambda i: (layer, 0, 0)),
            _resident((None, D_FF, D_MODEL), lambda i: (layer, 0, 0)),
        ],
        out_specs=[tok, tok],
        out_shape=[jax.ShapeDtypeStruct(x.shape, F32), jax.ShapeDtypeStruct(x.shape, BF16)],
        scratch_shapes=[pltpu.VMEM((tm, D_FF), BF16)],
        compiler_params=_cparams(("parallel",)),
        name=name,
    )(x, mod, gain, gain2, w_in, w_out)


def _hy_filter_kernel(feats_ref, w1_ref, b1_ref, w2_ref, b2_ref, w3_ref, win_ref,
                      ch_ref, cl_ref, sh_ref, sl_ref, kr_ref, ki_ref, kn_ref, *, seq):
    n_fft = 2 * seq
    z = jnp.sin(HY_SIN_W * (_dot3(feats_ref[...], w1_ref[...]) + b1_ref[...]))
    z = jnp.sin(HY_SIN_W * (_dot3(z, w2_ref[...]) + b2_ref[...]))
    zz = _dot3(z, w3_ref[...])
    win = win_ref[...]
    hf = zz[:, :HY_W] * win
    hb = zz[:, HY_W:] * win
    norm = (jnp.sum(jnp.abs(hf), axis=0, keepdims=True)
            + jnp.sum(jnp.abs(hb), axis=0, keepdims=True))
    hf = hf / norm
    hb = hb / norm
    row = _row(hf.shape)
    hb0 = jnp.where(row == 0, 0.0, hb)
    even = hf + hb0
    odd = hb0 - hf
    wk = jnp.where(row == 0, 1.0 / n_fft, 2.0 / n_fft)
    kr_ref[...] = _dot3_pre(ch_ref[...], cl_ref[...], even) * wk
    ki_ref[...] = _dot3_pre(sh_ref[...], sl_ref[...], odd) * wk
    sgn = jnp.where((row & 1) == 0, 1.0, -1.0)
    kn_ref[...] = jnp.sum(even * sgn, axis=0, keepdims=True) * (1.0 / n_fft)


def _hy_filter_call(seq, feats, w1, b1, w2, b2, w3, win, ch, cl, sh, sl):
    args = (feats, w1, b1, w2, b2, w3, win, ch, cl, sh, sl)
    return pl.pallas_call(
        functools.partial(_hy_filter_kernel, seq=seq),
        out_shape=[jax.ShapeDtypeStruct((seq, HY_W), F32),
                   jax.ShapeDtypeStruct((seq, HY_W), F32),
                   jax.ShapeDtypeStruct((1, HY_W), F32)],
        compiler_params=pltpu.CompilerParams(vmem_limit_bytes=VMEM_LIMIT),
        name=f"hyena_filter_{seq}",
    )(*args)


def _hyena_kernel(h_ref, w_ref, cw_ref, cb_ref, kr_ref, ki_ref, kn_ref, skip_ref,
                  c_ref, s_ref, y_ref, w_s, *, group, seq):
    _cast_weights_once((w_ref,), w_s)
    proj = jnp.dot(h_ref[...], w_s[...], preferred_element_type=F32)
    cw = cw_ref[...]
    cmat = c_ref[...]
    smat = s_ref[...]
    kr = kr_ref[...]
    ki = ki_ref[...]
    row = _row((seq, 3 * HY_W))
    row_w = _row((seq, HY_W))
    sgn = jnp.where((row_w & 1) == 0, 1.0, -1.0)
    for g in range(group):
        hy = proj[g * seq:(g + 1) * seq]
        prev = jnp.where(row == 0, 0.0, pltpu.roll(hy, 1, axis=0))
        nxt = jnp.where(row == seq - 1, 0.0, pltpu.roll(hy, seq - 1, axis=0))
        u = prev * cw[0:1] + hy * cw[1:2] + nxt * cw[2:3] + cb_ref[...]
        v, x0, x1 = u[:, :HY_W], u[:, HY_W:2 * HY_W], u[:, 2 * HY_W:]
        z = v * x1
        zb = z.astype(BF16)
        zr = jnp.dot(cmat, zb, preferred_element_type=F32)
        zs = jnp.dot(smat, zb, preferred_element_type=F32)
        yr = (zr * kr + zs * ki).astype(BF16)
        yi = (zr * ki - zs * kr).astype(BF16)
        nyq = jnp.sum(z * sgn, axis=0, keepdims=True) * kn_ref[...]
        conv = (jnp.dot(cmat, yr, preferred_element_type=F32)
                - jnp.dot(smat, yi, preferred_element_type=F32) + sgn * nyq)
        y_ref[g * seq:(g + 1) * seq, :] = (x0 * (conv + skip_ref[...] * z)).astype(BF16)


def _hyena_call(h, w_in, layer, cw, cb, kr, ki, kn, skip, cmat, smat, group, seq, name):
    nt, tm, _ = h.shape
    const2 = lambda a: pl.BlockSpec(a.shape, lambda i: (0, 0))
    return pl.pallas_call(
        functools.partial(_hyena_kernel, group=group, seq=seq),
        grid=(nt,),
        in_specs=[
            pl.BlockSpec((None, tm, D_MODEL), lambda i: (i, 0, 0)),
            _w_in_spec(layer, COL_HY, 3 * HY_W),
            const2(cw), const2(cb), const2(kr), const2(ki), const2(kn), const2(skip),
            const2(cmat), const2(smat),
        ],
        out_specs=pl.BlockSpec((None, tm, HY_W), lambda i: (i, 0, 0)),
        out_shape=jax.ShapeDtypeStruct((nt, tm, HY_W), BF16),
        scratch_shapes=[pltpu.VMEM((D_MODEL, 3 * HY_W), BF16)],
        compiler_params=_cparams(("arbitrary",)),
        name=name,
    )(h, w_in, cw, cb, kr, ki, kn, skip, cmat, smat)


def _diff_kernel(*refs, group, seq, past, lam_init, layer):
    has_cache = past > 0
    if has_cache:
        (h_ref, w_ref, gq_ref, gk_ref, lam_ref, gs_ref, cos_ref, sin_ref,
         ck_ref, cv_ref, y_ref, w_s, k_s, v_s) = refs
    else:
        h_ref, w_ref, gq_ref, gk_ref, lam_ref, gs_ref = refs[:6]
        y_ref, ko_ref, vo_ref, w_s = refs[-4:]
    width = 2 * DIFF_HEADS * HEAD_DIM
    _cast_weights_once((w_ref,), w_s)
    proj = jnp.dot(h_ref[...], w_s[...], preferred_element_type=F32)
    q, k, v = proj[:, :width], proj[:, width:2 * width], proj[:, 2 * width:]
    q = _rms_heads(q, gq_ref[...])
    k = _rms_heads(k, gk_ref[...])
    dl = lam_ref[...]
    lam = (jnp.exp(jnp.sum(dl[0:1] * dl[1:2], axis=1, keepdims=True))
           - jnp.exp(jnp.sum(dl[2:3] * dl[3:4], axis=1, keepdims=True)) + lam_init)
    ones = jnp.ones((seq, LANE), BF16)
    if has_cache:
        cos, sin = cos_ref[...], sin_ref[...]
        q = jnp.concatenate([_rope128(q[:, s * LANE:(s + 1) * LANE], cos, sin)
                             for s in range(width // LANE)], axis=1)
        k = jnp.concatenate([_rope128(k[:, s * LANE:(s + 1) * LANE], cos, sin)
                             for s in range(width // LANE)], axis=1)
        k_s[0:seq, :] = k.astype(BF16)
        k_s[seq:seq + past, :] = ck_ref[...].astype(BF16)
        for hh in range(DIFF_HEADS):
            v_s[0:seq, 2 * hh * LANE:(2 * hh + 1) * LANE] = v[:, hh * LANE:(hh + 1) * LANE].astype(BF16)
            v_s[seq:seq + past, 2 * hh * LANE:(2 * hh + 1) * LANE] = (
                cv_ref[:, hh * LANE:(hh + 1) * LANE].astype(BF16))
            v_s[:, (2 * hh + 1) * LANE:(2 * hh + 2) * LANE] = jnp.ones((seq + past, LANE), BF16)
    else:
        for g in range(group):
            _store_state(ko_ref, g, layer, k[g * seq:(g + 1) * seq])
            _store_state(vo_ref, g, layer, v[g * seq:(g + 1) * seq])
    q = q * (HEAD_DIM ** -0.5 * LOG2_E)
    tq = min(seq, ATT_Q_TILE)
    lane = _lane((tq, LANE))
    for g in range(group):
        for hh in range(DIFF_HEADS):
            sl = slice(hh * LANE, (hh + 1) * LANE)
            if has_cache:
                keys, vals = k_s[:, sl], v_s[:, 2 * hh * LANE:(2 * hh + 2) * LANE]
            else:
                keys = k[g * seq:(g + 1) * seq, sl].astype(BF16)
                vals = jnp.concatenate([v[g * seq:(g + 1) * seq, sl].astype(BF16), ones], axis=1)
            for qi in range(seq // tq):
                r0 = g * seq + qi * tq
                q128 = q[r0:r0 + tq, sl]
                qs = jnp.concatenate([jnp.where(lane < HEAD_DIM, q128, 0.0),
                                      jnp.where(lane >= HEAD_DIM, q128, 0.0)], axis=0)
                s = _bdot_nt(qs, keys)
                e = jnp.exp2(s - jnp.max(s, axis=-1, keepdims=True)).astype(BF16)
                oe = jnp.dot(e, vals, preferred_element_type=F32)
                r = 1.0 / oe[:, LANE:]
                o = oe[:tq, :LANE] * r[:tq] - oe[tq:, :LANE] * (lam * r[tq:])
                y = _rms(o, gs_ref[:, sl]) * (1.0 - lam_init)
                y_ref[r0:r0 + tq, sl] = y.astype(BF16)


def _diff_call(h, w_in, layer, gq, gk, lam, gs, lam_init, group, seq,
               rope=None, cache=None, states=(), name="diff"):
    nt, tm = h.shape[0], h.shape[1]
    width = 2 * DIFF_HEADS * HEAD_DIM
    const2 = lambda a: pl.BlockSpec(a.shape, lambda i: (0, 0))
    in_specs = [
        pl.BlockSpec((None, tm, D_MODEL), lambda i: (i, 0, 0)),
        _w_in_spec(layer, COL_DIFF, 3 * width),
        const2(gq), const2(gk), const2(lam), const2(gs),
    ]
    args = [h, w_in, gq, gk, lam, gs]
    y_spec = pl.BlockSpec((None, tm, width), lambda i: (i, 0, 0))
    y_shape = jax.ShapeDtypeStruct((nt, tm, width), BF16)
    scratch = [pltpu.VMEM((D_MODEL, 3 * width), BF16)]
    aliases = {}
    if cache is None:
        past = 0
        s_spec = _state_spec(layer, group, (seq, width))
        s_shape = jax.ShapeDtypeStruct((nt * group, DEPTH, seq, width), F32)
        out_specs = [y_spec, s_spec, s_spec]
        out_shape = [y_shape, s_shape, s_shape]
        aliases = {len(args) + n: 1 + n for n in range(len(states))}
        in_specs += [pl.BlockSpec(memory_space=pl.ANY)] * len(states)
        args += list(states)
    else:
        cos, sin = rope
        ck, cv = cache
        past = ck.shape[2]
        cspec = pl.BlockSpec((None, None, past, width), lambda i: (i, layer, 0, 0))
        in_specs += [const2(cos), const2(sin), cspec, cspec]
        args += [cos, sin, ck, cv]
        out_specs = [y_spec]
        out_shape = [y_shape]
        scratch += [pltpu.VMEM((seq + past, width), BF16), pltpu.VMEM((seq + past, 2 * width), BF16)]
    return pl.pallas_call(
        functools.partial(_diff_kernel, group=group, seq=seq, past=past, lam_init=lam_init, layer=layer),
        grid=(nt,),
        in_specs=in_specs,
        out_specs=out_specs,
        out_shape=out_shape,
        scratch_shapes=scratch,
        input_output_aliases=aliases,
        compiler_params=_cparams(("arbitrary",)),
        name=name,
    )(*args)


def _win_heads(q, hk, lane):
    out = []
    for gq in range(WIN_GROUP):
        j = hk * WIN_GROUP + gq
        slab = q[:, (j // 2) * LANE:(j // 2 + 1) * LANE]
        if j % 2 != hk:
            slab = pltpu.roll(slab, HEAD_DIM, axis=1)
        out.append(jnp.where(lane >= HEAD_DIM if hk == 1 else lane < HEAD_DIM, slab, 0.0))
    return out


def _win_place(o_heads, hk, lane):
    slabs = []
    for pair in range(WIN_GROUP // 2):
        halves = []
        for gq in (2 * pair, 2 * pair + 1):
            j = hk * WIN_GROUP + gq
            o = o_heads[gq]
            if j % 2 != hk:
                o = pltpu.roll(o, HEAD_DIM, axis=1)
            halves.append(o)
        slabs.append(jnp.where(lane < HEAD_DIM, halves[0], halves[1]))
    return slabs


def _sink_col(sink_ref, layer, hk, rows_per_head):
    rows = WIN_GROUP * rows_per_head
    r = _row((rows, 1))
    col = jnp.full((rows, 1), sink_ref[layer, hk * WIN_GROUP], F32)
    for gq in range(1, WIN_GROUP):
        col = jnp.where(r >= gq * rows_per_head, sink_ref[layer, hk * WIN_GROUP + gq], col)
    return col


def _win_kernel(*refs, group, seq, past, layer):
    has_cache = past > 0
    if has_cache:
        (sink_ref, h_ref, w_ref, gq_ref, gk_ref, cos_ref, sin_ref, ck_ref, cv_ref,
         y_ref, w_s, k_s, v_s) = refs
    else:
        sink_ref, h_ref, w_ref, gq_ref, gk_ref = refs[:5]
        y_ref, ko_ref, vo_ref, w_s = refs[-4:]
    qw = WIN_Q_HEADS * HEAD_DIM
    kw = WIN_KV_HEADS * HEAD_DIM
    _cast_weights_once((w_ref,), w_s)
    proj = jnp.dot(h_ref[...], w_s[...], preferred_element_type=F32)
    q, k, v = proj[:, :qw], proj[:, qw:qw + kw], proj[:, qw + kw:]
    q = _rms_heads(q, gq_ref[...])
    k = _rms_heads(k, gk_ref[...])
    scale = HEAD_DIM ** -0.5 * LOG2_E
    if not has_cache:
        for g in range(group):
            _store_state(ko_ref, g, layer, k[g * seq:(g + 1) * seq])
            _store_state(vo_ref, g, layer, v[g * seq:(g + 1) * seq])
        q = q * scale
        lane = _lane((seq, LANE))
        ones = jnp.ones((seq, LANE), BF16)
        for g in range(group):
            rows = slice(g * seq, (g + 1) * seq)
            kb = k[rows].astype(BF16)
            vb = jnp.concatenate([v[rows].astype(BF16), ones], axis=1)
            qg = q[rows]
            for hk in range(WIN_KV_HEADS):
                qs = jnp.concatenate(_win_heads(qg, hk, lane), axis=0)
                s = _bdot_nt(qs, kb)
                sink = _sink_col(sink_ref, layer, hk, seq) * LOG2_E
                m = jnp.maximum(jnp.max(s, axis=-1, keepdims=True), sink)
                oe = jnp.dot(jnp.exp2(s - m).astype(BF16), vb, preferred_element_type=F32)
                o = oe[:, :LANE] * (1.0 / (oe[:, LANE:] + jnp.exp2(sink - m)))
                slabs = _win_place([o[gq * seq:(gq + 1) * seq] for gq in range(WIN_GROUP)], hk, lane)
                for pair, slab in enumerate(slabs):
                    c0 = (hk * 2 + pair) * LANE
                    y_ref[rows, c0:c0 + LANE] = slab.astype(BF16)
        return

    cos, sin = cos_ref[...], sin_ref[...]
    q = jnp.concatenate([_rope128(q[:, s * LANE:(s + 1) * LANE], cos, sin)
                         for s in range(qw // LANE)], axis=1) * scale
    k = _rope128(k, cos, sin)
    zpad = jnp.zeros((BLOCK, kw), BF16)
    k_s[0:BLOCK, :] = zpad
    k_s[BLOCK:BLOCK + seq, :] = k.astype(BF16)
    k_s[BLOCK + seq:2 * BLOCK + seq, :] = zpad
    v_s[0:BLOCK, 0:kw] = zpad
    v_s[BLOCK:BLOCK + seq, 0:kw] = v.astype(BF16)
    v_s[BLOCK + seq:2 * BLOCK + seq, 0:kw] = zpad
    v_s[:, kw:2 * kw] = jnp.ones((seq + 2 * BLOCK, kw), BF16)
    ckb = ck_ref[...].astype(BF16)
    cvb = jnp.concatenate([cv_ref[...].astype(BF16), jnp.ones((past, kw), BF16)], axis=1)
    lane_l = _lane((seq, LANE))
    lane_b = _lane((BLOCK, LANE))
    nb = seq // BLOCK
    rows_q = WIN_GROUP * BLOCK
    kk = _lane((rows_q, 3 * BLOCK))
    qi = _row((rows_q, 3 * BLOCK)) & (BLOCK - 1)
    rel = kk - BLOCK - qi
    bias_mid = jnp.where(rel <= WINDOW, jnp.where(rel >= -WINDOW, 0.0, NEG_INF), NEG_INF)
    bias = {(False, False): bias_mid,
            (True, False): jnp.where(kk >= BLOCK, bias_mid, NEG_INF),
            (False, True): jnp.where(kk < 2 * BLOCK, bias_mid, NEG_INF)}
    bias[(True, True)] = jnp.where(kk < 2 * BLOCK, bias[(True, False)], NEG_INF)
    for hk in range(WIN_KV_HEADS):
        heads = _win_heads(q, hk, lane_l)
        sink = _sink_col(sink_ref, layer, hk, BLOCK) * LOG2_E
        for n in range(nb):
            qs = jnp.concatenate([hd[n * BLOCK:(n + 1) * BLOCK] for hd in heads], axis=0)
            sb = _bdot_nt(qs, k_s[n * BLOCK:(n + 3) * BLOCK, :]) + bias[(n == 0, n == nb - 1)]
            sc = _bdot_nt(qs, ckb)
            m = jnp.maximum(jnp.maximum(jnp.max(sb, axis=-1, keepdims=True),
                                        jnp.max(sc, axis=-1, keepdims=True)), sink)
            eb = jnp.exp2(sb - m).astype(BF16)
            ec = jnp.exp2(sc - m).astype(BF16)
            oe = (jnp.dot(eb, v_s[n * BLOCK:(n + 3) * BLOCK, :], preferred_element_type=F32)
                  + jnp.dot(ec, cvb, preferred_element_type=F32))
            o = oe[:, :LANE] * (1.0 / (oe[:, LANE:] + jnp.exp2(sink - m)))
            slabs = _win_place([o[gq * BLOCK:(gq + 1) * BLOCK] for gq in range(WIN_GROUP)], hk, lane_b)
            for pair, slab in enumerate(slabs):
                c0 = (hk * 2 + pair) * LANE
                y_ref[n * BLOCK:(n + 1) * BLOCK, c0:c0 + LANE] = slab.astype(BF16)


def _win_call(h, w_in, layer, sink, gq, gk, group, seq, rope=None, cache=None, states=(), name="win"):
    nt, tm = h.shape[0], h.shape[1]
    qw = WIN_Q_HEADS * HEAD_DIM
    kw = WIN_KV_HEADS * HEAD_DIM
    const2 = lambda a: pl.BlockSpec(a.shape, lambda i: (0, 0))
    in_specs = [
        pl.BlockSpec(memory_space=pltpu.SMEM),
        pl.BlockSpec((None, tm, D_MODEL), lambda i: (i, 0, 0)),
        _w_in_spec(layer, COL_WIN, qw + 2 * kw),
        const2(gq), const2(gk),
    ]
    args = [sink, h, w_in, gq, gk]
    y_spec = pl.BlockSpec((None, tm, qw), lambda i: (i, 0, 0))
    y_shape = jax.ShapeDtypeStruct((nt, tm, qw), BF16)
    scratch = [pltpu.VMEM((D_MODEL, qw + 2 * kw), BF16)]
    aliases = {}
    if cache is None:
        past = 0
        kv_spec = _state_spec(layer, group, (seq, kw))
        kv_shape = jax.ShapeDtypeStruct((nt * group, DEPTH, seq, kw), F32)
        out_specs = [y_spec, kv_spec, kv_spec]
        out_shape = [y_shape, kv_shape, kv_shape]
        aliases = {len(args) + n: 1 + n for n in range(len(states))}
        in_specs += [pl.BlockSpec(memory_space=pl.ANY)] * len(states)
        args += list(states)
    else:
        cos, sin = rope
        ck, cv = cache
        past = ck.shape[2]
        cspec = pl.BlockSpec((None, None, past, kw), lambda i: (i, layer, 0, 0))
        in_specs += [const2(cos), const2(sin), cspec, cspec]
        args += [cos, sin, ck, cv]
        out_specs = [y_spec]
        out_shape = [y_shape]
        scratch += [pltpu.VMEM((seq + 2 * BLOCK, kw), BF16), pltpu.VMEM((seq + 2 * BLOCK, 2 * kw), BF16)]
    return pl.pallas_call(
        functools.partial(_win_kernel, group=group, seq=seq, past=past, layer=layer),
        grid=(nt,),
        in_specs=in_specs,
        out_specs=out_specs,
        out_shape=out_shape,
        scratch_shapes=scratch,
        input_output_aliases=aliases,
        compiler_params=_cparams(("arbitrary",)),
        name=name,
    )(*args)


def _ret_kernel(*refs, group, seq, has_state, layer):
    if has_state:
        h_ref, wa_ref, wb_ref, df_ref, db_ref, s0f_ref, s0b_ref, y_ref, w_s, dec_s = refs
    else:
        h_ref, wa_ref, wb_ref, df_ref, db_ref = refs[:5]
        y_ref, sf_ref, sb_ref, w_s, dec_s = refs[-5:]
    qk_w = RET_HEADS * RET_DK
    v_w = RET_HEADS * RET_DV
    _cast_weights_once((wa_ref, wb_ref), w_s)
    proj = jnp.dot(h_ref[...], w_s[...], preferred_element_type=F32)
    q = proj[:, :qk_w]
    k = proj[:, qk_w:2 * qk_w] * (RET_DK ** -0.5)
    v = proj[:, 2 * qk_w:2 * qk_w + v_w]
    rg = proj[:, 2 * qk_w + v_w:]
    lgf_all = _log_gamma(df_ref[...])
    lgb_all = _log_gamma(db_ref[...])
    tq = min(seq, ATT_Q_TILE)
    n_q = seq // tq

    @pl.when(pl.program_id(0) == 0)
    def _():
        width = 2 * seq - tq
        rel = (_row((tq, width)) - _lane((tq, width)) + (seq - tq)).astype(F32)
        for hd in range(RET_HEADS):
            lgf = lgf_all[hd:hd + 1, 0:1]
            lgb = lgb_all[hd:hd + 1, 0:1]
            dec_s[hd] = (jnp.where(rel >= 0, jnp.exp(jnp.maximum(rel, 0.0) * lgf), 0.0)
                         + jnp.where(rel <= 0, jnp.exp(jnp.maximum(-rel, 0.0) * lgb), 0.0))

    lane = _lane((seq, LANE))
    t_full = _row((seq, LANE)).astype(F32)
    for hd in range(RET_HEADS):
        slab = slice((hd // 2) * LANE, (hd // 2 + 1) * LANE)
        vsl = slice(hd * RET_DV, (hd + 1) * RET_DV)
        half = hd % 2
        lgf = lgf_all[hd:hd + 1, 0:1]
        lgb = lgb_all[hd:hd + 1, 0:1]
        for g in range(group):
            rows = slice(g * seq, (g + 1) * seq)
            qm = jnp.where(lane >= RET_DK if half == 1 else lane < RET_DK, q[rows, slab], 0.0)
            k128 = k[rows, slab]
            kb = k128.astype(BF16)
            vb = v[rows, vsl].astype(BF16)
            if has_state:
                q_f = qm * jnp.exp((t_full + 1.0) * lgf)
                q_b = qm * jnp.exp((seq - t_full) * lgb)
                o_state = (_bdot(q_f, s0f_ref[hd // 2]) + _bdot(q_b, s0b_ref[hd // 2]))
            else:
                k_f = k128 * jnp.exp((seq - 1.0 - t_full) * lgf)
                k_b = k128 * jnp.exp(t_full * lgb)
                s_f = _bdot(k_f.T, vb)
                s_b = _bdot(k_b.T, vb)
                _store_state(sf_ref, g, layer, s_f[half * RET_DK:(half + 1) * RET_DK], hd)
                _store_state(sb_ref, g, layer, s_b[half * RET_DK:(half + 1) * RET_DK], hd)
            for qi in range(n_q):
                qrows = slice(qi * tq, (qi + 1) * tq)
                s = _bdot_nt(qm[qrows], kb)
                off = (n_q - 1 - qi) * tq
                dec = dec_s[hd, :, off:off + seq]
                o = jnp.dot((s * dec).astype(BF16), vb, preferred_element_type=F32)
                if has_state:
                    o = o + o_state[qrows]
                r0 = g * seq + qi * tq
                y = _rms(o, None) * _silu(rg[r0:r0 + tq, vsl])
                y_ref[r0:r0 + tq, vsl] = y.astype(BF16)


def _ret_call(h, w_in, layer, dec_f, dec_b, group, seq, state=None, states=(), name="ret"):
    nt, tm = h.shape[0], h.shape[1]
    v_w = RET_HEADS * RET_DV
    ret_cols = COL_GATE - COL_RET
    const2 = lambda a: pl.BlockSpec(a.shape, lambda i: (0, 0))
    in_specs = [
        pl.BlockSpec((None, tm, D_MODEL), lambda i: (i, 0, 0)),
        _w_in_spec(layer, COL_RET, ret_cols // 2),
        _w_in_spec(layer, COL_RET + ret_cols // 2, ret_cols // 2),
        const2(dec_f), const2(dec_b),
    ]
    args = [h, w_in, w_in, dec_f, dec_b]
    y_spec = pl.BlockSpec((None, tm, v_w), lambda i: (i, 0, 0))
    y_shape = jax.ShapeDtypeStruct((nt, tm, v_w), BF16)
    aliases = {}
    if state is None:
        s_spec = _state_spec(layer, group, (RET_HEADS, RET_DK, RET_DV))
        s_shape = jax.ShapeDtypeStruct((nt * group, DEPTH, RET_HEADS, RET_DK, RET_DV), F32)
        out_specs = [y_spec, s_spec, s_spec]
        out_shape = [y_shape, s_shape, s_shape]
        aliases = {len(args) + n: 1 + n for n in range(len(states))}
        in_specs += [pl.BlockSpec(memory_space=pl.ANY)] * len(states)
        args += list(states)
    else:
        s0f, s0b = state
        sspec = pl.BlockSpec((None, None, 2, LANE, RET_DV), lambda i: (i, layer, 0, 0, 0))
        in_specs += [sspec, sspec]
        args += [s0f, s0b]
        out_specs = [y_spec]
        out_shape = [y_shape]
    return pl.pallas_call(
        functools.partial(_ret_kernel, group=group, seq=seq, has_state=state is not None, layer=layer),
        grid=(nt,),
        in_specs=in_specs,
        out_specs=out_specs,
        out_shape=out_shape,
        input_output_aliases=aliases,
        scratch_shapes=[pltpu.VMEM((D_MODEL, ret_cols), BF16),
                        pltpu.VMEM((RET_HEADS, min(seq, ATT_Q_TILE), 2 * seq - min(seq, ATT_Q_TILE)), F32)],
        compiler_params=_cparams(("arbitrary",)),
        name=name,
    )(*args)


def _merge_ffn_kernel(x_ref, h_ref, mod_ref, gain_ref, y0_ref, y1_ref, y2_ref, y3_ref,
                      wg_ref, wb_ref, wo_ref, wi_ref, wo2_ref, o_ref, merged_s, act_s):
    m = mod_ref[...]
    for rows in _row_groups(x_ref.shape[0]):
        h = h_ref[rows, :]
        ys = [r[rows, :] for r in (y0_ref, y1_ref, y2_ref, y3_ref)]
        for c in range(D_MODEL // MERGE_TILE):
            cols = slice(c * MERGE_TILE, (c + 1) * MERGE_TILE)
            acc = None
            for b in range(N_BRANCH):
                gate = jax.nn.sigmoid(jnp.dot(
                    h, wg_ref[:, b * D_MODEL + c * MERGE_TILE:b * D_MODEL + (c + 1) * MERGE_TILE],
                    preferred_element_type=F32))
                part = gate * jnp.dot(ys[b], wb_ref[b, :, cols], preferred_element_type=F32)
                acc = part if acc is None else acc + part
            merged_s[rows, cols] = acc.astype(BF16)
        mixed = jnp.dot(merged_s[rows, :], wo_ref[...], preferred_element_type=F32)
        x = x_ref[rows, :] + m[5:6] * mixed
        o_ref[rows, :] = _swiglu_rows(x, m, 6, gain_ref[...], wi_ref, wo2_ref, act_s, rows)


def _merge_ffn_call(x, h, mod, mod_row, gain, ys, w_gate, w_branch, w_out, w_ff_in, w_ff_out, layer, name):
    nt, tm, _ = x.shape
    tok = pl.BlockSpec((None, tm, D_MODEL), lambda i: (i, 0, 0))
    ysp = pl.BlockSpec((None, tm, BRANCH_W), lambda i: (i, 0, 0))
    return pl.pallas_call(
        _merge_ffn_kernel,
        grid=(nt,),
        in_specs=[
            tok, tok,
            pl.BlockSpec((None, N_MOD, D_MODEL), lambda i: (mod_row(i), 0, 0)),
            pl.BlockSpec((1, D_MODEL), lambda i: (0, 0)),
            ysp, ysp, ysp, ysp,
            _resident((None, D_MODEL, N_BRANCH * D_MODEL), lambda i: (layer, 0, 0)),
            _resident((None, N_BRANCH, BRANCH_W, D_MODEL), lambda i: (layer, 0, 0, 0)),
            _resident((None, D_MODEL, D_MODEL), lambda i: (layer, 0, 0)),
            _resident((None, D_MODEL, 2 * D_FF), lambda i: (layer, 0, 0)),
            _resident((None, D_FF, D_MODEL), lambda i: (layer, 0, 0)),
        ],
        out_specs=tok,
        out_shape=jax.ShapeDtypeStruct(x.shape, F32),
        scratch_shapes=[pltpu.VMEM((tm, D_MODEL), BF16), pltpu.VMEM((tm, D_FF), BF16)],
        compiler_params=_cparams(("parallel",)),
        name=name,
    )(x, h, mod, gain, *ys, w_gate, w_branch, w_out, w_ff_in, w_ff_out)


def _dft_tables(seq):
    n_fft = 2 * seq
    idx = np.arange(seq, dtype=np.int64)
    ang = 2.0 * np.pi * ((idx[:, None] * idx[None, :]) % n_fft).astype(np.float64) / n_fft
    out = []
    for m in (np.cos(ang), np.sin(ang)):
        hi = jnp.asarray(m, F32).astype(BF16)
        lo = (jnp.asarray(m, F32) - hi.astype(F32)).astype(BF16)
        out += [hi, lo]
    return out


def _hyena_tables(seq):
    t = np.arange(seq, dtype=np.float32) / np.float32(seq)
    f = np.arange(1, HY_BANDS + 1, dtype=np.float32)
    ang = np.float32(2.0 * math.pi) * t[:, None] * f[None, :]
    feats = np.zeros((seq, LANE), np.float32)
    feats[:, 0] = t
    feats[:, 1:1 + HY_BANDS] = np.sin(ang)
    feats[:, 1 + HY_BANDS:HY_EMB] = np.cos(ang)
    min_decay = math.log(HY_TARGET) / HY_DECAY_LONG_PCT
    max_decay = math.log(HY_TARGET) / HY_DECAY_SHORT_PCT
    deltas = np.linspace(min_decay, max_decay, HY_W, dtype=np.float32)
    window = np.exp(-t[:, None] * np.abs(deltas)[None, :]).astype(np.float32)
    return jnp.asarray(feats), jnp.asarray(window)


def _rope_tables(seq):
    pos = np.arange(seq)
    row = (pos // GRID_W).astype(np.float32)
    col = (pos % GRID_W).astype(np.float32)
    inv_freq = (np.float32(ROPE_BASE) ** (-np.arange(ROPE_PAIRS, dtype=np.float32) / np.float32(ROPE_PAIRS)))
    lane = np.arange(LANE)
    in_head = lane % HEAD_DIM
    use_col = in_head >= HEAD_DIM // 2
    pair = in_head % ROPE_PAIRS
    second = (in_head % (2 * ROPE_PAIRS)) >= ROPE_PAIRS
    p = np.where(use_col[None, :], col[:, None], row[:, None]).astype(np.float32)
    ang = (p * inv_freq[pair][None, :]).astype(np.float32)
    cos = np.cos(ang).astype(np.float32)
    sin = np.sin(ang).astype(np.float32)
    sin = np.where(second[None, :], sin, -sin)
    return jnp.asarray(cos), jnp.asarray(sin)


def kernel(x_prompt, x_sample, c, cache_diff_k, cache_diff_v, cache_win_k, cache_win_v, state_ret_f, state_ret_b, c_ctx, norm_ffa, norm_mix, norm_ffb, w_ada, b_ada, w_ffa_in, w_ffa_out, w_ffb_in, w_ffb_out, w_in, hy_conv_w, hy_conv_b, hy_f_w1, hy_f_b1, hy_f_w2, hy_f_b2, hy_f_w3, hy_skip, diff_q_norm, diff_k_norm, diff_lambda, diff_subln, win_q_norm, win_k_norm, win_sink, ret_decay_f, ret_decay_b, w_branch, w_out):
    batch, seq, _ = x_prompt.shape
    dec_batch, dec_seq, _ = x_sample.shape
    past = cache_diff_k.shape[2]
    ctx_group = TOKEN_TILE // seq
    assert TOKEN_TILE % seq == 0 and batch % ctx_group == 0 and dec_seq == TOKEN_TILE
    n_ctx = batch // ctx_group

    n_rows = 16
    cond = jnp.zeros((n_rows, D_MODEL), F32).at[0].set(c_ctx).at[1:1 + dec_batch].set(c)
    mod = _ada_call(cond, w_ada, b_ada).reshape(DEPTH, n_rows, N_MOD, D_MODEL)

    bf = lambda a: a.astype(BF16)
    w_ffa_in_b, w_ffa_out_b, w_ffb_in_b, w_ffb_out_b = bf(w_ffa_in), bf(w_ffa_out), bf(w_ffb_in), bf(w_ffb_out)
    w_gate = bf(w_in[:, :, COL_GATE:])
    w_branch_b, w_out_b = bf(w_branch), bf(w_out)

    rope = _rope_tables(dec_seq)
    tables = {s: (_hyena_tables(s), _dft_tables(s)) for s in (seq, dec_seq)}

    ck_d = cache_diff_k.reshape(dec_batch, DEPTH, past, -1)
    cv_d = cache_diff_v.reshape(dec_batch, DEPTH, past, -1)
    ck_w = cache_win_k.reshape(dec_batch, DEPTH, past, -1)
    cv_w = cache_win_v.reshape(dec_batch, DEPTH, past, -1)
    s0_f = state_ret_f.reshape(dec_batch, DEPTH, 2, LANE, RET_DV)
    s0_b = state_ret_b.reshape(dec_batch, DEPTH, 2, LANE, RET_DV)

    pad_rows = lambda a, n: jnp.pad(a, ((0, n - a.shape[0]), (0, 0)))
    tile_lanes = lambda a, n: jnp.tile(a.reshape(1, -1), (1, n))

    tiled = lambda a, tile: a.reshape(-1, tile, a.shape[-1])
    y_ctx = x_prompt
    y_lat = x_sample
    ctx_row = lambda tile: (lambda i: 0)
    lat_row = lambda tile: (lambda i: i // (dec_seq // tile) + 1)
    st_diff, st_win, st_ret = (), (), ()

    for l in range(DEPTH):
        lam_init = 0.8 - 0.6 * math.exp(-0.3 * l)
        gain = lambda a: a[l].reshape(1, -1)
        gq_d, gk_d = tile_lanes(diff_q_norm[l], 2 * DIFF_HEADS), tile_lanes(diff_k_norm[l], 2 * DIFF_HEADS)
        gs_d = tile_lanes(diff_subln[l], DIFF_HEADS)
        gq_w, gk_w = tile_lanes(win_q_norm[l], WIN_Q_HEADS), tile_lanes(win_k_norm[l], WIN_KV_HEADS)
        dec_f = jnp.broadcast_to(ret_decay_f[l][:, None], (RET_HEADS, LANE))
        dec_b = jnp.broadcast_to(ret_decay_b[l][:, None], (RET_HEADS, LANE))
        w1 = jnp.pad(hy_f_w1[l], ((0, LANE - HY_EMB), (0, LANE - HY_FH)))
        b1 = jnp.pad(hy_f_b1[l].reshape(1, -1), ((0, 0), (0, LANE - HY_FH)))
        w2 = jnp.pad(hy_f_w2[l], ((0, LANE - HY_FH), (0, LANE - HY_FH)))
        b2 = jnp.pad(hy_f_b2[l].reshape(1, -1), ((0, 0), (0, LANE - HY_FH)))
        w3 = pad_rows(hy_f_w3[l], LANE)
        cw, cb, skip = hy_conv_w[l], hy_conv_b[l].reshape(1, -1), hy_skip[l].reshape(1, -1)

        for is_lat in (False, True):
            x = y_lat if is_lat else y_ctx
            s_len = dec_seq if is_lat else seq
            group = 1 if is_lat else ctx_group
            mod_row = lat_row if is_lat else ctx_row
            tag = f"{'lat' if is_lat else 'ctx'}{l}"
            (feats, window), (c_hi, c_lo, s_hi, s_lo) = tables[s_len]

            x, h = _ffn_a_call(tiled(x, FFN_A_TILE), mod[l], mod_row(FFN_A_TILE), gain(norm_ffa), gain(norm_mix),
                               w_ffa_in_b, w_ffa_out_b, l, f"ffn_a_{tag}")
            h = tiled(h, TOKEN_TILE)

            kr, ki, kn = _hy_filter_call(s_len, feats, w1, b1, w2, b2, w3, window, c_hi, c_lo, s_hi, s_lo)
            y_hy = _hyena_call(h, w_in, l, cw, cb, kr, ki, kn, skip, c_hi, s_hi, group, s_len, f"hyena_{tag}")
            if is_lat:
                (y_diff,) = _diff_call(h, w_in, l, gq_d, gk_d, diff_lambda[l], gs_d, lam_init,
                                       group, s_len, rope=rope, cache=(ck_d, cv_d), name=f"diff_{tag}")
                (y_win,) = _win_call(h, w_in, l, win_sink, gq_w, gk_w, group, s_len,
                                     rope=rope, cache=(ck_w, cv_w), name=f"win_{tag}")
                (y_ret,) = _ret_call(h, w_in, l, dec_f, dec_b, group, s_len, state=(s0_f, s0_b),
                                     name=f"ret_{tag}")
            else:
                y_diff, *st_diff = _diff_call(h, w_in, l, gq_d, gk_d, diff_lambda[l], gs_d, lam_init,
                                              group, s_len, states=st_diff, name=f"diff_{tag}")
                y_win, *st_win = _win_call(h, w_in, l, win_sink, gq_w, gk_w, group, s_len,
                                           states=st_win, name=f"win_{tag}")
                y_ret, *st_ret = _ret_call(h, w_in, l, dec_f, dec_b, group, s_len, states=st_ret,
                                           name=f"ret_{tag}")

            x = _merge_ffn_call(tiled(x, MERGE_FFN_TILE), tiled(h, MERGE_FFN_TILE), mod[l], mod_row(MERGE_FFN_TILE),
                                gain(norm_ffb),
                                tuple(tiled(y, MERGE_FFN_TILE) for y in (y_hy, y_diff, y_win, y_ret)),
                                w_gate, w_branch_b, w_out_b, w_ffb_in_b, w_ffb_out_b, l, f"merge_ffn_b_{tag}")
            if is_lat:
                y_lat = x
            else:
                y_ctx = x

    return (y_ctx.reshape(batch, seq, D_MODEL), y_lat.reshape(dec_batch, dec_seq, D_MODEL),
            st_diff[0].reshape(batch, DEPTH, seq, DIFF_HEADS, 2, HEAD_DIM),
            st_diff[1].reshape(batch, DEPTH, seq, DIFF_HEADS, 2 * HEAD_DIM),
            st_win[0].reshape(batch, DEPTH, seq, WIN_KV_HEADS, HEAD_DIM),
            st_win[1].reshape(batch, DEPTH, seq, WIN_KV_HEADS, HEAD_DIM),
            st_ret[0], st_ret[1])
```

```python
import functools
import math

import jax
import jax.numpy as jnp
import numpy as np
from jax import lax
from jax.experimental import pallas as pl
from jax.experimental.pallas import tpu as pltpu

F32 = jnp.float32
BF16 = jnp.bfloat16

D_MODEL = 1024
DEPTH = 2
GRID_W = 64
HEAD_DIM = 64
ROPE_PAIRS = HEAD_DIM // 4
ROPE_BASE = 10000.0
EPS = 1e-6
NEG_INF = -1e30
LOG2_E = math.log2(math.e)
D_FF = 2816
N_MOD = 9
N_BRANCH = 4
BRANCH_W = 512

HY_W = BRANCH_W
HY_BANDS = 16
HY_EMB = 1 + 2 * HY_BANDS
HY_FH = 64
HY_SIN_W = 1.0
HY_TARGET = 1e-2
HY_DECAY_SHORT_PCT = 0.3
HY_DECAY_LONG_PCT = 1.5

DIFF_HEADS = 4
WIN_Q_HEADS = 8
WIN_KV_HEADS = 2
WIN_GROUP = WIN_Q_HEADS // WIN_KV_HEADS
WINDOW = 128
BLOCK = 128
RET_HEADS = 4
RET_DK = 64
RET_DV = 128

COL_HY = 0
COL_DIFF = 3 * HY_W
COL_WIN = COL_DIFF + 3 * (2 * DIFF_HEADS * HEAD_DIM)
COL_RET = COL_WIN + (WIN_Q_HEADS + 2 * WIN_KV_HEADS) * HEAD_DIM
COL_GATE = COL_RET + 2 * RET_HEADS * (RET_DK + RET_DV)
IN_COLS = COL_GATE + N_BRANCH * D_MODEL

LANE = 128
TOKEN_TILE = 1024
FFN_A_TILE = 1024
MERGE_FFN_TILE = 512
FF_TILE = 256
ROW_GROUP = 256
MERGE_TILE = 256
ADA_TILE = 2304
ADA_ROWS = 16
ATT_Q_TILE = 256
VMEM_LIMIT = 56 * 1024 * 1024


def _cparams(sem):
    return pltpu.CompilerParams(dimension_semantics=sem, vmem_limit_bytes=VMEM_LIMIT)


def _bdot(a, b):
    return jnp.dot(a.astype(BF16), b.astype(BF16), preferred_element_type=F32)


def _bdot_nt(a, b):
    return lax.dot_general(a.astype(BF16), b.astype(BF16), (((1,), (1,)), ((), ())),
                           preferred_element_type=F32)


def _split(a):
    hi = a.astype(BF16)
    lo = (a - hi.astype(F32)).astype(BF16)
    return hi, lo


def _dot3_pre(ah, al, b):
    bh, bl = _split(b)
    d = functools.partial(jnp.dot, preferred_element_type=F32)
    return d(ah, bh) + d(ah, bl) + d(al, bh)


def _dot3(a, b):
    return _dot3_pre(*_split(a), b)


def _rms_heads(x, gain):
    first = _lane((x.shape[0], LANE)) < HEAD_DIM
    out = []
    for s in range(x.shape[1] // LANE):
        xs = x[:, s * LANE:(s + 1) * LANE]
        sq = xs * xs
        lo = jnp.sum(jnp.where(first, sq, 0.0), axis=-1, keepdims=True)
        hi = jnp.sum(jnp.where(first, 0.0, sq), axis=-1, keepdims=True)
        r_lo = lax.rsqrt(lo * (1.0 / HEAD_DIM) + EPS)
        r_hi = lax.rsqrt(hi * (1.0 / HEAD_DIM) + EPS)
        out.append(xs * jnp.where(first, r_lo, r_hi))
    return jnp.concatenate(out, axis=1) * gain


def _rms(x, gain):
    y = x * lax.rsqrt(jnp.mean(x * x, axis=-1, keepdims=True) + EPS)
    return y if gain is None else y * gain


def _silu(x):
    return x * jax.nn.sigmoid(x)


def _exp2_bf16(x):
    return jnp.exp2(x).astype(BF16)


def _lane(shape):
    return lax.broadcasted_iota(jnp.int32, shape, 1)


def _row(shape):
    return lax.broadcasted_iota(jnp.int32, shape, 0)


def _rope128(x, cos, sin):
    partner = jnp.where((_lane(x.shape) & 31) < 16,
                        pltpu.roll(x, LANE - ROPE_PAIRS, axis=1),
                        pltpu.roll(x, ROPE_PAIRS, axis=1))
    return x * cos + partner * sin


def _cast_weights_once(w_refs, w_s):
    @pl.when(pl.program_id(0) == 0)
    def _():
        off = 0
        for r in w_refs:
            w_s[:, off:off + r.shape[1]] = r[...].astype(BF16)
            off += r.shape[1]


def _w_in_spec(layer, col, width):
    assert col % width == 0
    return pl.BlockSpec((None, D_MODEL, width), lambda i: (layer, 0, col // width),
                        pipeline_mode=pl.Buffered(1))


def _state_spec(layer, group, tail):
    zeros = (0,) * len(tail)
    if layer == 0:
        return pl.BlockSpec((group, DEPTH) + tail, lambda i: (i, 0) + zeros)
    return pl.BlockSpec((group, None) + tail, lambda i: (i, layer) + zeros)


def _store_state(ref, g, layer, value, *sub):
    if layer == 0:
        ref[(g, 0) + sub] = value
        for later in range(1, DEPTH):
            ref[(g, later) + sub] = jnp.zeros_like(value)
    else:
        ref[(g,) + sub] = value


def _log_gamma(decay):
    x = -decay
    return -(jnp.maximum(x, 0.0) + jnp.log1p(jnp.exp(-jnp.abs(x))))


def _ada_kernel(cond_ref, w_ref, b_ref, o_ref):
    s = _silu(cond_ref[...])
    o_ref[...] = _bdot(s, w_ref[...]) + b_ref[...]


def _ada_call(cond, w_ada, b_ada):
    rows = cond.shape[0]
    n_out = N_MOD * D_MODEL
    return pl.pallas_call(
        _ada_kernel,
        grid=(DEPTH, n_out // ADA_TILE),
        in_specs=[
            pl.BlockSpec((rows, D_MODEL), lambda l, j: (0, 0)),
            pl.BlockSpec((None, D_MODEL, ADA_TILE), lambda l, j: (l, 0, j)),
            pl.BlockSpec((None, 1, ADA_TILE), lambda l, j: (l, 0, j)),
        ],
        out_specs=pl.BlockSpec((None, rows, ADA_TILE), lambda l, j: (l, 0, j)),
        out_shape=jax.ShapeDtypeStruct((DEPTH, rows, n_out), F32),
        compiler_params=_cparams(("parallel", "parallel")),
        name="ada_mod",
    )(cond, w_ada, b_ada.reshape(DEPTH, 1, n_out))


def _swiglu_rows(x, m, mod_base, gain, wi_ref, wo_ref, act_s, rows):
    h = (_rms(x, gain) * (1.0 + m[mod_base + 1:mod_base + 2]) + m[mod_base:mod_base + 1]).astype(BF16)
    for c in range(D_FF // FF_TILE):
        a = jnp.dot(h, wi_ref[:, c * FF_TILE:(c + 1) * FF_TILE], preferred_element_type=F32)
        g = jnp.dot(h, wi_ref[:, D_FF + c * FF_TILE:D_FF + (c + 1) * FF_TILE], preferred_element_type=F32)
        act_s[rows, c * FF_TILE:(c + 1) * FF_TILE] = (_silu(a) * g).astype(BF16)
    out = jnp.dot(act_s[rows, :], wo_ref[...], preferred_element_type=F32)
    return x + 0.5 * m[mod_base + 2:mod_base + 3] * out


def _row_groups(n_rows):
    return [slice(p * ROW_GROUP, (p + 1) * ROW_GROUP) for p in range(n_rows // ROW_GROUP)]


def _ffn_a_kernel(x_ref, mod_ref, gain_ref, gain2_ref, wi_ref, wo_ref, y_ref, h2_ref, act_s):
    m = mod_ref[...]
    for rows in _row_groups(x_ref.shape[0]):
        y = _swiglu_rows(x_ref[rows, :], m, 0, gain_ref[...], wi_ref, wo_ref, act_s, rows)
        y_ref[rows, :] = y
        h2_ref[rows, :] = (_rms(y, gain2_ref[...]) * (1.0 + m[4:5]) + m[3:4]).astype(BF16)


def _resident(block_shape, index_map):
    return pl.BlockSpec(block_shape, index_map, pipeline_mode=pl.Buffered(1))


def _ffn_a_call(x, mod, mod_row, gain, gain2, w_in, w_out, layer, name):
    nt, tm, _ = x.shape
    tok = pl.BlockSpec((None, tm, D_MODEL), lambda i: (i, 0, 0))
    vec = pl.BlockSpec((1, D_MODEL), lambda i: (0, 0))
    return pl.pallas_call(
        _ffn_a_kernel,
        grid=(nt,),
        in_specs=[
            tok, pl.BlockSpec((None, N_MOD, D_MODEL), lambda i: (mod_row(i), 0, 0)), vec, vec,
            _resident((None, D_MODEL, 2 * D_FF), lambda i: (layer, 0, 0)),
            _resident((None, D_FF, D_MODEL), lambda i: (layer, 0, 0)),
        ],
        out_specs=[tok, tok],
        out_shape=[jax.ShapeDtypeStruct(x.shape, F32), jax.ShapeDtypeStruct(x.shape, BF16)],
        scratch_shapes=[pltpu.VMEM((tm, D_FF), BF16)],
        compiler_params=_cparams(("parallel",)),
        name=name,
    )(x, mod, gain, gain2, w_in, w_out)


def _hy_filter_kernel(feats_ref, w1_ref, b1_ref, w2_ref, b2_ref, w3_ref, win_ref,
                      ch_ref, cl_ref, sh_ref, sl_ref, kr_ref, ki_ref, kn_ref, *, seq):
    n_fft = 2 * seq
    z = jnp.sin(HY_SIN_W * (_dot3(feats_ref[...], w1_ref[...]) + b1_ref[...]))
    z = jnp.sin(HY_SIN_W * (_dot3(z, w2_ref[...]) + b2_ref[...]))
    zz = _dot3(z, w3_ref[...])
    win = win_ref[...]
    hf = zz[:, :HY_W] * win
    hb = zz[:, HY_W:] * win
    norm = (jnp.sum(jnp.abs(hf), axis=0, keepdims=True)
            + jnp.sum(jnp.abs(hb), axis=0, keepdims=True))
    hf = hf / norm
    hb = hb / norm
    row = _row(hf.shape)
    hb0 = jnp.where(row == 0, 0.0, hb)
    even = hf + hb0
    odd = hb0 - hf
    wk = jnp.where(row == 0, 1.0 / n_fft, 2.0 / n_fft)
    kr_ref[...] = _dot3_pre(ch_ref[...], cl_ref[...], even) * wk
    ki_ref[...] = _dot3_pre(sh_ref[...], sl_ref[...], odd) * wk
    sgn = jnp.where((row & 1) == 0, 1.0, -1.0)
    kn_ref[...] = jnp.sum(even * sgn, axis=0, keepdims=True) * (1.0 / n_fft)


def _hy_filter_call(seq, feats, w1, b1, w2, b2, w3, win, ch, cl, sh, sl):
    args = (feats, w1, b1, w2, b2, w3, win, ch, cl, sh, sl)
    return pl.pallas_call(
        functools.partial(_hy_filter_kernel, seq=seq),
        out_shape=[jax.ShapeDtypeStruct((seq, HY_W), F32),
                   jax.ShapeDtypeStruct((seq, HY_W), F32),
                   jax.ShapeDtypeStruct((1, HY_W), F32)],
        compiler_params=pltpu.CompilerParams(vmem_limit_bytes=VMEM_LIMIT),
        name=f"hyena_filter_{seq}",
    )(*args)


def _hyena_kernel(h_ref, w_ref, cw_ref, cb_ref, kr_ref, ki_ref, kn_ref, skip_ref,
                  c_ref, s_ref, y_ref, w_s, *, group, seq):
    _cast_weights_once((w_ref,), w_s)
    proj = jnp.dot(h_ref[...], w_s[...], preferred_element_type=F32)
    cw = cw_ref[...]
    cmat = c_ref[...]
    smat = s_ref[...]
    kr = kr_ref[...]
    ki = ki_ref[...]
    row = _row((seq, 3 * HY_W))
    row_w = _row((seq, HY_W))
    sgn = jnp.where((row_w & 1) == 0, 1.0, -1.0)
    for g in range(group):
        hy = proj[g * seq:(g + 1) * seq]
        prev = jnp.where(row == 0, 0.0, pltpu.roll(hy, 1, axis=0))
        nxt = jnp.where(row == seq - 1, 0.0, pltpu.roll(hy, seq - 1, axis=0))
        u = prev * cw[0:1] + hy * cw[1:2] + nxt * cw[2:3] + cb_ref[...]
        v, x0, x1 = u[:, :HY_W], u[:, HY_W:2 * HY_W], u[:, 2 * HY_W:]
        z = v * x1
        zb = z.astype(BF16)
        zr = jnp.dot(cmat, zb, preferred_element_type=F32)
        zs = jnp.dot(smat, zb, preferred_element_type=F32)
        yr = (zr * kr + zs * ki).astype(BF16)
        yi = (zr * ki - zs * kr).astype(BF16)
        nyq = jnp.sum(z * sgn, axis=0, keepdims=True) * kn_ref[...]
        conv = (jnp.dot(cmat, yr, preferred_element_type=F32)
                - jnp.dot(smat, yi, preferred_element_type=F32) + sgn * nyq)
        y_ref[g * seq:(g + 1) * seq, :] = (x0 * (conv + skip_ref[...] * z)).astype(BF16)


def _hyena_call(h, w_in, layer, cw, cb, kr, ki, kn, skip, cmat, smat, group, seq, name):
    nt, tm, _ = h.shape
    const2 = lambda a: pl.BlockSpec(a.shape, lambda i: (0, 0))
    return pl.pallas_call(
        functools.partial(_hyena_kernel, group=group, seq=seq),
        grid=(nt,),
        in_specs=[
            pl.BlockSpec((None, tm, D_MODEL), lambda i: (i, 0, 0)),
            _w_in_spec(layer, COL_HY, 3 * HY_W),
            const2(cw), const2(cb), const2(kr), const2(ki), const2(kn), const2(skip),
            const2(cmat), const2(smat),
        ],
        out_specs=pl.BlockSpec((None, tm, HY_W), lambda i: (i, 0, 0)),
        out_shape=jax.ShapeDtypeStruct((nt, tm, HY_W), BF16),
        scratch_shapes=[pltpu.VMEM((D_MODEL, 3 * HY_W), BF16)],
        compiler_params=_cparams(("arbitrary",)),
        name=name,
    )(h, w_in, cw, cb, kr, ki, kn, skip, cmat, smat)


def _diff_kernel(*refs, group, seq, past, lam_init, layer):
    has_cache = past > 0
    if has_cache:
        (h_ref, w_ref, gq_ref, gk_ref, lam_ref, gs_ref, cos_ref, sin_ref,
         ckt_ref, cv_ref, y_ref, w_s, k_s, v_s) = refs
    else:
        h_ref, w_ref, gq_ref, gk_ref, lam_ref, gs_ref = refs[:6]
        y_ref, ko_ref, vo_ref, w_s = refs[-4:]
    width = 2 * DIFF_HEADS * HEAD_DIM
    _cast_weights_once((w_ref,), w_s)
    proj = jnp.dot(h_ref[...], w_s[...], preferred_element_type=F32)
    q, k, v = proj[:, :width], proj[:, width:2 * width], proj[:, 2 * width:]
    q = _rms_heads(q, gq_ref[...])
    k = _rms_heads(k, gk_ref[...])
    dl = lam_ref[...]
    lam = (jnp.exp(jnp.sum(dl[0:1] * dl[1:2], axis=1, keepdims=True))
           - jnp.exp(jnp.sum(dl[2:3] * dl[3:4], axis=1, keepdims=True)) + lam_init)
    ones = jnp.ones((seq, LANE), BF16)
    if has_cache:
        cos, sin = cos_ref[...], sin_ref[...]
        q = jnp.concatenate([_rope128(q[:, s * LANE:(s + 1) * LANE], cos, sin)
                             for s in range(width // LANE)], axis=1)
        k = jnp.concatenate([_rope128(k[:, s * LANE:(s + 1) * LANE], cos, sin)
                             for s in range(width // LANE)], axis=1)
        k_s[...] = k.astype(BF16)
        for hh in range(DIFF_HEADS):
            v_s[0:seq, 2 * hh * LANE:(2 * hh + 1) * LANE] = v[:, hh * LANE:(hh + 1) * LANE].astype(BF16)
            v_s[seq:seq + past, 2 * hh * LANE:(2 * hh + 1) * LANE] = (
                cv_ref[pl.ds(hh, past, stride=DIFF_HEADS), :].astype(BF16))
            v_s[:, (2 * hh + 1) * LANE:(2 * hh + 2) * LANE] = jnp.ones((seq + past, LANE), BF16)
    else:
        for g in range(group):
            rows = slice(g * seq, (g + 1) * seq)
            _store_state(ko_ref, g, layer, k[rows])
            for hh in range(DIFF_HEADS):
                _store_state(vo_ref, g, layer, v[rows, hh * LANE:(hh + 1) * LANE],
                             pl.ds(hh, seq, stride=DIFF_HEADS))
    q = q * (HEAD_DIM ** -0.5 * LOG2_E)
    tq = min(seq, ATT_Q_TILE)
    lane = _lane((tq, LANE))
    for g in range(group):
        for hh in range(DIFF_HEADS):
            sl = slice(hh * LANE, (hh + 1) * LANE)
            if has_cache:
                keys = k_s[:, sl]
                vals = v_s[0:seq, 2 * hh * LANE:(2 * hh + 2) * LANE]
                vals_c = v_s[seq:seq + past, 2 * hh * LANE:(2 * hh + 2) * LANE]
                keys_t = ckt_ref[sl, :].astype(BF16)
            else:
                keys = k[g * seq:(g + 1) * seq, sl].astype(BF16)
                vals = jnp.concatenate([v[g * seq:(g + 1) * seq, sl].astype(BF16), ones], axis=1)
            for qi in range(seq // tq):
                r0 = g * seq + qi * tq
                q128 = q[r0:r0 + tq, sl]
                qs = jnp.concatenate([jnp.where(lane < HEAD_DIM, q128, 0.0),
                                      jnp.where(lane >= HEAD_DIM, q128, 0.0)], axis=0).astype(BF16)
                s = _bdot_nt(qs, keys)
                if has_cache:
                    sc = jnp.dot(qs, keys_t, preferred_element_type=F32)
                    m = jnp.maximum(jnp.max(s, axis=-1, keepdims=True), jnp.max(sc, axis=-1, keepdims=True))
                    oe = (jnp.dot(_exp2_bf16(s - m), vals, preferred_element_type=F32)
                          + jnp.dot(_exp2_bf16(sc - m), vals_c, preferred_element_type=F32))
                else:
                    e = _exp2_bf16(s - jnp.max(s, axis=-1, keepdims=True))
                    oe = jnp.dot(e, vals, preferred_element_type=F32)
                r = 1.0 / oe[:, LANE:]
                o = oe[:tq, :LANE] * r[:tq] - oe[tq:, :LANE] * (lam * r[tq:])
                y = _rms(o, gs_ref[:, sl]) * (1.0 - lam_init)
                y_ref[r0:r0 + tq, sl] = y.astype(BF16)


def _diff_call(h, w_in, layer, gq, gk, lam, gs, lam_init, group, seq,
               rope=None, cache=None, states=(), name="diff"):
    nt, tm = h.shape[0], h.shape[1]
    width = 2 * DIFF_HEADS * HEAD_DIM
    const2 = lambda a: pl.BlockSpec(a.shape, lambda i: (0, 0))
    in_specs = [
        pl.BlockSpec((None, tm, D_MODEL), lambda i: (i, 0, 0)),
        _w_in_spec(layer, COL_DIFF, 3 * width),
        const2(gq), const2(gk), const2(lam), const2(gs),
    ]
    args = [h, w_in, gq, gk, lam, gs]
    y_spec = pl.BlockSpec((None, tm, width), lambda i: (i, 0, 0))
    y_shape = jax.ShapeDtypeStruct((nt, tm, width), BF16)
    scratch = [pltpu.VMEM((D_MODEL, 3 * width), BF16)]
    aliases = {}
    if cache is None:
        past = 0
        k_spec = _state_spec(layer, group, (seq, width))
        k_shape = jax.ShapeDtypeStruct((nt * group, DEPTH, seq, width), F32)
        v_spec = _state_spec(layer, group, (seq * DIFF_HEADS, 2 * HEAD_DIM))
        v_shape = jax.ShapeDtypeStruct((nt * group, DEPTH, seq * DIFF_HEADS, 2 * HEAD_DIM), F32)
        out_specs = [y_spec, k_spec, v_spec]
        out_shape = [y_shape, k_shape, v_shape]
        aliases = {len(args) + n: 1 + n for n in range(len(states))}
        in_specs += [pl.BlockSpec(memory_space=pl.ANY)] * len(states)
        args += list(states)
    else:
        cos, sin = rope
        ckt, cv = cache
        past = ckt.shape[3]
        in_specs += [const2(cos), const2(sin),
                     pl.BlockSpec((None, None, width, past), lambda i: (i, layer, 0, 0)),
                     pl.BlockSpec((None, None, past * DIFF_HEADS, 2 * HEAD_DIM), lambda i: (i, layer, 0, 0))]
        args += [cos, sin, ckt, cv]
        out_specs = [y_spec]
        out_shape = [y_shape]
        scratch += [pltpu.VMEM((seq, width), BF16), pltpu.VMEM((seq + past, 2 * width), BF16)]
    return pl.pallas_call(
        functools.partial(_diff_kernel, group=group, seq=seq, past=past, lam_init=lam_init, layer=layer),
        grid=(nt,),
        in_specs=in_specs,
        out_specs=out_specs,
        out_shape=out_shape,
        scratch_shapes=scratch,
        input_output_aliases=aliases,
        compiler_params=_cparams(("arbitrary",)),
        name=name,
    )(*args)


def _win_heads(q, hk, lane):
    out = []
    for gq in range(WIN_GROUP):
        j = hk * WIN_GROUP + gq
        slab = q[:, (j // 2) * LANE:(j // 2 + 1) * LANE]
        if j % 2 != hk:
            slab = pltpu.roll(slab, HEAD_DIM, axis=1)
        out.append(jnp.where(lane >= HEAD_DIM if hk == 1 else lane < HEAD_DIM, slab, 0.0))
    return out


def _win_place(o_heads, hk, lane):
    slabs = []
    for pair in range(WIN_GROUP // 2):
        halves = []
        for gq in (2 * pair, 2 * pair + 1):
            j = hk * WIN_GROUP + gq
            o = o_heads[gq]
            if j % 2 != hk:
                o = pltpu.roll(o, HEAD_DIM, axis=1)
            halves.append(o)
        slabs.append(jnp.where(lane < HEAD_DIM, halves[0], halves[1]))
    return slabs


def _sink_col(sink_ref, layer, hk, rows_per_head):
    rows = WIN_GROUP * rows_per_head
    r = _row((rows, 1))
    col = jnp.full((rows, 1), sink_ref[layer, hk * WIN_GROUP], F32)
    for gq in range(1, WIN_GROUP):
        col = jnp.where(r >= gq * rows_per_head, sink_ref[layer, hk * WIN_GROUP + gq], col)
    return col


def _win_kernel(*refs, group, seq, past, layer):
    has_cache = past > 0
    if has_cache:
        (sink_ref, h_ref, w_ref, gq_ref, gk_ref, cos_ref, sin_ref, ck_ref, cv_ref,
         y_ref, w_s, k_s, v_s) = refs
    else:
        sink_ref, h_ref, w_ref, gq_ref, gk_ref = refs[:5]
        y_ref, ko_ref, vo_ref, w_s = refs[-4:]
    qw = WIN_Q_HEADS * HEAD_DIM
    kw = WIN_KV_HEADS * HEAD_DIM
    _cast_weights_once((w_ref,), w_s)
    proj = jnp.dot(h_ref[...], w_s[...], preferred_element_type=F32)
    q, k, v = proj[:, :qw], proj[:, qw:qw + kw], proj[:, qw + kw:]
    q = _rms_heads(q, gq_ref[...])
    k = _rms_heads(k, gk_ref[...])
    scale = HEAD_DIM ** -0.5 * LOG2_E
    if not has_cache:
        for g in range(group):
            _store_state(ko_ref, g, layer, k[g * seq:(g + 1) * seq])
            _store_state(vo_ref, g, layer, v[g * seq:(g + 1) * seq])
        q = q * scale
        lane = _lane((seq, LANE))
        ones = jnp.ones((seq, LANE), BF16)
        for g in range(group):
            rows = slice(g * seq, (g + 1) * seq)
            kb = k[rows].astype(BF16)
            vb = jnp.concatenate([v[rows].astype(BF16), ones], axis=1)
            qg = q[rows]
            for hk in range(WIN_KV_HEADS):
                qs = jnp.concatenate(_win_heads(qg, hk, lane), axis=0)
                s = _bdot_nt(qs, kb)
                sink = _sink_col(sink_ref, layer, hk, seq) * LOG2_E
                m = jnp.maximum(jnp.max(s, axis=-1, keepdims=True), sink)
                oe = jnp.dot(_exp2_bf16(s - m), vb, preferred_element_type=F32)
                o = oe[:, :LANE] * (1.0 / (oe[:, LANE:] + jnp.exp2(sink - m)))
                slabs = _win_place([o[gq * seq:(gq + 1) * seq] for gq in range(WIN_GROUP)], hk, lane)
                for pair, slab in enumerate(slabs):
                    c0 = (hk * 2 + pair) * LANE
                    y_ref[rows, c0:c0 + LANE] = slab.astype(BF16)
        return

    cos, sin = cos_ref[...], sin_ref[...]
    q = jnp.concatenate([_rope128(q[:, s * LANE:(s + 1) * LANE], cos, sin)
                         for s in range(qw // LANE)], axis=1) * scale
    k = _rope128(k, cos, sin)
    zpad = jnp.zeros((BLOCK, kw), BF16)
    k_s[0:BLOCK, :] = zpad
    k_s[BLOCK:BLOCK + seq, :] = k.astype(BF16)
    k_s[BLOCK + seq:2 * BLOCK + seq, :] = zpad
    v_s[0:BLOCK, 0:kw] = zpad
    v_s[BLOCK:BLOCK + seq, 0:kw] = v.astype(BF16)
    v_s[BLOCK + seq:2 * BLOCK + seq, 0:kw] = zpad
    v_s[:, kw:2 * kw] = jnp.ones((seq + 2 * BLOCK, kw), BF16)
    ckt = ck_ref[...].astype(BF16)
    cvt = jnp.concatenate([cv_ref[...].astype(BF16), jnp.ones((kw, past), BF16)], axis=0)
    lane_l = _lane((seq, LANE))
    lane_b = _lane((BLOCK, LANE))
    nb = seq // BLOCK
    rows_q = WIN_GROUP * BLOCK
    kk = _lane((rows_q, 3 * BLOCK))
    qi = _row((rows_q, 3 * BLOCK)) & (BLOCK - 1)
    rel = kk - BLOCK - qi
    bias_mid = jnp.where(rel <= WINDOW, jnp.where(rel >= -WINDOW, 0.0, NEG_INF), NEG_INF)
    bias = {(False, False): bias_mid,
            (True, False): jnp.where(kk >= BLOCK, bias_mid, NEG_INF),
            (False, True): jnp.where(kk < 2 * BLOCK, bias_mid, NEG_INF)}
    bias[(True, True)] = jnp.where(kk < 2 * BLOCK, bias[(True, False)], NEG_INF)
    for hk in range(WIN_KV_HEADS):
        heads = _win_heads(q, hk, lane_l)
        sink = _sink_col(sink_ref, layer, hk, BLOCK) * LOG2_E
        for n in range(nb):
            qs = jnp.concatenate([hd[n * BLOCK:(n + 1) * BLOCK] for hd in heads], axis=0)
            sb = _bdot_nt(qs, k_s[n * BLOCK:(n + 3) * BLOCK, :]) + bias[(n == 0, n == nb - 1)]
            sc = jnp.dot(qs.astype(BF16), ckt, preferred_element_type=F32)
            m = jnp.maximum(jnp.maximum(jnp.max(sb, axis=-1, keepdims=True),
                                        jnp.max(sc, axis=-1, keepdims=True)), sink)
            eb = _exp2_bf16(sb - m)
            ec = _exp2_bf16(sc - m)
            oe = (jnp.dot(eb, v_s[n * BLOCK:(n + 3) * BLOCK, :], preferred_element_type=F32)
                  + _bdot_nt(ec, cvt))
            o = oe[:, :LANE] * (1.0 / (oe[:, LANE:] + jnp.exp2(sink - m)))
            slabs = _win_place([o[gq * BLOCK:(gq + 1) * BLOCK] for gq in range(WIN_GROUP)], hk, lane_b)
            for pair, slab in enumerate(slabs):
                c0 = (hk * 2 + pair) * LANE
                y_ref[n * BLOCK:(n + 1) * BLOCK, c0:c0 + LANE] = slab.astype(BF16)


def _win_call(h, w_in, layer, sink, gq, gk, group, seq, rope=None, cache=None, states=(), name="win"):
    nt, tm = h.shape[0], h.shape[1]
    qw = WIN_Q_HEADS * HEAD_DIM
    kw = WIN_KV_HEADS * HEAD_DIM
    const2 = lambda a: pl.BlockSpec(a.shape, lambda i: (0, 0))
    in_specs = [
        pl.BlockSpec(memory_space=pltpu.SMEM),
        pl.BlockSpec((None, tm, D_MODEL), lambda i: (i, 0, 0)),
        _w_in_spec(layer, COL_WIN, qw + 2 * kw),
        const2(gq), const2(gk),
    ]
    args = [sink, h, w_in, gq, gk]
    y_spec = pl.BlockSpec((None, tm, qw), lambda i: (i, 0, 0))
    y_shape = jax.ShapeDtypeStruct((nt, tm, qw), BF16)
    scratch = [pltpu.VMEM((D_MODEL, qw + 2 * kw), BF16)]
    aliases = {}
    if cache is None:
        past = 0
        kv_spec = _state_spec(layer, group, (seq, kw))
        kv_shape = jax.ShapeDtypeStruct((nt * group, DEPTH, seq, kw), F32)
        out_specs = [y_spec, kv_spec, kv_spec]
        out_shape = [y_shape, kv_shape, kv_shape]
        aliases = {len(args) + n: 1 + n for n in range(len(states))}
        in_specs += [pl.BlockSpec(memory_space=pl.ANY)] * len(states)
        args += list(states)
    else:
        cos, sin = rope
        ck, cv = cache
        past = ck.shape[3]
        cspec = pl.BlockSpec((None, None, kw, past), lambda i: (i, layer, 0, 0))
        in_specs += [const2(cos), const2(sin), cspec, cspec]
        args += [cos, sin, ck, cv]
        out_specs = [y_spec]
        out_shape = [y_shape]
        scratch += [pltpu.VMEM((seq + 2 * BLOCK, kw), BF16), pltpu.VMEM((seq + 2 * BLOCK, 2 * kw), BF16)]
    return pl.pallas_call(
        functools.partial(_win_kernel, group=group, seq=seq, past=past, layer=layer),
        grid=(nt,),
        in_specs=in_specs,
        out_specs=out_specs,
        out_shape=out_shape,
        scratch_shapes=scratch,
        input_output_aliases=aliases,
        compiler_params=_cparams(("arbitrary",)),
        name=name,
    )(*args)


def _ret_kernel(*refs, group, seq, has_state, layer):
    if has_state:
        h_ref, wa_ref, wb_ref, df_ref, db_ref, s0f_ref, s0b_ref, y_ref, w_s, dec_s = refs
    else:
        h_ref, wa_ref, wb_ref, df_ref, db_ref = refs[:5]
        y_ref, sf_ref, sb_ref, w_s, dec_s = refs[-5:]
    qk_w = RET_HEADS * RET_DK
    v_w = RET_HEADS * RET_DV
    _cast_weights_once((wa_ref, wb_ref), w_s)
    proj = jnp.dot(h_ref[...], w_s[...], preferred_element_type=F32)
    q = proj[:, :qk_w]
    k = proj[:, qk_w:2 * qk_w] * (RET_DK ** -0.5)
    v = proj[:, 2 * qk_w:2 * qk_w + v_w]
    rg = proj[:, 2 * qk_w + v_w:]
    lgf_all = _log_gamma(df_ref[...])
    lgb_all = _log_gamma(db_ref[...])
    tq = min(seq, ATT_Q_TILE)
    n_q = seq // tq

    @pl.when(pl.program_id(0) == 0)
    def _():
        width = 2 * seq - tq
        rel = (_row((tq, width)) - _lane((tq, width)) + (seq - tq)).astype(F32)
        for hd in range(RET_HEADS):
            lgf = lgf_all[hd:hd + 1, 0:1]
            lgb = lgb_all[hd:hd + 1, 0:1]
            dec_s[hd] = (jnp.where(rel >= 0, jnp.exp(jnp.maximum(rel, 0.0) * lgf), 0.0)
                         + jnp.where(rel <= 0, jnp.exp(jnp.maximum(-rel, 0.0) * lgb), 0.0))

    lane = _lane((seq, LANE))
    t_full = _row((seq, LANE)).astype(F32)
    for hd in range(RET_HEADS):
        slab = slice((hd // 2) * LANE, (hd // 2 + 1) * LANE)
        vsl = slice(hd * RET_DV, (hd + 1) * RET_DV)
        half = hd % 2
        lgf = lgf_all[hd:hd + 1, 0:1]
        lgb = lgb_all[hd:hd + 1, 0:1]
        for g in range(group):
            rows = slice(g * seq, (g + 1) * seq)
            qm = jnp.where(lane >= RET_DK if half == 1 else lane < RET_DK, q[rows, slab], 0.0)
            k128 = k[rows, slab]
            kb = k128.astype(BF16)
            vb = v[rows, vsl].astype(BF16)
            if has_state:
                q_f = qm * jnp.exp((t_full + 1.0) * lgf)
                q_b = qm * jnp.exp((seq - t_full) * lgb)
                o_state = (_bdot(q_f, s0f_ref[hd // 2]) + _bdot(q_b, s0b_ref[hd // 2]))
            else:
                k_f = k128 * jnp.exp((seq - 1.0 - t_full) * lgf)
                k_b = k128 * jnp.exp(t_full * lgb)
                s_f = _bdot(k_f.T, vb)
                s_b = _bdot(k_b.T, vb)
                _store_state(sf_ref, g, layer, s_f[half * RET_DK:(half + 1) * RET_DK], hd)
                _store_state(sb_ref, g, layer, s_b[half * RET_DK:(half + 1) * RET_DK], hd)
            for qi in range(n_q):
                qrows = slice(qi * tq, (qi + 1) * tq)
                s = _bdot_nt(qm[qrows], kb)
                off = (n_q - 1 - qi) * tq
                dec = dec_s[hd, :, off:off + seq]
                o = jnp.dot((s * dec).astype(BF16), vb, preferred_element_type=F32)
                if has_state:
                    o = o + o_state[qrows]
                r0 = g * seq + qi * tq
                y = _rms(o, None) * _silu(rg[r0:r0 + tq, vsl])
                y_ref[r0:r0 + tq, vsl] = y.astype(BF16)


def _ret_call(h, w_in, layer, dec_f, dec_b, group, seq, state=None, states=(), name="ret"):
    nt, tm = h.shape[0], h.shape[1]
    v_w = RET_HEADS * RET_DV
    ret_cols = COL_GATE - COL_RET
    const2 = lambda a: pl.BlockSpec(a.shape, lambda i: (0, 0))
    in_specs = [
        pl.BlockSpec((None, tm, D_MODEL), lambda i: (i, 0, 0)),
        _w_in_spec(layer, COL_RET, ret_cols // 2),
        _w_in_spec(layer, COL_RET + ret_cols // 2, ret_cols // 2),
        const2(dec_f), const2(dec_b),
    ]
    args = [h, w_in, w_in, dec_f, dec_b]
    y_spec = pl.BlockSpec((None, tm, v_w), lambda i: (i, 0, 0))
    y_shape = jax.ShapeDtypeStruct((nt, tm, v_w), BF16)
    aliases = {}
    if state is None:
        s_spec = _state_spec(layer, group, (RET_HEADS, RET_DK, RET_DV))
        s_shape = jax.ShapeDtypeStruct((nt * group, DEPTH, RET_HEADS, RET_DK, RET_DV), F32)
        out_specs = [y_spec, s_spec, s_spec]
        out_shape = [y_shape, s_shape, s_shape]
        aliases = {len(args) + n: 1 + n for n in range(len(states))}
        in_specs += [pl.BlockSpec(memory_space=pl.ANY)] * len(states)
        args += list(states)
    else:
        s0f, s0b = state
        sspec = pl.BlockSpec((None, None, 2, LANE, RET_DV), lambda i: (i, layer, 0, 0, 0))
        in_specs += [sspec, sspec]
        args += [s0f, s0b]
        out_specs = [y_spec]
        out_shape = [y_shape]
    return pl.pallas_call(
        functools.partial(_ret_kernel, group=group, seq=seq, has_state=state is not None, layer=layer),
        grid=(nt,),
        in_specs=in_specs,
        out_specs=out_specs,
        out_shape=out_shape,
        input_output_aliases=aliases,
        scratch_shapes=[pltpu.VMEM((D_MODEL, ret_cols), BF16),
                        pltpu.VMEM((RET_HEADS, min(seq, ATT_Q_TILE), 2 * seq - min(seq, ATT_Q_TILE)), F32)],
        compiler_params=_cparams(("arbitrary",)),
        name=name,
    )(*args)


def _merge_ffn_kernel(x_ref, h_ref, mod_ref, gain_ref, y0_ref, y1_ref, y2_ref, y3_ref,
                      wg_ref, wb_ref, wo_ref, wi_ref, wo2_ref, o_ref, merged_s, act_s):
    m = mod_ref[...]
    for rows in _row_groups(x_ref.shape[0]):
        h = h_ref[rows, :]
        ys = [r[rows, :] for r in (y0_ref, y1_ref, y2_ref, y3_ref)]
        for c in range(D_MODEL // MERGE_TILE):
            cols = slice(c * MERGE_TILE, (c + 1) * MERGE_TILE)
            acc = None
            for b in range(N_BRANCH):
                gate = jax.nn.sigmoid(jnp.dot(
                    h, wg_ref[:, b * D_MODEL + c * MERGE_TILE:b * D_MODEL + (c + 1) * MERGE_TILE],
                    preferred_element_type=F32))
                part = gate * jnp.dot(ys[b], wb_ref[b, :, cols], preferred_element_type=F32)
                acc = part if acc is None else acc + part
            merged_s[rows, cols] = acc.astype(BF16)
        mixed = jnp.dot(merged_s[rows, :], wo_ref[...], preferred_element_type=F32)
        x = x_ref[rows, :] + m[5:6] * mixed
        o_ref[rows, :] = _swiglu_rows(x, m, 6, gain_ref[...], wi_ref, wo2_ref, act_s, rows)


def _merge_ffn_call(x, h, mod, mod_row, gain, ys, w_gate, w_branch, w_out, w_ff_in, w_ff_out, layer, name):
    nt, tm, _ = x.shape
    tok = pl.BlockSpec((None, tm, D_MODEL), lambda i: (i, 0, 0))
    ysp = pl.BlockSpec((None, tm, BRANCH_W), lambda i: (i, 0, 0))
    return pl.pallas_call(
        _merge_ffn_kernel,
        grid=(nt,),
        in_specs=[
            tok, tok,
            pl.BlockSpec((None, N_MOD, D_MODEL), lambda i: (mod_row(i), 0, 0)),
            pl.BlockSpec((1, D_MODEL), lambda i: (0, 0)),
            ysp, ysp, ysp, ysp,
            _resident((None, D_MODEL, N_BRANCH * D_MODEL), lambda i: (layer, 0, 0)),
            _resident((None, N_BRANCH, BRANCH_W, D_MODEL), lambda i: (layer, 0, 0, 0)),
            _resident((None, D_MODEL, D_MODEL), lambda i: (layer, 0, 0)),
            _resident((None, D_MODEL, 2 * D_FF), lambda i: (layer, 0, 0)),
            _resident((None, D_FF, D_MODEL), lambda i: (layer, 0, 0)),
        ],
        out_specs=tok,
        out_shape=jax.ShapeDtypeStruct(x.shape, F32),
        scratch_shapes=[pltpu.VMEM((tm, D_MODEL), BF16), pltpu.VMEM((tm, D_FF), BF16)],
        compiler_params=_cparams(("parallel",)),
        name=name,
    )(x, h, mod, gain, *ys, w_gate, w_branch, w_out, w_ff_in, w_ff_out)


def _dft_tables(seq):
    n_fft = 2 * seq
    idx = np.arange(seq, dtype=np.int64)
    ang = 2.0 * np.pi * ((idx[:, None] * idx[None, :]) % n_fft).astype(np.float64) / n_fft
    out = []
    for m in (np.cos(ang), np.sin(ang)):
        hi = jnp.asarray(m, F32).astype(BF16)
        lo = (jnp.asarray(m, F32) - hi.astype(F32)).astype(BF16)
        out += [hi, lo]
    return out


def _hyena_tables(seq):
    t = np.arange(seq, dtype=np.float32) / np.float32(seq)
    f = np.arange(1, HY_BANDS + 1, dtype=np.float32)
    ang = np.float32(2.0 * math.pi) * t[:, None] * f[None, :]
    feats = np.zeros((seq, LANE), np.float32)
    feats[:, 0] = t
    feats[:, 1:1 + HY_BANDS] = np.sin(ang)
    feats[:, 1 + HY_BANDS:HY_EMB] = np.cos(ang)
    min_decay = math.log(HY_TARGET) / HY_DECAY_LONG_PCT
    max_decay = math.log(HY_TARGET) / HY_DECAY_SHORT_PCT
    deltas = np.linspace(min_decay, max_decay, HY_W, dtype=np.float32)
    window = np.exp(-t[:, None] * np.abs(deltas)[None, :]).astype(np.float32)
    return jnp.asarray(feats), jnp.asarray(window)


def _rope_tables(seq):
    pos = np.arange(seq)
    row = (pos // GRID_W).astype(np.float32)
    col = (pos % GRID_W).astype(np.float32)
    inv_freq = (np.float32(ROPE_BASE) ** (-np.arange(ROPE_PAIRS, dtype=np.float32) / np.float32(ROPE_PAIRS)))
    lane = np.arange(LANE)
    in_head = lane % HEAD_DIM
    use_col = in_head >= HEAD_DIM // 2
    pair = in_head % ROPE_PAIRS
    second = (in_head % (2 * ROPE_PAIRS)) >= ROPE_PAIRS
    p = np.where(use_col[None, :], col[:, None], row[:, None]).astype(np.float32)
    ang = (p * inv_freq[pair][None, :]).astype(np.float32)
    cos = np.cos(ang).astype(np.float32)
    sin = np.sin(ang).astype(np.float32)
    sin = np.where(second[None, :], sin, -sin)
    return jnp.asarray(cos), jnp.asarray(sin)


def kernel(x_prompt, x_sample, c, cache_diff_k, cache_diff_v, cache_win_k, cache_win_v, state_ret_f, state_ret_b, c_ctx, norm_ffa, norm_mix, norm_ffb, w_ada, b_ada, w_ffa_in, w_ffa_out, w_ffb_in, w_ffb_out, w_in, hy_conv_w, hy_conv_b, hy_f_w1, hy_f_b1, hy_f_w2, hy_f_b2, hy_f_w3, hy_skip, diff_q_norm, diff_k_norm, diff_lambda, diff_subln, win_q_norm, win_k_norm, win_sink, ret_decay_f, ret_decay_b, w_branch, w_out):
    batch, seq, _ = x_prompt.shape
    dec_batch, dec_seq, _ = x_sample.shape
    past = cache_diff_k.shape[2]
    ctx_group = TOKEN_TILE // seq
    assert TOKEN_TILE % seq == 0 and batch % ctx_group == 0 and dec_seq == TOKEN_TILE
    assert 1 + dec_batch <= ADA_ROWS

    cond = jnp.zeros((ADA_ROWS, D_MODEL), F32).at[0].set(c_ctx).at[1:1 + dec_batch].set(c)
    mod = _ada_call(cond, w_ada, b_ada).reshape(DEPTH, ADA_ROWS, N_MOD, D_MODEL)

    bf = lambda a: a.astype(BF16)
    w_ffa_in_b, w_ffa_out_b, w_ffb_in_b, w_ffb_out_b = bf(w_ffa_in), bf(w_ffa_out), bf(w_ffb_in), bf(w_ffb_out)
    w_gate = bf(w_in[:, :, COL_GATE:])
    w_branch_b, w_out_b = bf(w_branch), bf(w_out)

    rope = _rope_tables(dec_seq)
    tables = {s: (_hyena_tables(s), _dft_tables(s)) for s in (seq, dec_seq)}

    ck_d = jnp.transpose(cache_diff_k, (0, 1, 3, 4, 5, 2)).reshape(dec_batch, DEPTH, -1, past)
    cv_d = cache_diff_v.reshape(dec_batch, DEPTH, past * DIFF_HEADS, 2 * HEAD_DIM)
    ck_w = jnp.transpose(cache_win_k, (0, 1, 3, 4, 2)).reshape(dec_batch, DEPTH, -1, past)
    cv_w = jnp.transpose(cache_win_v, (0, 1, 3, 4, 2)).reshape(dec_batch, DEPTH, -1, past)
    s0_f = state_ret_f.reshape(dec_batch, DEPTH, 2, LANE, RET_DV)
    s0_b = state_ret_b.reshape(dec_batch, DEPTH, 2, LANE, RET_DV)

    pad_rows = lambda a, n: jnp.pad(a, ((0, n - a.shape[0]), (0, 0)))
    tile_lanes = lambda a, n: jnp.tile(a.reshape(1, -1), (1, n))

    tiled = lambda a, tile: a.reshape(-1, tile, a.shape[-1])
    y_ctx = x_prompt
    y_lat = x_sample
    ctx_row = lambda tile: (lambda i: 0)
    lat_row = lambda tile: (lambda i: i // (dec_seq // tile) + 1)
    st_diff, st_win, st_ret = (), (), ()

    for l in range(DEPTH):
        lam_init = 0.8 - 0.6 * math.exp(-0.3 * l)
        gain = lambda a: a[l].reshape(1, -1)
        gq_d, gk_d = tile_lanes(diff_q_norm[l], 2 * DIFF_HEADS), tile_lanes(diff_k_norm[l], 2 * DIFF_HEADS)
        gs_d = tile_lanes(diff_subln[l], DIFF_HEADS)
        gq_w, gk_w = tile_lanes(win_q_norm[l], WIN_Q_HEADS), tile_lanes(win_k_norm[l], WIN_KV_HEADS)
        dec_f = jnp.broadcast_to(ret_decay_f[l][:, None], (RET_HEADS, LANE))
        dec_b = jnp.broadcast_to(ret_decay_b[l][:, None], (RET_HEADS, LANE))
        w1 = jnp.pad(hy_f_w1[l], ((0, LANE - HY_EMB), (0, LANE - HY_FH)))
        b1 = jnp.pad(hy_f_b1[l].reshape(1, -1), ((0, 0), (0, LANE - HY_FH)))
        w2 = jnp.pad(hy_f_w2[l], ((0, LANE - HY_FH), (0, LANE - HY_FH)))
        b2 = jnp.pad(hy_f_b2[l].reshape(1, -1), ((0, 0), (0, LANE - HY_FH)))
        w3 = pad_rows(hy_f_w3[l], LANE)
        cw, cb, skip = hy_conv_w[l], hy_conv_b[l].reshape(1, -1), hy_skip[l].reshape(1, -1)

        for is_lat in (False, True):
            x = y_lat if is_lat else y_ctx
            s_len = dec_seq if is_lat else seq
            group = 1 if is_lat else ctx_group
            mod_row = lat_row if is_lat else ctx_row
            tag = f"{'lat' if is_lat else 'ctx'}{l}"
            (feats, window), (c_hi, c_lo, s_hi, s_lo) = tables[s_len]

            x, h = _ffn_a_call(tiled(x, FFN_A_TILE), mod[l], mod_row(FFN_A_TILE), gain(norm_ffa), gain(norm_mix),
                               w_ffa_in_b, w_ffa_out_b, l, f"ffn_a_{tag}")
            h = tiled(h, TOKEN_TILE)

            kr, ki, kn = _hy_filter_call(s_len, feats, w1, b1, w2, b2, w3, window, c_hi, c_lo, s_hi, s_lo)
            y_hy = _hyena_call(h, w_in, l, cw, cb, kr, ki, kn, skip, c_hi, s_hi, group, s_len, f"hyena_{tag}")
            if is_lat:
                (y_diff,) = _diff_call(h, w_in, l, gq_d, gk_d, diff_lambda[l], gs_d, lam_init,
                                       group, s_len, rope=rope, cache=(ck_d, cv_d), name=f"diff_{tag}")
                (y_win,) = _win_call(h, w_in, l, win_sink, gq_w, gk_w, group, s_len,
                                     rope=rope, cache=(ck_w, cv_w), name=f"win_{tag}")
                (y_ret,) = _ret_call(h, w_in, l, dec_f, dec_b, group, s_len, state=(s0_f, s0_b),
                                     name=f"ret_{tag}")
            else:
                y_diff, *st_diff = _diff_call(h, w_in, l, gq_d, gk_d, diff_lambda[l], gs_d, lam_init,
                                              group, s_len, states=st_diff, name=f"diff_{tag}")
                y_win, *st_win = _win_call(h, w_in, l, win_sink, gq_w, gk_w, group, s_len,
                                           states=st_win, name=f"win_{tag}")
                y_ret, *st_ret = _ret_call(h, w_in, l, dec_f, dec_b, group, s_len, states=st_ret,
                                           name=f"ret_{tag}")

            x = _merge_ffn_call(tiled(x, MERGE_FFN_TILE), tiled(h, MERGE_FFN_TILE), mod[l], mod_row(MERGE_FFN_TILE),
                                gain(norm_ffb),
                                tuple(tiled(y, MERGE_FFN_TILE) for y in (y_hy, y_diff, y_win, y_ret)),
                                w_gate, w_branch_b, w_out_b, w_ffb_in_b, w_ffb_out_b, l, f"merge_ffn_b_{tag}")
            if is_lat:
                y_lat = x
            else:
                y_ctx = x

    return (y_ctx.reshape(batch, seq, D_MODEL), y_lat.reshape(dec_batch, dec_seq, D_MODEL),
            st_diff[0].reshape(batch, DEPTH, seq, DIFF_HEADS, 2, HEAD_DIM),
            st_diff[1].reshape(batch, DEPTH, seq, DIFF_HEADS, 2 * HEAD_DIM),
            st_win[0].reshape(batch, DEPTH, seq, WIN_KV_HEADS, HEAD_DIM),
            st_win[1].reshape(batch, DEPTH, seq, WIN_KV_HEADS, HEAD_DIM),
            st_ret[0], st_ret[1])
```

```python
import functools
import math

import jax
import jax.numpy as jnp
import numpy as np
from jax import lax
from jax.experimental import pallas as pl
from jax.experimental.pallas import tpu as pltpu

F32 = jnp.float32
BF16 = jnp.bfloat16

D_MODEL = 1024
DEPTH = 2
GRID_W = 64
HEAD_DIM = 64
ROPE_PAIRS = HEAD_DIM // 4
ROPE_BASE = 10000.0
EPS = 1e-6
NEG_INF = -1e30
LOG2_E = math.log2(math.e)
D_FF = 2816
N_MOD = 9
N_BRANCH = 4
BRANCH_W = 512

HY_W = BRANCH_W
HY_BANDS = 16
HY_EMB = 1 + 2 * HY_BANDS
HY_FH = 64
HY_SIN_W = 1.0
HY_TARGET = 1e-2
HY_DECAY_SHORT_PCT = 0.3
HY_DECAY_LONG_PCT = 1.5

DIFF_HEADS = 4
WIN_Q_HEADS = 8
WIN_KV_HEADS = 2
WIN_GROUP = WIN_Q_HEADS // WIN_KV_HEADS
WINDOW = 128
BLOCK = 128
RET_HEADS = 4
RET_DK = 64
RET_DV = 128

COL_HY = 0
COL_DIFF = 3 * HY_W
COL_WIN = COL_DIFF + 3 * (2 * DIFF_HEADS * HEAD_DIM)
COL_RET = COL_WIN + (WIN_Q_HEADS + 2 * WIN_KV_HEADS) * HEAD_DIM
COL_GATE = COL_RET + 2 * RET_HEADS * (RET_DK + RET_DV)
IN_COLS = COL_GATE + N_BRANCH * D_MODEL

LANE = 128
TOKEN_TILE = 1024
FFN_A_TILE = 1024
MERGE_FFN_TILE = 512
FF_TILE = 256
ROW_GROUP = 256
MERGE_TILE = 256
ADA_TILE = 2304
ADA_ROWS = 16
ATT_Q_TILE = 256
VMEM_LIMIT = 56 * 1024 * 1024


def _cparams(sem):
    return pltpu.CompilerParams(dimension_semantics=sem, vmem_limit_bytes=VMEM_LIMIT)


def _bdot(a, b):
    return jnp.dot(a.astype(BF16), b.astype(BF16), preferred_element_type=F32)


def _bdot_nt(a, b):
    return lax.dot_general(a.astype(BF16), b.astype(BF16), (((1,), (1,)), ((), ())),
                           preferred_element_type=F32)


def _split(a):
    hi = a.astype(BF16)
    lo = (a - hi.astype(F32)).astype(BF16)
    return hi, lo


def _dot3_pre(ah, al, b):
    bh, bl = _split(b)
    d = functools.partial(jnp.dot, preferred_element_type=F32)
    return d(ah, bh) + d(ah, bl) + d(al, bh)


def _dot3(a, b):
    return _dot3_pre(*_split(a), b)


def _rms_heads(x, gain):
    first = _lane((x.shape[0], LANE)) < HEAD_DIM
    out = []
    for s in range(x.shape[1] // LANE):
        xs = x[:, s * LANE:(s + 1) * LANE]
        sq = xs * xs
        lo = jnp.sum(jnp.where(first, sq, 0.0), axis=-1, keepdims=True)
        hi = jnp.sum(jnp.where(first, 0.0, sq), axis=-1, keepdims=True)
        r_lo = lax.rsqrt(lo * (1.0 / HEAD_DIM) + EPS)
        r_hi = lax.rsqrt(hi * (1.0 / HEAD_DIM) + EPS)
        out.append(xs * jnp.where(first, r_lo, r_hi))
    return jnp.concatenate(out, axis=1) * gain


def _rms(x, gain):
    y = x * lax.rsqrt(jnp.mean(x * x, axis=-1, keepdims=True) + EPS)
    return y if gain is None else y * gain


def _silu(x):
    return x * jax.nn.sigmoid(x)


def _exp2_bf16(x):
    return jnp.exp2(x).astype(BF16)


def _lane(shape):
    return lax.broadcasted_iota(jnp.int32, shape, 1)


def _row(shape):
    return lax.broadcasted_iota(jnp.int32, shape, 0)


def _rope128(x, cos, sin):
    partner = jnp.where((_lane(x.shape) & 31) < 16,
                        pltpu.roll(x, LANE - ROPE_PAIRS, axis=1),
                        pltpu.roll(x, ROPE_PAIRS, axis=1))
    return x * cos + partner * sin


def _cast_weights_once(w_refs, w_s):
    @pl.when(pl.program_id(0) == 0)
    def _():
        off = 0
        for r in w_refs:
            w_s[:, off:off + r.shape[1]] = r[...].astype(BF16)
            off += r.shape[1]


def _w_in_spec(layer, col, width):
    assert col % width == 0
    return pl.BlockSpec((None, D_MODEL, width), lambda i: (layer, 0, col // width),
                        pipeline_mode=pl.Buffered(1))


def _state_spec(layer, group, tail):
    zeros = (0,) * len(tail)
    if layer == 0:
        return pl.BlockSpec((group, DEPTH) + tail, lambda i: (i, 0) + zeros)
    return pl.BlockSpec((group, None) + tail, lambda i: (i, layer) + zeros)


def _store_state(ref, g, layer, value, *sub):
    if layer == 0:
        ref[(g, 0) + sub] = value
        for later in range(1, DEPTH):
            ref[(g, later) + sub] = jnp.zeros_like(value)
    else:
        ref[(g,) + sub] = value


def _log_gamma(decay):
    x = -decay
    return -(jnp.maximum(x, 0.0) + jnp.log1p(jnp.exp(-jnp.abs(x))))


def _ada_kernel(cond_ref, w_ref, b_ref, o_ref):
    s = _silu(cond_ref[...])
    o_ref[...] = _bdot(s, w_ref[...]) + b_ref[...]


def _ada_call(cond, w_ada, b_ada):
    rows = cond.shape[0]
    n_out = N_MOD * D_MODEL
    return pl.pallas_call(
        _ada_kernel,
        grid=(DEPTH, n_out // ADA_TILE),
        in_specs=[
            pl.BlockSpec((rows, D_MODEL), lambda l, j: (0, 0)),
            pl.BlockSpec((None, D_MODEL, ADA_TILE), lambda l, j: (l, 0, j)),
            pl.BlockSpec((None, 1, ADA_TILE), lambda l, j: (l, 0, j)),
        ],
        out_specs=pl.BlockSpec((None, rows, ADA_TILE), lambda l, j: (l, 0, j)),
        out_shape=jax.ShapeDtypeStruct((DEPTH, rows, n_out), F32),
        compiler_params=_cparams(("parallel", "parallel")),
        name="ada_mod",
    )(cond, w_ada, b_ada.reshape(DEPTH, 1, n_out))


def _swiglu_rows(x, m, mod_base, gain, wi_ref, wo_ref, act_s, rows):
    h = (_rms(x, gain) * (1.0 + m[mod_base + 1:mod_base + 2]) + m[mod_base:mod_base + 1]).astype(BF16)
    for c in range(D_FF // FF_TILE):
        a = jnp.dot(h, wi_ref[:, c * FF_TILE:(c + 1) * FF_TILE], preferred_element_type=F32)
        g = jnp.dot(h, wi_ref[:, D_FF + c * FF_TILE:D_FF + (c + 1) * FF_TILE], preferred_element_type=F32)
        act_s[rows, c * FF_TILE:(c + 1) * FF_TILE] = (_silu(a) * g).astype(BF16)
    out = jnp.dot(act_s[rows, :], wo_ref[...], preferred_element_type=F32)
    return x + 0.5 * m[mod_base + 2:mod_base + 3] * out


def _row_groups(n_rows):
    return [slice(p * ROW_GROUP, (p + 1) * ROW_GROUP) for p in range(n_rows // ROW_GROUP)]


def _ffn_a_kernel(x_ref, mod_ref, gain_ref, gain2_ref, wi_ref, wo_ref, y_ref, h2_ref, act_s):
    m = mod_ref[...]
    for rows in _row_groups(x_ref.shape[0]):
        y = _swiglu_rows(x_ref[rows, :], m, 0, gain_ref[...], wi_ref, wo_ref, act_s, rows)
        y_ref[rows, :] = y
        h2_ref[rows, :] = (_rms(y, gain2_ref[...]) * (1.0 + m[4:5]) + m[3:4]).astype(BF16)


def _resident(block_shape, index_map):
    return pl.BlockSpec(block_shape, index_map, pipeline_mode=pl.Buffered(1))


def _ffn_a_call(x, mod, mod_row, gain, gain2, w_in, w_out, layer, name):
    nt, tm, _ = x.shape
    tok = pl.BlockSpec((None, tm, D_MODEL), lambda i: (i, 0, 0))
    vec = pl.BlockSpec((1, D_MODEL), lambda i: (0, 0))
    return pl.pallas_call(
        _ffn_a_kernel,
        grid=(nt,),
        in_specs=[
            tok, pl.BlockSpec((None, N_MOD, D_MODEL), lambda i: (mod_row(i), 0, 0)), vec, vec,
            _resident((None, D_MODEL, 2 * D_FF), lambda i: (layer, 0, 0)),
            _resident((None, D_FF, D_MODEL), lambda i: (layer, 0, 0)),
        ],
        out_specs=[tok, tok],
        out_shape=[jax.ShapeDtypeStruct(x.shape, F32), jax.ShapeDtypeStruct(x.shape, BF16)],
        scratch_shapes=[pltpu.VMEM((tm, D_FF), BF16)],
        compiler_params=_cparams(("parallel",)),
        name=name,
    )(x, mod, gain, gain2, w_in, w_out)


def _hy_filter_kernel(feats_ref, w1_ref, b1_ref, w2_ref, b2_ref, w3_ref, win_ref,
                      ch_ref, cl_ref, sh_ref, sl_ref, kr_ref, ki_ref, kn_ref, *, seq):
    n_fft = 2 * seq
    z = jnp.sin(HY_SIN_W * (_dot3(feats_ref[...], w1_ref[...]) + b1_ref[...]))
    z = jnp.sin(HY_SIN_W * (_dot3(z, w2_ref[...]) + b2_ref[...]))
    zz = _dot3(z, w3_ref[...])
    win = win_ref[...]
    hf = zz[:, :HY_W] * win
    hb = zz[:, HY_W:] * win
    norm = (jnp.sum(jnp.abs(hf), axis=0, keepdims=True)
            + jnp.sum(jnp.abs(hb), axis=0, keepdims=True))
    hf = hf / norm
    hb = hb / norm
    row = _row(hf.shape)
    hb0 = jnp.where(row == 0, 0.0, hb)
    even = hf + hb0
    odd = hb0 - hf
    wk = jnp.where(row == 0, 1.0 / n_fft, 2.0 / n_fft)
    kr_ref[...] = _dot3_pre(ch_ref[...], cl_ref[...], even) * wk
    ki_ref[...] = _dot3_pre(sh_ref[...], sl_ref[...], odd) * wk
    sgn = jnp.where((row & 1) == 0, 1.0, -1.0)
    kn_ref[...] = jnp.sum(even * sgn, axis=0, keepdims=True) * (1.0 / n_fft)


def _hy_filter_call(seq, feats, w1, b1, w2, b2, w3, win, ch, cl, sh, sl):
    args = (feats, w1, b1, w2, b2, w3, win, ch, cl, sh, sl)
    return pl.pallas_call(
        functools.partial(_hy_filter_kernel, seq=seq),
        out_shape=[jax.ShapeDtypeStruct((seq, HY_W), F32),
                   jax.ShapeDtypeStruct((seq, HY_W), F32),
                   jax.ShapeDtypeStruct((1, HY_W), F32)],
        compiler_params=pltpu.CompilerParams(vmem_limit_bytes=VMEM_LIMIT),
        name=f"hyena_filter_{seq}",
    )(*args)


def _hyena_kernel(h_ref, w_ref, cw_ref, cb_ref, kr_ref, ki_ref, kn_ref, skip_ref,
                  c_ref, s_ref, y_ref, w_s, *, group, seq):
    _cast_weights_once((w_ref,), w_s)
    proj = jnp.dot(h_ref[...], w_s[...], preferred_element_type=F32)
    cw = cw_ref[...]
    cmat = c_ref[...]
    smat = s_ref[...]
    kr = kr_ref[...]
    ki = ki_ref[...]
    row = _row((seq, 3 * HY_W))
    row_w = _row((seq, HY_W))
    sgn = jnp.where((row_w & 1) == 0, 1.0, -1.0)
    for g in range(group):
        hy = proj[g * seq:(g + 1) * seq]
        prev = jnp.where(row == 0, 0.0, pltpu.roll(hy, 1, axis=0))
        nxt = jnp.where(row == seq - 1, 0.0, pltpu.roll(hy, seq - 1, axis=0))
        u = prev * cw[0:1] + hy * cw[1:2] + nxt * cw[2:3] + cb_ref[...]
        v, x0, x1 = u[:, :HY_W], u[:, HY_W:2 * HY_W], u[:, 2 * HY_W:]
        z = v * x1
        zb = z.astype(BF16)
        zr = jnp.dot(cmat, zb, preferred_element_type=F32)
        zs = jnp.dot(smat, zb, preferred_element_type=F32)
        yr = (zr * kr + zs * ki).astype(BF16)
        yi = (zr * ki - zs * kr).astype(BF16)
        nyq = jnp.sum(z * sgn, axis=0, keepdims=True) * kn_ref[...]
        conv = (jnp.dot(cmat, yr, preferred_element_type=F32)
                - jnp.dot(smat, yi, preferred_element_type=F32) + sgn * nyq)
        y_ref[g * seq:(g + 1) * seq, :] = (x0 * (conv + skip_ref[...] * z)).astype(BF16)


def _hyena_call(h, w_in, layer, cw, cb, kr, ki, kn, skip, cmat, smat, group, seq, name):
    nt, tm, _ = h.shape
    const2 = lambda a: pl.BlockSpec(a.shape, lambda i: (0, 0))
    return pl.pallas_call(
        functools.partial(_hyena_kernel, group=group, seq=seq),
        grid=(nt,),
        in_specs=[
            pl.BlockSpec((None, tm, D_MODEL), lambda i: (i, 0, 0)),
            _w_in_spec(layer, COL_HY, 3 * HY_W),
            const2(cw), const2(cb), const2(kr), const2(ki), const2(kn), const2(skip),
            const2(cmat), const2(smat),
        ],
        out_specs=pl.BlockSpec((None, tm, HY_W), lambda i: (i, 0, 0)),
        out_shape=jax.ShapeDtypeStruct((nt, tm, HY_W), BF16),
        scratch_shapes=[pltpu.VMEM((D_MODEL, 3 * HY_W), BF16)],
        compiler_params=_cparams(("arbitrary",)),
        name=name,
    )(h, w_in, cw, cb, kr, ki, kn, skip, cmat, smat)


def _diff_kernel(*refs, group, seq, past, lam_init, layer):
    has_cache = past > 0
    if has_cache:
        (h_ref, w_ref, gq_ref, gk_ref, lam_ref, gs_ref, cos_ref, sin_ref,
         ckt_ref, cv_ref, y_ref, w_s, k_s, v_s) = refs
    else:
        h_ref, w_ref, gq_ref, gk_ref, lam_ref, gs_ref = refs[:6]
        y_ref, ko_ref, vo_ref, w_s = refs[-4:]
    width = 2 * DIFF_HEADS * HEAD_DIM
    _cast_weights_once((w_ref,), w_s)
    proj = jnp.dot(h_ref[...], w_s[...], preferred_element_type=F32)
    q, k, v = proj[:, :width], proj[:, width:2 * width], proj[:, 2 * width:]
    q = _rms_heads(q, gq_ref[...])
    k = _rms_heads(k, gk_ref[...])
    dl = lam_ref[...]
    lam = (jnp.exp(jnp.sum(dl[0:1] * dl[1:2], axis=1, keepdims=True))
           - jnp.exp(jnp.sum(dl[2:3] * dl[3:4], axis=1, keepdims=True)) + lam_init)
    ones = jnp.ones((seq, LANE), BF16)
    if has_cache:
        cos, sin = cos_ref[...], sin_ref[...]
        q = jnp.concatenate([_rope128(q[:, s * LANE:(s + 1) * LANE], cos, sin)
                             for s in range(width // LANE)], axis=1)
        k = jnp.concatenate([_rope128(k[:, s * LANE:(s + 1) * LANE], cos, sin)
                             for s in range(width // LANE)], axis=1)
        k_s[...] = k.astype(BF16)
        for hh in range(DIFF_HEADS):
            v_s[0:seq, 2 * hh * LANE:(2 * hh + 1) * LANE] = v[:, hh * LANE:(hh + 1) * LANE].astype(BF16)
            v_s[seq:seq + past, 2 * hh * LANE:(2 * hh + 1) * LANE] = (
                cv_ref[pl.ds(hh, past, stride=DIFF_HEADS), :].astype(BF16))
            v_s[:, (2 * hh + 1) * LANE:(2 * hh + 2) * LANE] = jnp.ones((seq + past, LANE), BF16)
    else:
        for g in range(group):
            rows = slice(g * seq, (g + 1) * seq)
            _store_state(ko_ref, g, layer, k[rows])
            for hh in range(DIFF_HEADS):
                _store_state(vo_ref, g, layer, v[rows, hh * LANE:(hh + 1) * LANE],
                             pl.ds(hh, seq, stride=DIFF_HEADS))
    q = q * (HEAD_DIM ** -0.5 * LOG2_E)
    tq = min(seq, ATT_Q_TILE)
    lane = _lane((tq, LANE))
    for g in range(group):
        for hh in range(DIFF_HEADS):
            sl = slice(hh * LANE, (hh + 1) * LANE)
            if has_cache:
                keys = k_s[:, sl]
                vals = v_s[0:seq, 2 * hh * LANE:(2 * hh + 2) * LANE]
                vals_c = v_s[seq:seq + past, 2 * hh * LANE:(2 * hh + 2) * LANE]
                keys_t = ckt_ref[sl, :].astype(BF16)
            else:
                keys = k[g * seq:(g + 1) * seq, sl].astype(BF16)
                vals = jnp.concatenate([v[g * seq:(g + 1) * seq, sl].astype(BF16), ones], axis=1)
            for qi in range(seq // tq):
                r0 = g * seq + qi * tq
                q128 = q[r0:r0 + tq, sl]
                qs = jnp.concatenate([jnp.where(lane < HEAD_DIM, q128, 0.0),
                                      jnp.where(lane >= HEAD_DIM, q128, 0.0)], axis=0).astype(BF16)
                s = _bdot_nt(qs, keys)
                if has_cache:
                    sc = jnp.dot(qs, keys_t, preferred_element_type=F32)
                    m = jnp.maximum(jnp.max(s, axis=-1, keepdims=True), jnp.max(sc, axis=-1, keepdims=True))
                    oe = (jnp.dot(_exp2_bf16(s - m), vals, preferred_element_type=F32)
                          + jnp.dot(_exp2_bf16(sc - m), vals_c, preferred_element_type=F32))
                else:
                    e = _exp2_bf16(s - jnp.max(s, axis=-1, keepdims=True))
                    oe = jnp.dot(e, vals, preferred_element_type=F32)
                r = 1.0 / oe[:, LANE:]
                o = oe[:tq, :LANE] * r[:tq] - oe[tq:, :LANE] * (lam * r[tq:])
                y = _rms(o, gs_ref[:, sl]) * (1.0 - lam_init)
                y_ref[r0:r0 + tq, sl] = y.astype(BF16)


def _diff_call(h, w_in, layer, gq, gk, lam, gs, lam_init, group, seq,
               rope=None, cache=None, states=(), name="diff"):
    nt, tm = h.shape[0], h.shape[1]
    width = 2 * DIFF_HEADS * HEAD_DIM
    const2 = lambda a: pl.BlockSpec(a.shape, lambda i: (0, 0))
    in_specs = [
        pl.BlockSpec((None, tm, D_MODEL), lambda i: (i, 0, 0)),
        _w_in_spec(layer, COL_DIFF, 3 * width),
        const2(gq), const2(gk), const2(lam), const2(gs),
    ]
    args = [h, w_in, gq, gk, lam, gs]
    y_spec = pl.BlockSpec((None, tm, width), lambda i: (i, 0, 0))
    y_shape = jax.ShapeDtypeStruct((nt, tm, width), BF16)
    scratch = [pltpu.VMEM((D_MODEL, 3 * width), BF16)]
    aliases = {}
    if cache is None:
        past = 0
        k_spec = _state_spec(layer, group, (seq, width))
        k_shape = jax.ShapeDtypeStruct((nt * group, DEPTH, seq, width), F32)
        v_spec = _state_spec(layer, group, (seq * DIFF_HEADS, 2 * HEAD_DIM))
        v_shape = jax.ShapeDtypeStruct((nt * group, DEPTH, seq * DIFF_HEADS, 2 * HEAD_DIM), F32)
        out_specs = [y_spec, k_spec, v_spec]
        out_shape = [y_shape, k_shape, v_shape]
        aliases = {len(args) + n: 1 + n for n in range(len(states))}
        in_specs += [pl.BlockSpec(memory_space=pl.ANY)] * len(states)
        args += list(states)
    else:
        cos, sin = rope
        ckt, cv = cache
        past = ckt.shape[3]
        in_specs += [const2(cos), const2(sin),
                     pl.BlockSpec((None, None, width, past), lambda i: (i, layer, 0, 0)),
                     pl.BlockSpec((None, None, past * DIFF_HEADS, 2 * HEAD_DIM), lambda i: (i, layer, 0, 0))]
        args += [cos, sin, ckt, cv]
        out_specs = [y_spec]
        out_shape = [y_shape]
        scratch += [pltpu.VMEM((seq, width), BF16), pltpu.VMEM((seq + past, 2 * width), BF16)]
    return pl.pallas_call(
        functools.partial(_diff_kernel, group=group, seq=seq, past=past, lam_init=lam_init, layer=layer),
        grid=(nt,),
        in_specs=in_specs,
        out_specs=out_specs,
        out_shape=out_shape,
        scratch_shapes=scratch,
        input_output_aliases=aliases,
        compiler_params=_cparams(("arbitrary",)),
        name=name,
    )(*args)


def _win_heads(q, hk, lane):
    out = []
    for gq in range(WIN_GROUP):
        j = hk * WIN_GROUP + gq
        slab = q[:, (j // 2) * LANE:(j // 2 + 1) * LANE]
        if j % 2 != hk:
            slab = pltpu.roll(slab, HEAD_DIM, axis=1)
        out.append(jnp.where(lane >= HEAD_DIM if hk == 1 else lane < HEAD_DIM, slab, 0.0))
    return out


def _win_place(o_heads, hk, lane):
    slabs = []
    for pair in range(WIN_GROUP // 2):
        halves = []
        for gq in (2 * pair, 2 * pair + 1):
            j = hk * WIN_GROUP + gq
            o = o_heads[gq]
            if j % 2 != hk:
                o = pltpu.roll(o, HEAD_DIM, axis=1)
            halves.append(o)
        slabs.append(jnp.where(lane < HEAD_DIM, halves[0], halves[1]))
    return slabs


def _sink_col(sink_ref, layer, hk, rows_per_head):
    rows = WIN_GROUP * rows_per_head
    r = _row((rows, 1))
    col = jnp.full((rows, 1), sink_ref[layer, hk * WIN_GROUP], F32)
    for gq in range(1, WIN_GROUP):
        col = jnp.where(r >= gq * rows_per_head, sink_ref[layer, hk * WIN_GROUP + gq], col)
    return col


def _win_kernel(*refs, group, seq, past, layer):
    has_cache = past > 0
    if has_cache:
        (sink_ref, h_ref, w_ref, gq_ref, gk_ref, cos_ref, sin_ref, ck_ref, cv_ref,
         y_ref, w_s, k_s, v_s) = refs
    else:
        sink_ref, h_ref, w_ref, gq_ref, gk_ref = refs[:5]
        y_ref, ko_ref, vo_ref, w_s = refs[-4:]
    qw = WIN_Q_HEADS * HEAD_DIM
    kw = WIN_KV_HEADS * HEAD_DIM
    _cast_weights_once((w_ref,), w_s)
    proj = jnp.dot(h_ref[...], w_s[...], preferred_element_type=F32)
    q, k, v = proj[:, :qw], proj[:, qw:qw + kw], proj[:, qw + kw:]
    q = _rms_heads(q, gq_ref[...])
    k = _rms_heads(k, gk_ref[...])
    scale = HEAD_DIM ** -0.5 * LOG2_E
    if not has_cache:
        for g in range(group):
            _store_state(ko_ref, g, layer, k[g * seq:(g + 1) * seq])
            _store_state(vo_ref, g, layer, v[g * seq:(g + 1) * seq])
        q = q * scale
        lane = _lane((seq, LANE))
        ones = jnp.ones((seq, LANE), BF16)
        for g in range(group):
            rows = slice(g * seq, (g + 1) * seq)
            kb = k[rows].astype(BF16)
            vb = jnp.concatenate([v[rows].astype(BF16), ones], axis=1)
            qg = q[rows]
            for hk in range(WIN_KV_HEADS):
                qs = jnp.concatenate(_win_heads(qg, hk, lane), axis=0)
                s = _bdot_nt(qs, kb)
                sink = _sink_col(sink_ref, layer, hk, seq) * LOG2_E
                m = jnp.maximum(jnp.max(s, axis=-1, keepdims=True), sink)
                oe = jnp.dot(_exp2_bf16(s - m), vb, preferred_element_type=F32)
                o = oe[:, :LANE] * (1.0 / (oe[:, LANE:] + jnp.exp2(sink - m)))
                slabs = _win_place([o[gq * seq:(gq + 1) * seq] for gq in range(WIN_GROUP)], hk, lane)
                for pair, slab in enumerate(slabs):
                    c0 = (hk * 2 + pair) * LANE
                    y_ref[rows, c0:c0 + LANE] = slab.astype(BF16)
        return

    cos, sin = cos_ref[...], sin_ref[...]
    q = jnp.concatenate([_rope128(q[:, s * LANE:(s + 1) * LANE], cos, sin)
                         for s in range(qw // LANE)], axis=1) * scale
    k = _rope128(k, cos, sin)
    zpad = jnp.zeros((BLOCK, kw), BF16)
    k_s[0:BLOCK, :] = zpad
    k_s[BLOCK:BLOCK + seq, :] = k.astype(BF16)
    k_s[BLOCK + seq:2 * BLOCK + seq, :] = zpad
    v_s[0:BLOCK, 0:kw] = zpad
    v_s[BLOCK:BLOCK + seq, 0:kw] = v.astype(BF16)
    v_s[BLOCK + seq:2 * BLOCK + seq, 0:kw] = zpad
    v_s[:, kw:2 * kw] = jnp.ones((seq + 2 * BLOCK, kw), BF16)
    ckt = ck_ref[...].astype(BF16)
    cvt = jnp.concatenate([cv_ref[...].astype(BF16), jnp.ones((kw, past), BF16)], axis=0)
    lane_l = _lane((seq, LANE))
    lane_b = _lane((BLOCK, LANE))
    nb = seq // BLOCK
    rows_q = WIN_GROUP * BLOCK
    kk = _lane((rows_q, 3 * BLOCK))
    qi = _row((rows_q, 3 * BLOCK)) & (BLOCK - 1)
    rel = kk - BLOCK - qi
    bias_mid = jnp.where(rel <= WINDOW, jnp.where(rel >= -WINDOW, 0.0, NEG_INF), NEG_INF)
    bias = {(False, False): bias_mid,
            (True, False): jnp.where(kk >= BLOCK, bias_mid, NEG_INF),
            (False, True): jnp.where(kk < 2 * BLOCK, bias_mid, NEG_INF)}
    bias[(True, True)] = jnp.where(kk < 2 * BLOCK, bias[(True, False)], NEG_INF)
    for hk in range(WIN_KV_HEADS):
        heads = _win_heads(q, hk, lane_l)
        sink = _sink_col(sink_ref, layer, hk, BLOCK) * LOG2_E
        for n in range(nb):
            qs = jnp.concatenate([hd[n * BLOCK:(n + 1) * BLOCK] for hd in heads], axis=0)
            sb = _bdot_nt(qs, k_s[n * BLOCK:(n + 3) * BLOCK, :]) + bias[(n == 0, n == nb - 1)]
            sc = jnp.dot(qs.astype(BF16), ckt, preferred_element_type=F32)
            m = jnp.maximum(jnp.maximum(jnp.max(sb, axis=-1, keepdims=True),
                                        jnp.max(sc, axis=-1, keepdims=True)), sink)
            eb = _exp2_bf16(sb - m)
            ec = _exp2_bf16(sc - m)
            oe = (jnp.dot(eb, v_s[n * BLOCK:(n + 3) * BLOCK, :], preferred_element_type=F32)
                  + _bdot_nt(ec, cvt))
            o = oe[:, :LANE] * (1.0 / (oe[:, LANE:] + jnp.exp2(sink - m)))
            slabs = _win_place([o[gq * BLOCK:(gq + 1) * BLOCK] for gq in range(WIN_GROUP)], hk, lane_b)
            for pair, slab in enumerate(slabs):
                c0 = (hk * 2 + pair) * LANE
                y_ref[n * BLOCK:(n + 1) * BLOCK, c0:c0 + LANE] = slab.astype(BF16)


def _win_call(h, w_in, layer, sink, gq, gk, group, seq, rope=None, cache=None, states=(), name="win"):
    nt, tm = h.shape[0], h.shape[1]
    qw = WIN_Q_HEADS * HEAD_DIM
    kw = WIN_KV_HEADS * HEAD_DIM
    const2 = lambda a: pl.BlockSpec(a.shape, lambda i: (0, 0))
    in_specs = [
        pl.BlockSpec(memory_space=pltpu.SMEM),
        pl.BlockSpec((None, tm, D_MODEL), lambda i: (i, 0, 0)),
        _w_in_spec(layer, COL_WIN, qw + 2 * kw),
        const2(gq), const2(gk),
    ]
    args = [sink, h, w_in, gq, gk]
    y_spec = pl.BlockSpec((None, tm, qw), lambda i: (i, 0, 0))
    y_shape = jax.ShapeDtypeStruct((nt, tm, qw), BF16)
    scratch = [pltpu.VMEM((D_MODEL, qw + 2 * kw), BF16)]
    aliases = {}
    if cache is None:
        past = 0
        kv_spec = _state_spec(layer, group, (seq, kw))
        kv_shape = jax.ShapeDtypeStruct((nt * group, DEPTH, seq, kw), F32)
        out_specs = [y_spec, kv_spec, kv_spec]
        out_shape = [y_shape, kv_shape, kv_shape]
        aliases = {len(args) + n: 1 + n for n in range(len(states))}
        in_specs += [pl.BlockSpec(memory_space=pl.ANY)] * len(states)
        args += list(states)
    else:
        cos, sin = rope
        ck, cv = cache
        past = ck.shape[3]
        cspec = pl.BlockSpec((None, None, kw, past), lambda i: (i, layer, 0, 0))
        in_specs += [const2(cos), const2(sin), cspec, cspec]
        args += [cos, sin, ck, cv]
        out_specs = [y_spec]
        out_shape = [y_shape]
        scratch += [pltpu.VMEM((seq + 2 * BLOCK, kw), BF16), pltpu.VMEM((seq + 2 * BLOCK, 2 * kw), BF16)]
    return pl.pallas_call(
        functools.partial(_win_kernel, group=group, seq=seq, past=past, layer=layer),
        grid=(nt,),
        in_specs=in_specs,
        out_specs=out_specs,
        out_shape=out_shape,
        scratch_shapes=scratch,
        input_output_aliases=aliases,
        compiler_params=_cparams(("arbitrary",)),
        name=name,
    )(*args)


def _ret_kernel(*refs, group, seq, has_state, layer):
    if has_state:
        h_ref, wa_ref, wb_ref, df_ref, db_ref, s0f_ref, s0b_ref, y_ref, w_s, dec_s = refs
    else:
        h_ref, wa_ref, wb_ref, df_ref, db_ref = refs[:5]
        y_ref, sf_ref, sb_ref, w_s, dec_s = refs[-5:]
    qk_w = RET_HEADS * RET_DK
    v_w = RET_HEADS * RET_DV
    _cast_weights_once((wa_ref, wb_ref), w_s)
    proj = jnp.dot(h_ref[...], w_s[...], preferred_element_type=F32)
    q = proj[:, :qk_w]
    k = proj[:, qk_w:2 * qk_w] * (RET_DK ** -0.5)
    v = proj[:, 2 * qk_w:2 * qk_w + v_w]
    rg = proj[:, 2 * qk_w + v_w:]
    lgf_all = _log_gamma(df_ref[...])
    lgb_all = _log_gamma(db_ref[...])
    ch = min(seq, ATT_Q_TILE)
    n_ch = seq // ch

    @pl.when(pl.program_id(0) == 0)
    def _():
        rel = (_row((ch, ch)) - _lane((ch, ch))).astype(F32)
        for hd in range(RET_HEADS):
            lgf = lgf_all[hd:hd + 1, 0:1]
            lgb = lgb_all[hd:hd + 1, 0:1]
            dec_s[hd] = (jnp.where(rel >= 0, jnp.exp(jnp.maximum(rel, 0.0) * lgf), 0.0)
                         + jnp.where(rel <= 0, jnp.exp(jnp.maximum(-rel, 0.0) * lgb), 0.0))

    lane = _lane((ch, LANE))
    i_loc = _row((ch, LANE)).astype(F32)
    for hd in range(RET_HEADS):
        slab = slice((hd // 2) * LANE, (hd // 2 + 1) * LANE)
        vsl = slice(hd * RET_DV, (hd + 1) * RET_DV)
        half = hd % 2
        keep = lane >= RET_DK if half == 1 else lane < RET_DK
        lgf = lgf_all[hd:hd + 1, 0:1]
        lgb = lgb_all[hd:hd + 1, 0:1]
        q_in_f = jnp.exp((i_loc + 1.0) * lgf)
        q_in_b = jnp.exp((ch - i_loc) * lgb)
        k_out_f = jnp.exp((ch - 1.0 - i_loc) * lgf)
        k_out_b = jnp.exp(i_loc * lgb)
        carry_f = jnp.exp(ch * lgf)
        carry_b = jnp.exp(ch * lgb)
        dec = dec_s[hd]
        for g in range(group):
            rows = [slice(g * seq + c * ch, g * seq + (c + 1) * ch) for c in range(n_ch)]
            qm = [jnp.where(keep, q[r, slab], 0.0) for r in rows]
            kc = [k[r, slab] for r in rows]
            vb = [v[r, vsl].astype(BF16) for r in rows]
            o_state = [None] * n_ch
            for q_in, k_out, carry, s_ref, order in (
                    (q_in_f, k_out_f, carry_f, s0f_ref if has_state else sf_ref, range(n_ch)),
                    (q_in_b, k_out_b, carry_b, s0b_ref if has_state else sb_ref, range(n_ch - 1, -1, -1))):
                state = s_ref[hd // 2] if has_state else None
                for pos, c in enumerate(order):
                    if state is not None:
                        term = _bdot(qm[c] * q_in, state)
                        o_state[c] = term if o_state[c] is None else o_state[c] + term
                    if pos + 1 < n_ch or not has_state:
                        update = _bdot((kc[c] * k_out).T, vb[c])
                        state = update if state is None else state * carry + update
                if not has_state:
                    _store_state(s_ref, g, layer, state[half * RET_DK:(half + 1) * RET_DK], hd)
            for c, r in enumerate(rows):
                s = _bdot_nt(qm[c], kc[c])
                o = jnp.dot((s * dec).astype(BF16), vb[c], preferred_element_type=F32)
                if o_state[c] is not None:
                    o = o + o_state[c]
                y = _rms(o, None) * _silu(rg[r, vsl])
                y_ref[r, vsl] = y.astype(BF16)


def _ret_call(h, w_in, layer, dec_f, dec_b, group, seq, state=None, states=(), name="ret"):
    nt, tm = h.shape[0], h.shape[1]
    v_w = RET_HEADS * RET_DV
    ret_cols = COL_GATE - COL_RET
    const2 = lambda a: pl.BlockSpec(a.shape, lambda i: (0, 0))
    in_specs = [
        pl.BlockSpec((None, tm, D_MODEL), lambda i: (i, 0, 0)),
        _w_in_spec(layer, COL_RET, ret_cols // 2),
        _w_in_spec(layer, COL_RET + ret_cols // 2, ret_cols // 2),
        const2(dec_f), const2(dec_b),
    ]
    args = [h, w_in, w_in, dec_f, dec_b]
    y_spec = pl.BlockSpec((None, tm, v_w), lambda i: (i, 0, 0))
    y_shape = jax.ShapeDtypeStruct((nt, tm, v_w), BF16)
    aliases = {}
    if state is None:
        s_spec = _state_spec(layer, group, (RET_HEADS, RET_DK, RET_DV))
        s_shape = jax.ShapeDtypeStruct((nt * group, DEPTH, RET_HEADS, RET_DK, RET_DV), F32)
        out_specs = [y_spec, s_spec, s_spec]
        out_shape = [y_shape, s_shape, s_shape]
        aliases = {len(args) + n: 1 + n for n in range(len(states))}
        in_specs += [pl.BlockSpec(memory_space=pl.ANY)] * len(states)
        args += list(states)
    else:
        s0f, s0b = state
        sspec = pl.BlockSpec((None, None, 2, LANE, RET_DV), lambda i: (i, layer, 0, 0, 0))
        in_specs += [sspec, sspec]
        args += [s0f, s0b]
        out_specs = [y_spec]
        out_shape = [y_shape]
    return pl.pallas_call(
        functools.partial(_ret_kernel, group=group, seq=seq, has_state=state is not None, layer=layer),
        grid=(nt,),
        in_specs=in_specs,
        out_specs=out_specs,
        out_shape=out_shape,
        input_output_aliases=aliases,
        scratch_shapes=[pltpu.VMEM((D_MODEL, ret_cols), BF16),
                        pltpu.VMEM((RET_HEADS, min(seq, ATT_Q_TILE), min(seq, ATT_Q_TILE)), F32)],
        compiler_params=_cparams(("arbitrary",)),
        name=name,
    )(*args)


def _merge_ffn_kernel(x_ref, h_ref, mod_ref, gain_ref, y0_ref, y1_ref, y2_ref, y3_ref,
                      wg_ref, wb_ref, wo_ref, wi_ref, wo2_ref, o_ref, merged_s, act_s):
    m = mod_ref[...]
    for rows in _row_groups(x_ref.shape[0]):
        h = h_ref[rows, :]
        ys = [r[rows, :] for r in (y0_ref, y1_ref, y2_ref, y3_ref)]
        for c in range(D_MODEL // MERGE_TILE):
            cols = slice(c * MERGE_TILE, (c + 1) * MERGE_TILE)
            acc = None
            for b in range(N_BRANCH):
                gate = jax.nn.sigmoid(jnp.dot(
                    h, wg_ref[:, b * D_MODEL + c * MERGE_TILE:b * D_MODEL + (c + 1) * MERGE_TILE],
                    preferred_element_type=F32))
                part = gate * jnp.dot(ys[b], wb_ref[b, :, cols], preferred_element_type=F32)
                acc = part if acc is None else acc + part
            merged_s[rows, cols] = acc.astype(BF16)
        mixed = jnp.dot(merged_s[rows, :], wo_ref[...], preferred_element_type=F32)
        x = x_ref[rows, :] + m[5:6] * mixed
        o_ref[rows, :] = _swiglu_rows(x, m, 6, gain_ref[...], wi_ref, wo2_ref, act_s, rows)


def _merge_ffn_call(x, h, mod, mod_row, gain, ys, w_gate, w_branch, w_out, w_ff_in, w_ff_out, layer, name):
    nt, tm, _ = x.shape
    tok = pl.BlockSpec((None, tm, D_MODEL), lambda i: (i, 0, 0))
    ysp = pl.BlockSpec((None, tm, BRANCH_W), lambda i: (i, 0, 0))
    return pl.pallas_call(
        _merge_ffn_kernel,
        grid=(nt,),
        in_specs=[
            tok, tok,
            pl.BlockSpec((None, N_MOD, D_MODEL), lambda i: (mod_row(i), 0, 0)),
            pl.BlockSpec((1, D_MODEL), lambda i: (0, 0)),
            ysp, ysp, ysp, ysp,
            _resident((None, D_MODEL, N_BRANCH * D_MODEL), lambda i: (layer, 0, 0)),
            _resident((None, N_BRANCH, BRANCH_W, D_MODEL), lambda i: (layer, 0, 0, 0)),
            _resident((None, D_MODEL, D_MODEL), lambda i: (layer, 0, 0)),
            _resident((None, D_MODEL, 2 * D_FF), lambda i: (layer, 0, 0)),
            _resident((None, D_FF, D_MODEL), lambda i: (layer, 0, 0)),
        ],
        out_specs=tok,
        out_shape=jax.ShapeDtypeStruct(x.shape, F32),
        scratch_shapes=[pltpu.VMEM((tm, D_MODEL), BF16), pltpu.VMEM((tm, D_FF), BF16)],
        compiler_params=_cparams(("parallel",)),
        name=name,
    )(x, h, mod, gain, *ys, w_gate, w_branch, w_out, w_ff_in, w_ff_out)


def _dft_tables(seq):
    n_fft = 2 * seq
    idx = np.arange(seq, dtype=np.int64)
    ang = 2.0 * np.pi * ((idx[:, None] * idx[None, :]) % n_fft).astype(np.float64) / n_fft
    out = []
    for m in (np.cos(ang), np.sin(ang)):
        hi = jnp.asarray(m, F32).astype(BF16)
        lo = (jnp.asarray(m, F32) - hi.astype(F32)).astype(BF16)
        out += [hi, lo]
    return out


def _hyena_tables(seq):
    t = np.arange(seq, dtype=np.float32) / np.float32(seq)
    f = np.arange(1, HY_BANDS + 1, dtype=np.float32)
    ang = np.float32(2.0 * math.pi) * t[:, None] * f[None, :]
    feats = np.zeros((seq, LANE), np.float32)
    feats[:, 0] = t
    feats[:, 1:1 + HY_BANDS] = np.sin(ang)
    feats[:, 1 + HY_BANDS:HY_EMB] = np.cos(ang)
    min_decay = math.log(HY_TARGET) / HY_DECAY_LONG_PCT
    max_decay = math.log(HY_TARGET) / HY_DECAY_SHORT_PCT
    deltas = np.linspace(min_decay, max_decay, HY_W, dtype=np.float32)
    window = np.exp(-t[:, None] * np.abs(deltas)[None, :]).astype(np.float32)
    return jnp.asarray(feats), jnp.asarray(window)


def _rope_tables(seq):
    pos = np.arange(seq)
    row = (pos // GRID_W).astype(np.float32)
    col = (pos % GRID_W).astype(np.float32)
    inv_freq = (np.float32(ROPE_BASE) ** (-np.arange(ROPE_PAIRS, dtype=np.float32) / np.float32(ROPE_PAIRS)))
    lane = np.arange(LANE)
    in_head = lane % HEAD_DIM
    use_col = in_head >= HEAD_DIM // 2
    pair = in_head % ROPE_PAIRS
    second = (in_head % (2 * ROPE_PAIRS)) >= ROPE_PAIRS
    p = np.where(use_col[None, :], col[:, None], row[:, None]).astype(np.float32)
    ang = (p * inv_freq[pair][None, :]).astype(np.float32)
    cos = np.cos(ang).astype(np.float32)
    sin = np.sin(ang).astype(np.float32)
    sin = np.where(second[None, :], sin, -sin)
    return jnp.asarray(cos), jnp.asarray(sin)


def kernel(x_prompt, x_sample, c, cache_diff_k, cache_diff_v, cache_win_k, cache_win_v, state_ret_f, state_ret_b, c_ctx, norm_ffa, norm_mix, norm_ffb, w_ada, b_ada, w_ffa_in, w_ffa_out, w_ffb_in, w_ffb_out, w_in, hy_conv_w, hy_conv_b, hy_f_w1, hy_f_b1, hy_f_w2, hy_f_b2, hy_f_w3, hy_skip, diff_q_norm, diff_k_norm, diff_lambda, diff_subln, win_q_norm, win_k_norm, win_sink, ret_decay_f, ret_decay_b, w_branch, w_out):
    batch, seq, _ = x_prompt.shape
    dec_batch, dec_seq, _ = x_sample.shape
    past = cache_diff_k.shape[2]
    ctx_group = TOKEN_TILE // seq
    assert TOKEN_TILE % seq == 0 and batch % ctx_group == 0 and dec_seq == TOKEN_TILE
    assert 1 + dec_batch <= ADA_ROWS

    cond = jnp.zeros((ADA_ROWS, D_MODEL), F32).at[0].set(c_ctx).at[1:1 + dec_batch].set(c)
    mod = _ada_call(cond, w_ada, b_ada).reshape(DEPTH, ADA_ROWS, N_MOD, D_MODEL)

    bf = lambda a: a.astype(BF16)
    w_ffa_in_b, w_ffa_out_b, w_ffb_in_b, w_ffb_out_b = bf(w_ffa_in), bf(w_ffa_out), bf(w_ffb_in), bf(w_ffb_out)
    w_gate = bf(w_in[:, :, COL_GATE:])
    w_branch_b, w_out_b = bf(w_branch), bf(w_out)

    rope = _rope_tables(dec_seq)
    tables = {s: (_hyena_tables(s), _dft_tables(s)) for s in (seq, dec_seq)}

    ck_d = jnp.transpose(cache_diff_k, (0, 1, 3, 4, 5, 2)).reshape(dec_batch, DEPTH, -1, past)
    cv_d = cache_diff_v.reshape(dec_batch, DEPTH, past * DIFF_HEADS, 2 * HEAD_DIM)
    ck_w = jnp.transpose(cache_win_k, (0, 1, 3, 4, 2)).reshape(dec_batch, DEPTH, -1, past)
    cv_w = jnp.transpose(cache_win_v, (0, 1, 3, 4, 2)).reshape(dec_batch, DEPTH, -1, past)
    s0_f = state_ret_f.reshape(dec_batch, DEPTH, 2, LANE, RET_DV)
    s0_b = state_ret_b.reshape(dec_batch, DEPTH, 2, LANE, RET_DV)

    pad_rows = lambda a, n: jnp.pad(a, ((0, n - a.shape[0]), (0, 0)))
    tile_lanes = lambda a, n: jnp.tile(a.reshape(1, -1), (1, n))

    tiled = lambda a, tile: a.reshape(-1, tile, a.shape[-1])
    y_ctx = x_prompt
    y_lat = x_sample
    ctx_row = lambda tile: (lambda i: 0)
    lat_row = lambda tile: (lambda i: i // (dec_seq // tile) + 1)
    st_diff, st_win, st_ret = (), (), ()

    for l in range(DEPTH):
        lam_init = 0.8 - 0.6 * math.exp(-0.3 * l)
        gain = lambda a: a[l].reshape(1, -1)
        gq_d, gk_d = tile_lanes(diff_q_norm[l], 2 * DIFF_HEADS), tile_lanes(diff_k_norm[l], 2 * DIFF_HEADS)
        gs_d = tile_lanes(diff_subln[l], DIFF_HEADS)
        gq_w, gk_w = tile_lanes(win_q_norm[l], WIN_Q_HEADS), tile_lanes(win_k_norm[l], WIN_KV_HEADS)
        dec_f = jnp.broadcast_to(ret_decay_f[l][:, None], (RET_HEADS, LANE))
        dec_b = jnp.broadcast_to(ret_decay_b[l][:, None], (RET_HEADS, LANE))
        w1 = jnp.pad(hy_f_w1[l], ((0, LANE - HY_EMB), (0, LANE - HY_FH)))
        b1 = jnp.pad(hy_f_b1[l].reshape(1, -1), ((0, 0), (0, LANE - HY_FH)))
        w2 = jnp.pad(hy_f_w2[l], ((0, LANE - HY_FH), (0, LANE - HY_FH)))
        b2 = jnp.pad(hy_f_b2[l].reshape(1, -1), ((0, 0), (0, LANE - HY_FH)))
        w3 = pad_rows(hy_f_w3[l], LANE)
        cw, cb, skip = hy_conv_w[l], hy_conv_b[l].reshape(1, -1), hy_skip[l].reshape(1, -1)

        for is_lat in (False, True):
            x = y_lat if is_lat else y_ctx
            s_len = dec_seq if is_lat else seq
            group = 1 if is_lat else ctx_group
            mod_row = lat_row if is_lat else ctx_row
            tag = f"{'lat' if is_lat else 'ctx'}{l}"
            (feats, window), (c_hi, c_lo, s_hi, s_lo) = tables[s_len]

            x, h = _ffn_a_call(tiled(x, FFN_A_TILE), mod[l], mod_row(FFN_A_TILE), gain(norm_ffa), gain(norm_mix),
                               w_ffa_in_b, w_ffa_out_b, l, f"ffn_a_{tag}")
            h = tiled(h, TOKEN_TILE)

            kr, ki, kn = _hy_filter_call(s_len, feats, w1, b1, w2, b2, w3, window, c_hi, c_lo, s_hi, s_lo)
            y_hy = _hyena_call(h, w_in, l, cw, cb, kr, ki, kn, skip, c_hi, s_hi, group, s_len, f"hyena_{tag}")
            if is_lat:
                (y_diff,) = _diff_call(h, w_in, l, gq_d, gk_d, diff_lambda[l], gs_d, lam_init,
                                       group, s_len, rope=rope, cache=(ck_d, cv_d), name=f"diff_{tag}")
                (y_win,) = _win_call(h, w_in, l, win_sink, gq_w, gk_w, group, s_len,
                                     rope=rope, cache=(ck_w, cv_w), name=f"win_{tag}")
                (y_ret,) = _ret_call(h, w_in, l, dec_f, dec_b, group, s_len, state=(s0_f, s0_b),
                                     name=f"ret_{tag}")
            else:
                y_diff, *st_diff = _diff_call(h, w_in, l, gq_d, gk_d, diff_lambda[l], gs_d, lam_init,
                                              group, s_len, states=st_diff, name=f"diff_{tag}")
                y_win, *st_win = _win_call(h, w_in, l, win_sink, gq_w, gk_w, group, s_len,
                                           states=st_win, name=f"win_{tag}")
                y_ret, *st_ret = _ret_call(h, w_in, l, dec_f, dec_b, group, s_len, states=st_ret,
                                           name=f"ret_{tag}")

            x = _merge_ffn_call(tiled(x, MERGE_FFN_TILE), tiled(h, MERGE_FFN_TILE), mod[l], mod_row(MERGE_FFN_TILE),
                                gain(norm_ffb),
                                tuple(tiled(y, MERGE_FFN_TILE) for y in (y_hy, y_diff, y_win, y_ret)),
                                w_gate, w_branch_b, w_out_b, w_ffb_in_b, w_ffb_out_b, l, f"merge_ffn_b_{tag}")
            if is_lat:
                y_lat = x
            else:
                y_ctx = x

    return (y_ctx.reshape(batch, seq, D_MODEL), y_lat.reshape(dec_batch, dec_seq, D_MODEL),
            st_diff[0].reshape(batch, DEPTH, seq, DIFF_HEADS, 2, HEAD_DIM),
            st_diff[1].reshape(batch, DEPTH, seq, DIFF_HEADS, 2 * HEAD_DIM),
            st_win[0].reshape(batch, DEPTH, seq, WIN_KV_HEADS, HEAD_DIM),
            st_win[1].reshape(batch, DEPTH, seq, WIN_KV_HEADS, HEAD_DIM),
            st_ret[0], st_ret[1])
```

```python
import functools
import math

import jax
import jax.numpy as jnp
import numpy as np
from jax import lax
from jax.experimental import pallas as pl
from jax.experimental.pallas import tpu as pltpu

F32 = jnp.float32
BF16 = jnp.bfloat16

D_MODEL = 1024
DEPTH = 2
GRID_W = 64
HEAD_DIM = 64
ROPE_PAIRS = HEAD_DIM // 4
ROPE_BASE = 10000.0
EPS = 1e-6
NEG_INF = -1e30
LOG2_E = math.log2(math.e)
D_FF = 2816
N_MOD = 9
N_BRANCH = 4
BRANCH_W = 512

HY_W = BRANCH_W
HY_BANDS = 16
HY_EMB = 1 + 2 * HY_BANDS
HY_FH = 64
HY_SIN_W = 1.0
HY_TARGET = 1e-2
HY_DECAY_SHORT_PCT = 0.3
HY_DECAY_LONG_PCT = 1.5

DIFF_HEADS = 4
WIN_Q_HEADS = 8
WIN_KV_HEADS = 2
WIN_GROUP = WIN_Q_HEADS // WIN_KV_HEADS
WINDOW = 128
BLOCK = 128
RET_HEADS = 4
RET_DK = 64
RET_DV = 128

COL_HY = 0
COL_DIFF = 3 * HY_W
COL_WIN = COL_DIFF + 3 * (2 * DIFF_HEADS * HEAD_DIM)
COL_RET = COL_WIN + (WIN_Q_HEADS + 2 * WIN_KV_HEADS) * HEAD_DIM
COL_GATE = COL_RET + 2 * RET_HEADS * (RET_DK + RET_DV)
IN_COLS = COL_GATE + N_BRANCH * D_MODEL

LANE = 128
TOKEN_TILE = 1024
FFN_A_TILE = 1024
MERGE_FFN_TILE = 512
FF_TILE = 256
ROW_GROUP = 256
MERGE_TILE = 256
ADA_TILE = 2304
ADA_ROWS = 16
ATT_Q_TILE = 256
VMEM_LIMIT = 56 * 1024 * 1024


def _cparams(sem):
    return pltpu.CompilerParams(dimension_semantics=sem, vmem_limit_bytes=VMEM_LIMIT)


def _bdot(a, b):
    return jnp.dot(a.astype(BF16), b.astype(BF16), preferred_element_type=F32)


def _bdot_nt(a, b):
    return lax.dot_general(a.astype(BF16), b.astype(BF16), (((1,), (1,)), ((), ())),
                           preferred_element_type=F32)


def _split(a):
    hi = a.astype(BF16)
    lo = (a - hi.astype(F32)).astype(BF16)
    return hi, lo


def _dot3_pre(ah, al, b):
    bh, bl = _split(b)
    d = functools.partial(jnp.dot, preferred_element_type=F32)
    return d(ah, bh) + d(ah, bl) + d(al, bh)


def _dot3(a, b):
    return _dot3_pre(*_split(a), b)


def _rms_heads(x, gain):
    first = _lane((x.shape[0], LANE)) < HEAD_DIM
    out = []
    for s in range(x.shape[1] // LANE):
        xs = x[:, s * LANE:(s + 1) * LANE]
        sq = xs * xs
        lo = jnp.sum(jnp.where(first, sq, 0.0), axis=-1, keepdims=True)
        hi = jnp.sum(jnp.where(first, 0.0, sq), axis=-1, keepdims=True)
        r_lo = lax.rsqrt(lo * (1.0 / HEAD_DIM) + EPS)
        r_hi = lax.rsqrt(hi * (1.0 / HEAD_DIM) + EPS)
        out.append(xs * jnp.where(first, r_lo, r_hi))
    return jnp.concatenate(out, axis=1) * gain


def _rms(x, gain):
    y = x * lax.rsqrt(jnp.mean(x * x, axis=-1, keepdims=True) + EPS)
    return y if gain is None else y * gain


def _silu(x):
    return x * jax.nn.sigmoid(x)


def _exp2_bf16(x):
    return jnp.exp2(x).astype(BF16)


def _lane(shape):
    return lax.broadcasted_iota(jnp.int32, shape, 1)


def _row(shape):
    return lax.broadcasted_iota(jnp.int32, shape, 0)


def _rope128(x, cos, sin):
    partner = jnp.where((_lane(x.shape) & 31) < 16,
                        pltpu.roll(x, LANE - ROPE_PAIRS, axis=1),
                        pltpu.roll(x, ROPE_PAIRS, axis=1))
    return x * cos + partner * sin


def _cast_weights_once(w_refs, w_s):
    @pl.when(pl.program_id(0) == 0)
    def _():
        off = 0
        for r in w_refs:
            w_s[:, off:off + r.shape[1]] = r[...].astype(BF16)
            off += r.shape[1]


def _w_in_spec(layer, col, width):
    assert col % width == 0
    return pl.BlockSpec((None, D_MODEL, width), lambda i: (layer, 0, col // width),
                        pipeline_mode=pl.Buffered(1))


def _state_spec(layer, group, tail):
    zeros = (0,) * len(tail)
    if layer == 0:
        return pl.BlockSpec((group, DEPTH) + tail, lambda i: (i, 0) + zeros)
    return pl.BlockSpec((group, None) + tail, lambda i: (i, layer) + zeros)


def _store_state(ref, g, layer, value, *sub):
    if layer == 0:
        ref[(g, 0) + sub] = value
        for later in range(1, DEPTH):
            ref[(g, later) + sub] = jnp.zeros_like(value)
    else:
        ref[(g,) + sub] = value


def _log_gamma(decay):
    x = -decay
    return -(jnp.maximum(x, 0.0) + jnp.log1p(jnp.exp(-jnp.abs(x))))


def _ada_kernel(cond_ref, w_ref, b_ref, o_ref):
    s = _silu(cond_ref[...])
    o_ref[...] = _bdot(s, w_ref[...]) + b_ref[...]


def _ada_call(cond, w_ada, b_ada):
    rows = cond.shape[0]
    n_out = N_MOD * D_MODEL
    return pl.pallas_call(
        _ada_kernel,
        grid=(DEPTH, n_out // ADA_TILE),
        in_specs=[
            pl.BlockSpec((rows, D_MODEL), lambda l, j: (0, 0)),
            pl.BlockSpec((None, D_MODEL, ADA_TILE), lambda l, j: (l, 0, j)),
            pl.BlockSpec((None, 1, ADA_TILE), lambda l, j: (l, 0, j)),
        ],
        out_specs=pl.BlockSpec((None, rows, ADA_TILE), lambda l, j: (l, 0, j)),
        out_shape=jax.ShapeDtypeStruct((DEPTH, rows, n_out), F32),
        compiler_params=_cparams(("parallel", "parallel")),
        name="ada_mod",
    )(cond, w_ada, b_ada.reshape(DEPTH, 1, n_out))


def _swiglu_rows(x, m, mod_base, gain, wi_ref, wo_ref, act_s, rows):
    h = (_rms(x, gain) * (1.0 + m[mod_base + 1:mod_base + 2]) + m[mod_base:mod_base + 1]).astype(BF16)
    for c in range(D_FF // FF_TILE):
        a = jnp.dot(h, wi_ref[:, c * FF_TILE:(c + 1) * FF_TILE], preferred_element_type=F32)
        g = jnp.dot(h, wi_ref[:, D_FF + c * FF_TILE:D_FF + (c + 1) * FF_TILE], preferred_element_type=F32)
        act_s[rows, c * FF_TILE:(c + 1) * FF_TILE] = (_silu(a) * g).astype(BF16)
    out = jnp.dot(act_s[rows, :], wo_ref[...], preferred_element_type=F32)
    return x + 0.5 * m[mod_base + 2:mod_base + 3] * out


def _row_groups(n_rows):
    return [slice(p * ROW_GROUP, (p + 1) * ROW_GROUP) for p in range(n_rows // ROW_GROUP)]


def _ffn_a_kernel(x_ref, mod_ref, gain_ref, gain2_ref, wi_ref, wo_ref, y_ref, h2_ref, act_s):
    m = mod_ref[...]
    for rows in _row_groups(x_ref.shape[0]):
        y = _swiglu_rows(x_ref[rows, :], m, 0, gain_ref[...], wi_ref, wo_ref, act_s, rows)
        y_ref[rows, :] = y
        h2_ref[rows, :] = (_rms(y, gain2_ref[...]) * (1.0 + m[4:5]) + m[3:4]).astype(BF16)


def _resident(block_shape, index_map):
    return pl.BlockSpec(block_shape, index_map, pipeline_mode=pl.Buffered(1))


def _ffn_a_call(x, mod, mod_row, gain, gain2, w_in, w_out, layer, name):
    nt, tm, _ = x.shape
    tok = pl.BlockSpec((None, tm, D_MODEL), lambda i: (i, 0, 0))
    vec = pl.BlockSpec((1, D_MODEL), lambda i: (0, 0))
    return pl.pallas_call(
        _ffn_a_kernel,
        grid=(nt,),
        in_specs=[
            tok, pl.BlockSpec((None, N_MOD, D_MODEL), lambda i: (mod_row(i), 0, 0)), vec, vec,
            _resident((None, D_MODEL, 2 * D_FF), lambda i: (layer, 0, 0)),
            _resident((None, D_FF, D_MODEL), lambda i: (layer, 0, 0)),
        ],
        out_specs=[tok, tok],
        out_shape=[jax.ShapeDtypeStruct(x.shape, F32), jax.ShapeDtypeStruct(x.shape, BF16)],
        scratch_shapes=[pltpu.VMEM((tm, D_FF), BF16)],
        compiler_params=_cparams(("parallel",)),
        name=name,
    )(x, mod, gain, gain2, w_in, w_out)


def _hy_filter_kernel(feats_ref, w1_ref, b1_ref, w2_ref, b2_ref, w3_ref, win_ref,
                      ch_ref, cl_ref, sh_ref, sl_ref, kr_ref, ki_ref, kn_ref, *, seq):
    n_fft = 2 * seq
    z = jnp.sin(HY_SIN_W * (_dot3(feats_ref[...], w1_ref[...]) + b1_ref[...]))
    z = jnp.sin(HY_SIN_W * (_dot3(z, w2_ref[...]) + b2_ref[...]))
    zz = _dot3(z, w3_ref[...])
    win = win_ref[...]
    hf = zz[:, :HY_W] * win
    hb = zz[:, HY_W:] * win
    norm = (jnp.sum(jnp.abs(hf), axis=0, keepdims=True)
            + jnp.sum(jnp.abs(hb), axis=0, keepdims=True))
    hf = hf / norm
    hb = hb / norm
    row = _row(hf.shape)
    hb0 = jnp.where(row == 0, 0.0, hb)
    even = hf + hb0
    odd = hb0 - hf
    wk = jnp.where(row == 0, 1.0 / n_fft, 2.0 / n_fft)
    kr_ref[...] = _dot3_pre(ch_ref[...], cl_ref[...], even) * wk
    ki_ref[...] = _dot3_pre(sh_ref[...], sl_ref[...], odd) * wk
    sgn = jnp.where((row & 1) == 0, 1.0, -1.0)
    kn_ref[...] = jnp.sum(even * sgn, axis=0, keepdims=True) * (1.0 / n_fft)


def _hy_filter_call(seq, feats, w1, b1, w2, b2, w3, win, ch, cl, sh, sl):
    args = (feats, w1, b1, w2, b2, w3, win, ch, cl, sh, sl)
    return pl.pallas_call(
        functools.partial(_hy_filter_kernel, seq=seq),
        out_shape=[jax.ShapeDtypeStruct((seq, HY_W), F32),
                   jax.ShapeDtypeStruct((seq, HY_W), F32),
                   jax.ShapeDtypeStruct((1, HY_W), F32)],
        compiler_params=pltpu.CompilerParams(vmem_limit_bytes=VMEM_LIMIT),
        name=f"hyena_filter_{seq}",
    )(*args)


def _hyena_kernel(h_ref, w_ref, cw_ref, cb_ref, kr_ref, ki_ref, kn_ref, skip_ref,
                  c_ref, s_ref, y_ref, w_s, *, group, seq):
    _cast_weights_once((w_ref,), w_s)
    proj = jnp.dot(h_ref[...], w_s[...], preferred_element_type=F32)
    cw = cw_ref[...]
    cmat = c_ref[...]
    smat = s_ref[...]
    kr = kr_ref[...]
    ki = ki_ref[...]
    row = _row((seq, 3 * HY_W))
    row_w = _row((seq, HY_W))
    sgn = jnp.where((row_w & 1) == 0, 1.0, -1.0)
    for g in range(group):
        hy = proj[g * seq:(g + 1) * seq]
        prev = jnp.where(row == 0, 0.0, pltpu.roll(hy, 1, axis=0))
        nxt = jnp.where(row == seq - 1, 0.0, pltpu.roll(hy, seq - 1, axis=0))
        u = prev * cw[0:1] + hy * cw[1:2] + nxt * cw[2:3] + cb_ref[...]
        v, x0, x1 = u[:, :HY_W], u[:, HY_W:2 * HY_W], u[:, 2 * HY_W:]
        z = v * x1
        zb = z.astype(BF16)
        zr = jnp.dot(cmat, zb, preferred_element_type=F32)
        zs = jnp.dot(smat, zb, preferred_element_type=F32)
        yr = (zr * kr + zs * ki).astype(BF16)
        yi = (zr * ki - zs * kr).astype(BF16)
        nyq = jnp.sum(z * sgn, axis=0, keepdims=True) * kn_ref[...]
        conv = (jnp.dot(cmat, yr, preferred_element_type=F32)
                - jnp.dot(smat, yi, preferred_element_type=F32) + sgn * nyq)
        y_ref[g * seq:(g + 1) * seq, :] = (x0 * (conv + skip_ref[...] * z)).astype(BF16)


def _hyena_call(h, w_in, layer, cw, cb, kr, ki, kn, skip, cmat, smat, group, seq, name):
    nt, tm, _ = h.shape
    const2 = lambda a: pl.BlockSpec(a.shape, lambda i: (0, 0))
    return pl.pallas_call(
        functools.partial(_hyena_kernel, group=group, seq=seq),
        grid=(nt,),
        in_specs=[
            pl.BlockSpec((None, tm, D_MODEL), lambda i: (i, 0, 0)),
            _w_in_spec(layer, COL_HY, 3 * HY_W),
            const2(cw), const2(cb), const2(kr), const2(ki), const2(kn), const2(skip),
            const2(cmat), const2(smat),
        ],
        out_specs=pl.BlockSpec((None, tm, HY_W), lambda i: (i, 0, 0)),
        out_shape=jax.ShapeDtypeStruct((nt, tm, HY_W), BF16),
        scratch_shapes=[pltpu.VMEM((D_MODEL, 3 * HY_W), BF16)],
        compiler_params=_cparams(("arbitrary",)),
        name=name,
    )(h, w_in, cw, cb, kr, ki, kn, skip, cmat, smat)


def _diff_kernel(*refs, group, seq, past, lam_init, layer):
    has_cache = past > 0
    if has_cache:
        (h_ref, w_ref, gq_ref, gk_ref, lam_ref, gs_ref, cos_ref, sin_ref,
         ckt_ref, cv_ref, y_ref, w_s, k_s, v_s) = refs
    else:
        h_ref, w_ref, gq_ref, gk_ref, lam_ref, gs_ref = refs[:6]
        y_ref, ko_ref, vo_ref, w_s = refs[-4:]
    width = 2 * DIFF_HEADS * HEAD_DIM
    _cast_weights_once((w_ref,), w_s)
    proj = jnp.dot(h_ref[...], w_s[...], preferred_element_type=F32)
    q, k, v = proj[:, :width], proj[:, width:2 * width], proj[:, 2 * width:]
    q = _rms_heads(q, gq_ref[...])
    k = _rms_heads(k, gk_ref[...])
    dl = lam_ref[...]
    lam = (jnp.exp(jnp.sum(dl[0:1] * dl[1:2], axis=1, keepdims=True))
           - jnp.exp(jnp.sum(dl[2:3] * dl[3:4], axis=1, keepdims=True)) + lam_init)
    ones = jnp.ones((seq, LANE), BF16)
    if has_cache:
        cos, sin = cos_ref[...], sin_ref[...]
        q = jnp.concatenate([_rope128(q[:, s * LANE:(s + 1) * LANE], cos, sin)
                             for s in range(width // LANE)], axis=1)
        k = jnp.concatenate([_rope128(k[:, s * LANE:(s + 1) * LANE], cos, sin)
                             for s in range(width // LANE)], axis=1)
        k_s[...] = k.astype(BF16)
        for hh in range(DIFF_HEADS):
            v_s[0:seq, 2 * hh * LANE:(2 * hh + 1) * LANE] = v[:, hh * LANE:(hh + 1) * LANE].astype(BF16)
            v_s[seq:seq + past, 2 * hh * LANE:(2 * hh + 1) * LANE] = (
                cv_ref[pl.ds(hh, past, stride=DIFF_HEADS), :].astype(BF16))
            v_s[:, (2 * hh + 1) * LANE:(2 * hh + 2) * LANE] = jnp.ones((seq + past, LANE), BF16)
    else:
        for g in range(group):
            rows = slice(g * seq, (g + 1) * seq)
            _store_state(ko_ref, g, layer, k[rows])
            for hh in range(DIFF_HEADS):
                _store_state(vo_ref, g, layer, v[rows, hh * LANE:(hh + 1) * LANE],
                             pl.ds(hh, seq, stride=DIFF_HEADS))
    q = q * (HEAD_DIM ** -0.5 * LOG2_E)
    tq = min(seq, ATT_Q_TILE)
    lane = _lane((tq, LANE))
    for g in range(group):
        for hh in range(DIFF_HEADS):
            sl = slice(hh * LANE, (hh + 1) * LANE)
            if has_cache:
                keys = k_s[:, sl]
                vals = v_s[0:seq, 2 * hh * LANE:(2 * hh + 2) * LANE]
                vals_c = v_s[seq:seq + past, 2 * hh * LANE:(2 * hh + 2) * LANE]
                keys_t = ckt_ref[sl, :].astype(BF16)
            else:
                keys = k[g * seq:(g + 1) * seq, sl].astype(BF16)
                vals = jnp.concatenate([v[g * seq:(g + 1) * seq, sl].astype(BF16), ones], axis=1)
            for qi in range(seq // tq):
                r0 = g * seq + qi * tq
                q128 = q[r0:r0 + tq, sl]
                qs = jnp.concatenate([jnp.where(lane < HEAD_DIM, q128, 0.0),
                                      jnp.where(lane >= HEAD_DIM, q128, 0.0)], axis=0).astype(BF16)
                s = _bdot_nt(qs, keys)
                if has_cache:
                    sc = jnp.dot(qs, keys_t, preferred_element_type=F32)
                    m = jnp.maximum(jnp.max(s, axis=-1, keepdims=True), jnp.max(sc, axis=-1, keepdims=True))
                    oe = (jnp.dot(_exp2_bf16(s - m), vals, preferred_element_type=F32)
                          + jnp.dot(_exp2_bf16(sc - m), vals_c, preferred_element_type=F32))
                else:
                    e = _exp2_bf16(s - jnp.max(s, axis=-1, keepdims=True))
                    oe = jnp.dot(e, vals, preferred_element_type=F32)
                r = 1.0 / oe[:, LANE:]
                o = oe[:tq, :LANE] * r[:tq] - oe[tq:, :LANE] * (lam * r[tq:])
                y = _rms(o, gs_ref[:, sl]) * (1.0 - lam_init)
                y_ref[r0:r0 + tq, sl] = y.astype(BF16)


def _diff_call(h, w_in, layer, gq, gk, lam, gs, lam_init, group, seq,
               rope=None, cache=None, states=(), name="diff"):
    nt, tm = h.shape[0], h.shape[1]
    width = 2 * DIFF_HEADS * HEAD_DIM
    const2 = lambda a: pl.BlockSpec(a.shape, lambda i: (0, 0))
    in_specs = [
        pl.BlockSpec((None, tm, D_MODEL), lambda i: (i, 0, 0)),
        _w_in_spec(layer, COL_DIFF, 3 * width),
        const2(gq), const2(gk), const2(lam), const2(gs),
    ]
    args = [h, w_in, gq, gk, lam, gs]
    y_spec = pl.BlockSpec((None, tm, width), lambda i: (i, 0, 0))
    y_shape = jax.ShapeDtypeStruct((nt, tm, width), BF16)
    scratch = [pltpu.VMEM((D_MODEL, 3 * width), BF16)]
    aliases = {}
    if cache is None:
        past = 0
        k_spec = _state_spec(layer, group, (seq, width))
        k_shape = jax.ShapeDtypeStruct((nt * group, DEPTH, seq, width), F32)
        v_spec = _state_spec(layer, group, (seq * DIFF_HEADS, 2 * HEAD_DIM))
        v_shape = jax.ShapeDtypeStruct((nt * group, DEPTH, seq * DIFF_HEADS, 2 * HEAD_DIM), F32)
        out_specs = [y_spec, k_spec, v_spec]
        out_shape = [y_shape, k_shape, v_shape]
        aliases = {len(args) + n: 1 + n for n in range(len(states))}
        in_specs += [pl.BlockSpec(memory_space=pl.ANY)] * len(states)
        args += list(states)
    else:
        cos, sin = rope
        ckt, cv = cache
        past = ckt.shape[3]
        in_specs += [const2(cos), const2(sin),
                     pl.BlockSpec((None, None, width, past), lambda i: (i, layer, 0, 0)),
                     pl.BlockSpec((None, None, past * DIFF_HEADS, 2 * HEAD_DIM), lambda i: (i, layer, 0, 0))]
        args += [cos, sin, ckt, cv]
        out_specs = [y_spec]
        out_shape = [y_shape]
        scratch += [pltpu.VMEM((seq, width), BF16), pltpu.VMEM((seq + past, 2 * width), BF16)]
    return pl.pallas_call(
        functools.partial(_diff_kernel, group=group, seq=seq, past=past, lam_init=lam_init, layer=layer),
        grid=(nt,),
        in_specs=in_specs,
        out_specs=out_specs,
        out_shape=out_shape,
        scratch_shapes=scratch,
        input_output_aliases=aliases,
        compiler_params=_cparams(("arbitrary",)),
        name=name,
    )(*args)


def _win_heads(q, hk, lane):
    keep = lane >= HEAD_DIM if hk == 1 else lane < HEAD_DIM
    return [jnp.where(keep, q[:, s * LANE:(s + 1) * LANE], 0.0) for s in range(WIN_GROUP)]


def _win_place(o_by_kv, lane):
    return [jnp.where(lane < HEAD_DIM, o_by_kv[0][s], o_by_kv[1][s]) for s in range(WIN_GROUP)]


def _sink_col(sink_ref, layer, hk, rows_per_head):
    rows = WIN_GROUP * rows_per_head
    r = _row((rows, 1))
    col = jnp.full((rows, 1), sink_ref[layer, hk * WIN_GROUP], F32)
    for gq in range(1, WIN_GROUP):
        col = jnp.where(r >= gq * rows_per_head, sink_ref[layer, hk * WIN_GROUP + gq], col)
    return col


def _win_kernel(*refs, group, seq, past, layer):
    has_cache = past > 0
    if has_cache:
        (sink_ref, h_ref, w_ref, gq_ref, gk_ref, cos_ref, sin_ref, ck_ref, cv_ref,
         y_ref, w_s, k_s, v_s) = refs
    else:
        sink_ref, h_ref, w_ref, gq_ref, gk_ref = refs[:5]
        y_ref, ko_ref, vo_ref, w_s = refs[-4:]
    qw = WIN_Q_HEADS * HEAD_DIM
    kw = WIN_KV_HEADS * HEAD_DIM
    _cast_weights_once((w_ref,), w_s)
    proj = jnp.dot(h_ref[...], w_s[...], preferred_element_type=F32)
    q, k, v = proj[:, :qw], proj[:, qw:qw + kw], proj[:, qw + kw:]
    q = _rms_heads(q, gq_ref[...])
    k = _rms_heads(k, gk_ref[...])
    scale = HEAD_DIM ** -0.5 * LOG2_E
    if not has_cache:
        for g in range(group):
            _store_state(ko_ref, g, layer, k[g * seq:(g + 1) * seq])
            _store_state(vo_ref, g, layer, v[g * seq:(g + 1) * seq])
        q = q * scale
        lane = _lane((seq, LANE))
        ones = jnp.ones((seq, LANE), BF16)
        for g in range(group):
            rows = slice(g * seq, (g + 1) * seq)
            kb = k[rows].astype(BF16)
            vb = jnp.concatenate([v[rows].astype(BF16), ones], axis=1)
            qg = q[rows]
            o_by_kv = []
            for hk in range(WIN_KV_HEADS):
                qs = jnp.concatenate(_win_heads(qg, hk, lane), axis=0)
                s = _bdot_nt(qs, kb)
                sink = _sink_col(sink_ref, layer, hk, seq) * LOG2_E
                m = jnp.maximum(jnp.max(s, axis=-1, keepdims=True), sink)
                oe = jnp.dot(_exp2_bf16(s - m), vb, preferred_element_type=F32)
                o = oe[:, :LANE] * (1.0 / (oe[:, LANE:] + jnp.exp2(sink - m)))
                o_by_kv.append([o[gq * seq:(gq + 1) * seq] for gq in range(WIN_GROUP)])
            for s_idx, slab in enumerate(_win_place(o_by_kv, lane)):
                y_ref[rows, s_idx * LANE:(s_idx + 1) * LANE] = slab.astype(BF16)
        return

    cos, sin = cos_ref[...], sin_ref[...]
    q = jnp.concatenate([_rope128(q[:, s * LANE:(s + 1) * LANE], cos, sin)
                         for s in range(qw // LANE)], axis=1) * scale
    k = _rope128(k, cos, sin)
    zpad = jnp.zeros((BLOCK, kw), BF16)
    k_s[0:BLOCK, :] = zpad
    k_s[BLOCK:BLOCK + seq, :] = k.astype(BF16)
    k_s[BLOCK + seq:2 * BLOCK + seq, :] = zpad
    v_s[0:BLOCK, 0:kw] = zpad
    v_s[BLOCK:BLOCK + seq, 0:kw] = v.astype(BF16)
    v_s[BLOCK + seq:2 * BLOCK + seq, 0:kw] = zpad
    v_s[:, kw:2 * kw] = jnp.ones((seq + 2 * BLOCK, kw), BF16)
    ckt = ck_ref[...].astype(BF16)
    cvt = jnp.concatenate([cv_ref[...].astype(BF16), jnp.ones((kw, past), BF16)], axis=0)
    lane_l = _lane((seq, LANE))
    lane_b = _lane((BLOCK, LANE))
    nb = seq // BLOCK
    rows_q = WIN_GROUP * BLOCK
    kk = _lane((rows_q, 3 * BLOCK))
    qi = _row((rows_q, 3 * BLOCK)) & (BLOCK - 1)
    rel = kk - BLOCK - qi
    bias_mid = jnp.where(rel <= WINDOW, jnp.where(rel >= -WINDOW, 0.0, NEG_INF), NEG_INF)
    bias = {(False, False): bias_mid,
            (True, False): jnp.where(kk >= BLOCK, bias_mid, NEG_INF),
            (False, True): jnp.where(kk < 2 * BLOCK, bias_mid, NEG_INF)}
    bias[(True, True)] = jnp.where(kk < 2 * BLOCK, bias[(True, False)], NEG_INF)
    heads = [_win_heads(q, hk, lane_l) for hk in range(WIN_KV_HEADS)]
    sinks = [_sink_col(sink_ref, layer, hk, BLOCK) * LOG2_E for hk in range(WIN_KV_HEADS)]
    for n in range(nb):
        o_by_kv = []
        for hk in range(WIN_KV_HEADS):
            sink = sinks[hk]
            qs = jnp.concatenate([hd[n * BLOCK:(n + 1) * BLOCK] for hd in heads[hk]], axis=0)
            sb = _bdot_nt(qs, k_s[n * BLOCK:(n + 3) * BLOCK, :]) + bias[(n == 0, n == nb - 1)]
            sc = jnp.dot(qs.astype(BF16), ckt, preferred_element_type=F32)
            m = jnp.maximum(jnp.maximum(jnp.max(sb, axis=-1, keepdims=True),
                                        jnp.max(sc, axis=-1, keepdims=True)), sink)
            eb = _exp2_bf16(sb - m)
            ec = _exp2_bf16(sc - m)
            oe = (jnp.dot(eb, v_s[n * BLOCK:(n + 3) * BLOCK, :], preferred_element_type=F32)
                  + _bdot_nt(ec, cvt))
            o = oe[:, :LANE] * (1.0 / (oe[:, LANE:] + jnp.exp2(sink - m)))
            o_by_kv.append([o[gq * BLOCK:(gq + 1) * BLOCK] for gq in range(WIN_GROUP)])
        for s_idx, slab in enumerate(_win_place(o_by_kv, lane_b)):
            y_ref[n * BLOCK:(n + 1) * BLOCK, s_idx * LANE:(s_idx + 1) * LANE] = slab.astype(BF16)


def _win_call(h, w_win, layer, sink, gq, gk, group, seq, rope=None, cache=None, states=(), name="win"):
    nt, tm = h.shape[0], h.shape[1]
    qw = WIN_Q_HEADS * HEAD_DIM
    kw = WIN_KV_HEADS * HEAD_DIM
    assert WIN_KV_HEADS * HEAD_DIM == LANE and WIN_GROUP * LANE == qw
    const2 = lambda a: pl.BlockSpec(a.shape, lambda i: (0, 0))
    in_specs = [
        pl.BlockSpec(memory_space=pltpu.SMEM),
        pl.BlockSpec((None, tm, D_MODEL), lambda i: (i, 0, 0)),
        _resident((None, D_MODEL, qw + 2 * kw), lambda i: (layer, 0, 0)),
        const2(gq), const2(gk),
    ]
    args = [sink, h, w_win, gq, gk]
    y_spec = pl.BlockSpec((None, tm, qw), lambda i: (i, 0, 0))
    y_shape = jax.ShapeDtypeStruct((nt, tm, qw), BF16)
    scratch = [pltpu.VMEM((D_MODEL, qw + 2 * kw), BF16)]
    aliases = {}
    if cache is None:
        past = 0
        kv_spec = _state_spec(layer, group, (seq, kw))
        kv_shape = jax.ShapeDtypeStruct((nt * group, DEPTH, seq, kw), F32)
        out_specs = [y_spec, kv_spec, kv_spec]
        out_shape = [y_shape, kv_shape, kv_shape]
        aliases = {len(args) + n: 1 + n for n in range(len(states))}
        in_specs += [pl.BlockSpec(memory_space=pl.ANY)] * len(states)
        args += list(states)
    else:
        cos, sin = rope
        ck, cv = cache
        past = ck.shape[3]
        cspec = pl.BlockSpec((None, None, kw, past), lambda i: (i, layer, 0, 0))
        in_specs += [const2(cos), const2(sin), cspec, cspec]
        args += [cos, sin, ck, cv]
        out_specs = [y_spec]
        out_shape = [y_shape]
        scratch += [pltpu.VMEM((seq + 2 * BLOCK, kw), BF16), pltpu.VMEM((seq + 2 * BLOCK, 2 * kw), BF16)]
    return pl.pallas_call(
        functools.partial(_win_kernel, group=group, seq=seq, past=past, layer=layer),
        grid=(nt,),
        in_specs=in_specs,
        out_specs=out_specs,
        out_shape=out_shape,
        scratch_shapes=scratch,
        input_output_aliases=aliases,
        compiler_params=_cparams(("arbitrary",)),
        name=name,
    )(*args)


def _ret_kernel(*refs, group, seq, has_state, layer):
    if has_state:
        h_ref, wa_ref, wb_ref, df_ref, db_ref, s0f_ref, s0b_ref, y_ref, w_s, dec_s = refs
    else:
        h_ref, wa_ref, wb_ref, df_ref, db_ref = refs[:5]
        y_ref, sf_ref, sb_ref, w_s, dec_s = refs[-5:]
    qk_w = RET_HEADS * RET_DK
    v_w = RET_HEADS * RET_DV
    _cast_weights_once((wa_ref, wb_ref), w_s)
    proj = jnp.dot(h_ref[...], w_s[...], preferred_element_type=F32)
    q = proj[:, :qk_w]
    k = proj[:, qk_w:2 * qk_w] * (RET_DK ** -0.5)
    v = proj[:, 2 * qk_w:2 * qk_w + v_w]
    rg = proj[:, 2 * qk_w + v_w:]
    lgf_all = _log_gamma(df_ref[...])
    lgb_all = _log_gamma(db_ref[...])
    ch = min(seq, ATT_Q_TILE)
    n_ch = seq // ch

    @pl.when(pl.program_id(0) == 0)
    def _():
        rel = (_row((ch, ch)) - _lane((ch, ch))).astype(F32)
        for hd in range(RET_HEADS):
            lgf = lgf_all[hd:hd + 1, 0:1]
            lgb = lgb_all[hd:hd + 1, 0:1]
            dec_s[hd] = (jnp.where(rel >= 0, jnp.exp(jnp.maximum(rel, 0.0) * lgf), 0.0)
                         + jnp.where(rel <= 0, jnp.exp(jnp.maximum(-rel, 0.0) * lgb), 0.0))

    lane = _lane((ch, LANE))
    i_loc = _row((ch, LANE)).astype(F32)
    for hd in range(RET_HEADS):
        slab = slice((hd // 2) * LANE, (hd // 2 + 1) * LANE)
        vsl = slice(hd * RET_DV, (hd + 1) * RET_DV)
        half = hd % 2
        keep = lane >= RET_DK if half == 1 else lane < RET_DK
        lgf = lgf_all[hd:hd + 1, 0:1]
        lgb = lgb_all[hd:hd + 1, 0:1]
        q_in_f = jnp.exp((i_loc + 1.0) * lgf)
        q_in_b = jnp.exp((ch - i_loc) * lgb)
        k_out_f = jnp.exp((ch - 1.0 - i_loc) * lgf)
        k_out_b = jnp.exp(i_loc * lgb)
        carry_f = jnp.exp(ch * lgf)
        carry_b = jnp.exp(ch * lgb)
        dec = dec_s[hd]
        for g in range(group):
            rows = [slice(g * seq + c * ch, g * seq + (c + 1) * ch) for c in range(n_ch)]
            qm = [jnp.where(keep, q[r, slab], 0.0) for r in rows]
            kc = [k[r, slab] for r in rows]
            vb = [v[r, vsl].astype(BF16) for r in rows]
            o_state = [None] * n_ch
            for q_in, k_out, carry, s_ref, order in (
                    (q_in_f, k_out_f, carry_f, s0f_ref if has_state else sf_ref, range(n_ch)),
                    (q_in_b, k_out_b, carry_b, s0b_ref if has_state else sb_ref, range(n_ch - 1, -1, -1))):
                state = s_ref[hd // 2] if has_state else None
                for pos, c in enumerate(order):
                    if state is not None:
                        term = _bdot(qm[c] * q_in, state)
                        o_state[c] = term if o_state[c] is None else o_state[c] + term
                    if pos + 1 < n_ch or not has_state:
                        update = _bdot((kc[c] * k_out).T, vb[c])
                        state = update if state is None else state * carry + update
                if not has_state:
                    _store_state(s_ref, g, layer, state[half * RET_DK:(half + 1) * RET_DK], hd)
            for c, r in enumerate(rows):
                s = _bdot_nt(qm[c], kc[c])
                o = jnp.dot((s * dec).astype(BF16), vb[c], preferred_element_type=F32)
                if o_state[c] is not None:
                    o = o + o_state[c]
                y = _rms(o, None) * _silu(rg[r, vsl])
                y_ref[r, vsl] = y.astype(BF16)


def _ret_call(h, w_in, layer, dec_f, dec_b, group, seq, state=None, states=(), name="ret"):
    nt, tm = h.shape[0], h.shape[1]
    v_w = RET_HEADS * RET_DV
    ret_cols = COL_GATE - COL_RET
    const2 = lambda a: pl.BlockSpec(a.shape, lambda i: (0, 0))
    in_specs = [
        pl.BlockSpec((None, tm, D_MODEL), lambda i: (i, 0, 0)),
        _w_in_spec(layer, COL_RET, ret_cols // 2),
        _w_in_spec(layer, COL_RET + ret_cols // 2, ret_cols // 2),
        const2(dec_f), const2(dec_b),
    ]
    args = [h, w_in, w_in, dec_f, dec_b]
    y_spec = pl.BlockSpec((None, tm, v_w), lambda i: (i, 0, 0))
    y_shape = jax.ShapeDtypeStruct((nt, tm, v_w), BF16)
    aliases = {}
    if state is None:
        s_spec = _state_spec(layer, group, (RET_HEADS, RET_DK, RET_DV))
        s_shape = jax.ShapeDtypeStruct((nt * group, DEPTH, RET_HEADS, RET_DK, RET_DV), F32)
        out_specs = [y_spec, s_spec, s_spec]
        out_shape = [y_shape, s_shape, s_shape]
        aliases = {len(args) + n: 1 + n for n in range(len(states))}
        in_specs += [pl.BlockSpec(memory_space=pl.ANY)] * len(states)
        args += list(states)
    else:
        s0f, s0b = state
        sspec = pl.BlockSpec((None, None, 2, LANE, RET_DV), lambda i: (i, layer, 0, 0, 0))
        in_specs += [sspec, sspec]
        args += [s0f, s0b]
        out_specs = [y_spec]
        out_shape = [y_shape]
    return pl.pallas_call(
        functools.partial(_ret_kernel, group=group, seq=seq, has_state=state is not None, layer=layer),
        grid=(nt,),
        in_specs=in_specs,
        out_specs=out_specs,
        out_shape=out_shape,
        input_output_aliases=aliases,
        scratch_shapes=[pltpu.VMEM((D_MODEL, ret_cols), BF16),
                        pltpu.VMEM((RET_HEADS, min(seq, ATT_Q_TILE), min(seq, ATT_Q_TILE)), F32)],
        compiler_params=_cparams(("arbitrary",)),
        name=name,
    )(*args)


def _merge_ffn_kernel(x_ref, h_ref, mod_ref, gain_ref, y0_ref, y1_ref, y2_ref, y3_ref,
                      wg_ref, wb_ref, wo_ref, wi_ref, wo2_ref, o_ref, merged_s, act_s):
    m = mod_ref[...]
    for rows in _row_groups(x_ref.shape[0]):
        h = h_ref[rows, :]
        ys = [r[rows, :] for r in (y0_ref, y1_ref, y2_ref, y3_ref)]
        for c in range(D_MODEL // MERGE_TILE):
            cols = slice(c * MERGE_TILE, (c + 1) * MERGE_TILE)
            acc = None
            for b in range(N_BRANCH):
                gate = jax.nn.sigmoid(jnp.dot(
                    h, wg_ref[:, b * D_MODEL + c * MERGE_TILE:b * D_MODEL + (c + 1) * MERGE_TILE],
                    preferred_element_type=F32))
                part = gate * jnp.dot(ys[b], wb_ref[b, :, cols], preferred_element_type=F32)
                acc = part if acc is None else acc + part
            merged_s[rows, cols] = acc.astype(BF16)
        mixed = jnp.dot(merged_s[rows, :], wo_ref[...], preferred_element_type=F32)
        x = x_ref[rows, :] + m[5:6] * mixed
        o_ref[rows, :] = _swiglu_rows(x, m, 6, gain_ref[...], wi_ref, wo2_ref, act_s, rows)


def _merge_ffn_call(x, h, mod, mod_row, gain, ys, w_gate, w_branch, w_out, w_ff_in, w_ff_out, layer, name):
    nt, tm, _ = x.shape
    tok = pl.BlockSpec((None, tm, D_MODEL), lambda i: (i, 0, 0))
    ysp = pl.BlockSpec((None, tm, BRANCH_W), lambda i: (i, 0, 0))
    return pl.pallas_call(
        _merge_ffn_kernel,
        grid=(nt,),
        in_specs=[
            tok, tok,
            pl.BlockSpec((None, N_MOD, D_MODEL), lambda i: (mod_row(i), 0, 0)),
            pl.BlockSpec((1, D_MODEL), lambda i: (0, 0)),
            ysp, ysp, ysp, ysp,
            _resident((None, D_MODEL, N_BRANCH * D_MODEL), lambda i: (layer, 0, 0)),
            _resident((None, N_BRANCH, BRANCH_W, D_MODEL), lambda i: (layer, 0, 0, 0)),
            _resident((None, D_MODEL, D_MODEL), lambda i: (layer, 0, 0)),
            _resident((None, D_MODEL, 2 * D_FF), lambda i: (layer, 0, 0)),
            _resident((None, D_FF, D_MODEL), lambda i: (layer, 0, 0)),
        ],
        out_specs=tok,
        out_shape=jax.ShapeDtypeStruct(x.shape, F32),
        scratch_shapes=[pltpu.VMEM((tm, D_MODEL), BF16), pltpu.VMEM((tm, D_FF), BF16)],
        compiler_params=_cparams(("parallel",)),
        name=name,
    )(x, h, mod, gain, *ys, w_gate, w_branch, w_out, w_ff_in, w_ff_out)


def _dft_tables(seq):
    n_fft = 2 * seq
    idx = np.arange(seq, dtype=np.int64)
    ang = 2.0 * np.pi * ((idx[:, None] * idx[None, :]) % n_fft).astype(np.float64) / n_fft
    out = []
    for m in (np.cos(ang), np.sin(ang)):
        hi = jnp.asarray(m, F32).astype(BF16)
        lo = (jnp.asarray(m, F32) - hi.astype(F32)).astype(BF16)
        out += [hi, lo]
    return out


def _hyena_tables(seq):
    t = np.arange(seq, dtype=np.float32) / np.float32(seq)
    f = np.arange(1, HY_BANDS + 1, dtype=np.float32)
    ang = np.float32(2.0 * math.pi) * t[:, None] * f[None, :]
    feats = np.zeros((seq, LANE), np.float32)
    feats[:, 0] = t
    feats[:, 1:1 + HY_BANDS] = np.sin(ang)
    feats[:, 1 + HY_BANDS:HY_EMB] = np.cos(ang)
    min_decay = math.log(HY_TARGET) / HY_DECAY_LONG_PCT
    max_decay = math.log(HY_TARGET) / HY_DECAY_SHORT_PCT
    deltas = np.linspace(min_decay, max_decay, HY_W, dtype=np.float32)
    window = np.exp(-t[:, None] * np.abs(deltas)[None, :]).astype(np.float32)
    return jnp.asarray(feats), jnp.asarray(window)


def _rope_tables(seq):
    pos = np.arange(seq)
    row = (pos // GRID_W).astype(np.float32)
    col = (pos % GRID_W).astype(np.float32)
    inv_freq = (np.float32(ROPE_BASE) ** (-np.arange(ROPE_PAIRS, dtype=np.float32) / np.float32(ROPE_PAIRS)))
    lane = np.arange(LANE)
    in_head = lane % HEAD_DIM
    use_col = in_head >= HEAD_DIM // 2
    pair = in_head % ROPE_PAIRS
    second = (in_head % (2 * ROPE_PAIRS)) >= ROPE_PAIRS
    p = np.where(use_col[None, :], col[:, None], row[:, None]).astype(np.float32)
    ang = (p * inv_freq[pair][None, :]).astype(np.float32)
    cos = np.cos(ang).astype(np.float32)
    sin = np.sin(ang).astype(np.float32)
    sin = np.where(second[None, :], sin, -sin)
    return jnp.asarray(cos), jnp.asarray(sin)


def kernel(x_prompt, x_sample, c, cache_diff_k, cache_diff_v, cache_win_k, cache_win_v, state_ret_f, state_ret_b, c_ctx, norm_ffa, norm_mix, norm_ffb, w_ada, b_ada, w_ffa_in, w_ffa_out, w_ffb_in, w_ffb_out, w_in, hy_conv_w, hy_conv_b, hy_f_w1, hy_f_b1, hy_f_w2, hy_f_b2, hy_f_w3, hy_skip, diff_q_norm, diff_k_norm, diff_lambda, diff_subln, win_q_norm, win_k_norm, win_sink, ret_decay_f, ret_decay_b, w_branch, w_out):
    batch, seq, _ = x_prompt.shape
    dec_batch, dec_seq, _ = x_sample.shape
    past = cache_diff_k.shape[2]
    ctx_group = TOKEN_TILE // seq
    assert TOKEN_TILE % seq == 0 and batch % ctx_group == 0 and dec_seq == TOKEN_TILE
    assert 1 + dec_batch <= ADA_ROWS

    cond = jnp.zeros((ADA_ROWS, D_MODEL), F32).at[0].set(c_ctx).at[1:1 + dec_batch].set(c)
    mod = _ada_call(cond, w_ada, b_ada).reshape(DEPTH, ADA_ROWS, N_MOD, D_MODEL)

    bf = lambda a: a.astype(BF16)
    w_ffa_in_b, w_ffa_out_b, w_ffb_in_b, w_ffb_out_b = bf(w_ffa_in), bf(w_ffa_out), bf(w_ffb_in), bf(w_ffb_out)
    w_gate = bf(w_in[:, :, COL_GATE:])
    qw = WIN_Q_HEADS * HEAD_DIM
    slab_order = lambda a, axis: jnp.swapaxes(
        a.reshape(a.shape[:axis] + (WIN_KV_HEADS, WIN_GROUP, HEAD_DIM) + a.shape[axis + 1:]), axis, axis + 1
    ).reshape(a.shape)
    w_win = jnp.concatenate([slab_order(w_in[:, :, COL_WIN:COL_WIN + qw], 2),
                             w_in[:, :, COL_WIN + qw:COL_RET]], axis=2)
    win_branch = 2
    w_branch_b = bf(w_branch.at[:, win_branch].set(slab_order(w_branch[:, win_branch], 1)))
    w_out_b = bf(w_out)

    rope = _rope_tables(dec_seq)
    tables = {s: (_hyena_tables(s), _dft_tables(s)) for s in (seq, dec_seq)}

    ck_d = jnp.transpose(cache_diff_k, (0, 1, 3, 4, 5, 2)).reshape(dec_batch, DEPTH, -1, past)
    cv_d = cache_diff_v.reshape(dec_batch, DEPTH, past * DIFF_HEADS, 2 * HEAD_DIM)
    ck_w = jnp.transpose(cache_win_k, (0, 1, 3, 4, 2)).reshape(dec_batch, DEPTH, -1, past)
    cv_w = jnp.transpose(cache_win_v, (0, 1, 3, 4, 2)).reshape(dec_batch, DEPTH, -1, past)
    s0_f = state_ret_f.reshape(dec_batch, DEPTH, 2, LANE, RET_DV)
    s0_b = state_ret_b.reshape(dec_batch, DEPTH, 2, LANE, RET_DV)

    pad_rows = lambda a, n: jnp.pad(a, ((0, n - a.shape[0]), (0, 0)))
    tile_lanes = lambda a, n: jnp.tile(a.reshape(1, -1), (1, n))

    tiled = lambda a, tile: a.reshape(-1, tile, a.shape[-1])
    y_ctx = x_prompt
    y_lat = x_sample
    ctx_row = lambda tile: (lambda i: 0)
    lat_row = lambda tile: (lambda i: i // (dec_seq // tile) + 1)
    st_diff, st_win, st_ret = (), (), ()

    for l in range(DEPTH):
        lam_init = 0.8 - 0.6 * math.exp(-0.3 * l)
        gain = lambda a: a[l].reshape(1, -1)
        gq_d, gk_d = tile_lanes(diff_q_norm[l], 2 * DIFF_HEADS), tile_lanes(diff_k_norm[l], 2 * DIFF_HEADS)
        gs_d = tile_lanes(diff_subln[l], DIFF_HEADS)
        gq_w, gk_w = tile_lanes(win_q_norm[l], WIN_Q_HEADS), tile_lanes(win_k_norm[l], WIN_KV_HEADS)
        dec_f = jnp.broadcast_to(ret_decay_f[l][:, None], (RET_HEADS, LANE))
        dec_b = jnp.broadcast_to(ret_decay_b[l][:, None], (RET_HEADS, LANE))
        w1 = jnp.pad(hy_f_w1[l], ((0, LANE - HY_EMB), (0, LANE - HY_FH)))
        b1 = jnp.pad(hy_f_b1[l].reshape(1, -1), ((0, 0), (0, LANE - HY_FH)))
        w2 = jnp.pad(hy_f_w2[l], ((0, LANE - HY_FH), (0, LANE - HY_FH)))
        b2 = jnp.pad(hy_f_b2[l].reshape(1, -1), ((0, 0), (0, LANE - HY_FH)))
        w3 = pad_rows(hy_f_w3[l], LANE)
        cw, cb, skip = hy_conv_w[l], hy_conv_b[l].reshape(1, -1), hy_skip[l].reshape(1, -1)

        for is_lat in (False, True):
            x = y_lat if is_lat else y_ctx
            s_len = dec_seq if is_lat else seq
            group = 1 if is_lat else ctx_group
            mod_row = lat_row if is_lat else ctx_row
            tag = f"{'lat' if is_lat else 'ctx'}{l}"
            (feats, window), (c_hi, c_lo, s_hi, s_lo) = tables[s_len]

            x, h = _ffn_a_call(tiled(x, FFN_A_TILE), mod[l], mod_row(FFN_A_TILE), gain(norm_ffa), gain(norm_mix),
                               w_ffa_in_b, w_ffa_out_b, l, f"ffn_a_{tag}")
            h = tiled(h, TOKEN_TILE)

            kr, ki, kn = _hy_filter_call(s_len, feats, w1, b1, w2, b2, w3, window, c_hi, c_lo, s_hi, s_lo)
            y_hy = _hyena_call(h, w_in, l, cw, cb, kr, ki, kn, skip, c_hi, s_hi, group, s_len, f"hyena_{tag}")
            if is_lat:
                (y_diff,) = _diff_call(h, w_in, l, gq_d, gk_d, diff_lambda[l], gs_d, lam_init,
                                       group, s_len, rope=rope, cache=(ck_d, cv_d), name=f"diff_{tag}")
                (y_win,) = _win_call(h, w_win, l, win_sink, gq_w, gk_w, group, s_len,
                                     rope=rope, cache=(ck_w, cv_w), name=f"win_{tag}")
                (y_ret,) = _ret_call(h, w_in, l, dec_f, dec_b, group, s_len, state=(s0_f, s0_b),
                                     name=f"ret_{tag}")
            else:
                y_diff, *st_diff = _diff_call(h, w_in, l, gq_d, gk_d, diff_lambda[l], gs_d, lam_init,
                                              group, s_len, states=st_diff, name=f"diff_{tag}")
                y_win, *st_win = _win_call(h, w_win, l, win_sink, gq_w, gk_w, group, s_len,
                                           states=st_win, name=f"win_{tag}")
                y_ret, *st_ret = _ret_call(h, w_in, l, dec_f, dec_b, group, s_len, states=st_ret,
                                           name=f"ret_{tag}")

            x = _merge_ffn_call(tiled(x, MERGE_FFN_TILE), tiled(h, MERGE_FFN_TILE), mod[l], mod_row(MERGE_FFN_TILE),
                                gain(norm_ffb),
                                tuple(tiled(y, MERGE_FFN_TILE) for y in (y_hy, y_diff, y_win, y_ret)),
                                w_gate, w_branch_b, w_out_b, w_ffb_in_b, w_ffb_out_b, l, f"merge_ffn_b_{tag}")
            if is_lat:
                y_lat = x
            else:
                y_ctx = x

    return (y_ctx.reshape(batch, seq, D_MODEL), y_lat.reshape(dec_batch, dec_seq, D_MODEL),
            st_diff[0].reshape(batch, DEPTH, seq, DIFF_HEADS, 2, HEAD_DIM),
            st_diff[1].reshape(batch, DEPTH, seq, DIFF_HEADS, 2 * HEAD_DIM),
            st_win[0].reshape(batch, DEPTH, seq, WIN_KV_HEADS, HEAD_DIM),
            st_win[1].reshape(batch, DEPTH, seq, WIN_KV_HEADS, HEAD_DIM),
            st_ret[0], st_ret[1])
```

```python
import functools
import math

import jax
import jax.numpy as jnp
import numpy as np
from jax import lax
from jax.experimental import pallas as pl
from jax.experimental.pallas import tpu as pltpu

F32 = jnp.float32
BF16 = jnp.bfloat16

D_MODEL = 1024
DEPTH = 2
GRID_W = 64
HEAD_DIM = 64
ROPE_PAIRS = HEAD_DIM // 4
ROPE_BASE = 10000.0
EPS = 1e-6
NEG_INF = -1e30
LOG2_E = math.log2(math.e)
D_FF = 2816
N_MOD = 9
N_BRANCH = 4
WIN_BRANCH = 2
BRANCH_W = 512

HY_W = BRANCH_W
HY_BANDS = 16
HY_EMB = 1 + 2 * HY_BANDS
HY_FH = 64
HY_SIN_W = 1.0
HY_TARGET = 1e-2
HY_DECAY_SHORT_PCT = 0.3
HY_DECAY_LONG_PCT = 1.5

DIFF_HEADS = 4
WIN_Q_HEADS = 8
WIN_KV_HEADS = 2
WIN_GROUP = WIN_Q_HEADS // WIN_KV_HEADS
WINDOW = 128
BLOCK = 128
RET_HEADS = 4
RET_DK = 64
RET_DV = 128

COL_HY = 0
COL_DIFF = 3 * HY_W
COL_WIN = COL_DIFF + 3 * (2 * DIFF_HEADS * HEAD_DIM)
COL_RET = COL_WIN + (WIN_Q_HEADS + 2 * WIN_KV_HEADS) * HEAD_DIM
COL_GATE = COL_RET + 2 * RET_HEADS * (RET_DK + RET_DV)
IN_COLS = COL_GATE + N_BRANCH * D_MODEL

LANE = 128
TOKEN_TILE = 1024
FFN_A_TILE = 1024
MERGE_FFN_TILE = 512
FF_TILE = 256
ROW_GROUP = 256
MERGE_TILE = 256
ADA_TILE = 2304
ADA_ROWS = 16
ATT_Q_TILE = 256
VMEM_LIMIT = 56 * 1024 * 1024


def _cparams(sem):
    return pltpu.CompilerParams(dimension_semantics=sem, vmem_limit_bytes=VMEM_LIMIT)


def _bdot(a, b):
    return jnp.dot(a.astype(BF16), b.astype(BF16), preferred_element_type=F32)


def _bdot_nt(a, b):
    return lax.dot_general(a.astype(BF16), b.astype(BF16), (((1,), (1,)), ((), ())),
                           preferred_element_type=F32)


def _split(a):
    hi = a.astype(BF16)
    lo = (a - hi.astype(F32)).astype(BF16)
    return hi, lo


def _dot3_pre(ah, al, b):
    bh, bl = _split(b)
    d = functools.partial(jnp.dot, preferred_element_type=F32)
    return d(ah, bh) + d(ah, bl) + d(al, bh)


def _dot3(a, b):
    return _dot3_pre(*_split(a), b)


def _rms_heads(x, gain):
    first = _lane((x.shape[0], LANE)) < HEAD_DIM
    out = []
    for s in range(x.shape[1] // LANE):
        xs = x[:, s * LANE:(s + 1) * LANE]
        sq = xs * xs
        lo = jnp.sum(jnp.where(first, sq, 0.0), axis=-1, keepdims=True)
        hi = jnp.sum(jnp.where(first, 0.0, sq), axis=-1, keepdims=True)
        r_lo = lax.rsqrt(lo * (1.0 / HEAD_DIM) + EPS)
        r_hi = lax.rsqrt(hi * (1.0 / HEAD_DIM) + EPS)
        out.append(xs * jnp.where(first, r_lo, r_hi))
    return jnp.concatenate(out, axis=1) * gain


def _rms(x, gain):
    y = x * lax.rsqrt(jnp.mean(x * x, axis=-1, keepdims=True) + EPS)
    return y if gain is None else y * gain


def _silu(x):
    return x * jax.nn.sigmoid(x)


def _exp2_bf16(x):
    return jnp.exp2(x).astype(BF16)


def _lane(shape):
    return lax.broadcasted_iota(jnp.int32, shape, 1)


def _row(shape):
    return lax.broadcasted_iota(jnp.int32, shape, 0)


def _rope128(x, cos, sin):
    partner = jnp.where((_lane(x.shape) & 31) < 16,
                        pltpu.roll(x, LANE - ROPE_PAIRS, axis=1),
                        pltpu.roll(x, ROPE_PAIRS, axis=1))
    return x * cos + partner * sin


def _cast_weights_once(w_refs, w_s):
    @pl.when(pl.program_id(0) == 0)
    def _():
        off = 0
        for r in w_refs:
            w_s[:, off:off + r.shape[1]] = r[...].astype(BF16)
            off += r.shape[1]


def _w_in_spec(layer, col, width):
    assert col % width == 0
    return pl.BlockSpec((None, D_MODEL, width), lambda i: (layer, 0, col // width),
                        pipeline_mode=pl.Buffered(1))


def _state_spec(layer, group, tail):
    zeros = (0,) * len(tail)
    if layer == 0:
        return pl.BlockSpec((group, DEPTH) + tail, lambda i: (i, 0) + zeros)
    return pl.BlockSpec((group, None) + tail, lambda i: (i, layer) + zeros)


def _store_state(ref, g, layer, value, *sub):
    if layer == 0:
        ref[(g, 0) + sub] = value
        for later in range(1, DEPTH):
            ref[(g, later) + sub] = jnp.zeros_like(value)
    else:
        ref[(g,) + sub] = value


def _log_gamma(decay):
    x = -decay
    return -(jnp.maximum(x, 0.0) + jnp.log1p(jnp.exp(-jnp.abs(x))))


def _ada_kernel(cond_ref, w_ref, b_ref, o_ref):
    s = _silu(cond_ref[...])
    o_ref[...] = _bdot(s, w_ref[...]) + b_ref[...]


def _ada_call(cond, w_ada, b_ada):
    rows = cond.shape[0]
    n_out = N_MOD * D_MODEL
    return pl.pallas_call(
        _ada_kernel,
        grid=(DEPTH, n_out // ADA_TILE),
        in_specs=[
            pl.BlockSpec((rows, D_MODEL), lambda l, j: (0, 0)),
            pl.BlockSpec((None, D_MODEL, ADA_TILE), lambda l, j: (l, 0, j)),
            pl.BlockSpec((None, 1, ADA_TILE), lambda l, j: (l, 0, j)),
        ],
        out_specs=pl.BlockSpec((None, rows, ADA_TILE), lambda l, j: (l, 0, j)),
        out_shape=jax.ShapeDtypeStruct((DEPTH, rows, n_out), F32),
        compiler_params=_cparams(("parallel", "parallel")),
        name="ada_mod",
    )(cond, w_ada, b_ada.reshape(DEPTH, 1, n_out))


def _swiglu_rows(x, m, mod_base, gain, wi_ref, wo_ref, act_s, rows):
    h = (_rms(x, gain) * (1.0 + m[mod_base + 1:mod_base + 2]) + m[mod_base:mod_base + 1]).astype(BF16)
    for c in range(D_FF // FF_TILE):
        a = jnp.dot(h, wi_ref[:, c * FF_TILE:(c + 1) * FF_TILE], preferred_element_type=F32)
        g = jnp.dot(h, wi_ref[:, D_FF + c * FF_TILE:D_FF + (c + 1) * FF_TILE], preferred_element_type=F32)
        act_s[rows, c * FF_TILE:(c + 1) * FF_TILE] = (_silu(a) * g).astype(BF16)
    out = jnp.dot(act_s[rows, :], wo_ref[...], preferred_element_type=F32)
    return x + 0.5 * m[mod_base + 2:mod_base + 3] * out


def _row_groups(n_rows):
    return [slice(p * ROW_GROUP, (p + 1) * ROW_GROUP) for p in range(n_rows // ROW_GROUP)]


def _ffn_a_kernel(x_ref, mod_ref, gain_ref, gain2_ref, wi_ref, wo_ref, y_ref, h2_ref, act_s):
    m = mod_ref[...]
    for rows in _row_groups(x_ref.shape[0]):
        y = _swiglu_rows(x_ref[rows, :], m, 0, gain_ref[...], wi_ref, wo_ref, act_s, rows)
        y_ref[rows, :] = y
        h2_ref[rows, :] = (_rms(y, gain2_ref[...]) * (1.0 + m[4:5]) + m[3:4]).astype(BF16)


def _resident(block_shape, index_map):
    return pl.BlockSpec(block_shape, index_map, pipeline_mode=pl.Buffered(1))


def _ffn_a_call(x, mod, mod_row, gain, gain2, w_in, w_out, layer, name):
    nt, tm, _ = x.shape
    tok = pl.BlockSpec((None, tm, D_MODEL), lambda i: (i, 0, 0))
    vec = pl.BlockSpec((1, D_MODEL), lambda i: (0, 0))
    return pl.pallas_call(
        _ffn_a_kernel,
        grid=(nt,),
        in_specs=[
            tok, pl.BlockSpec((None, N_MOD, D_MODEL), lambda i: (mod_row(i), 0, 0)), vec, vec,
            _resident((None, D_MODEL, 2 * D_FF), lambda i: (layer, 0, 0)),
            _resident((None, D_FF, D_MODEL), lambda i: (layer, 0, 0)),
        ],
        out_specs=[tok, tok],
        out_shape=[jax.ShapeDtypeStruct(x.shape, F32), jax.ShapeDtypeStruct(x.shape, BF16)],
        scratch_shapes=[pltpu.VMEM((tm, D_FF), BF16)],
        compiler_params=_cparams(("parallel",)),
        name=name,
    )(x, mod, gain, gain2, w_in, w_out)


def _hy_filter_kernel(feats_ref, w1_ref, b1_ref, w2_ref, b2_ref, w3_ref, win_ref,
                      ch_ref, cl_ref, sh_ref, sl_ref, kr_ref, ki_ref, kn_ref, *, seq):
    n_fft = 2 * seq
    z = jnp.sin(HY_SIN_W * (_dot3(feats_ref[...], w1_ref[...]) + b1_ref[...]))
    z = jnp.sin(HY_SIN_W * (_dot3(z, w2_ref[...]) + b2_ref[...]))
    zz = _dot3(z, w3_ref[...])
    win = win_ref[...]
    hf = zz[:, :HY_W] * win
    hb = zz[:, HY_W:] * win
    norm = (jnp.sum(jnp.abs(hf), axis=0, keepdims=True)
            + jnp.sum(jnp.abs(hb), axis=0, keepdims=True))
    hf = hf / norm
    hb = hb / norm
    row = _row(hf.shape)
    hb0 = jnp.where(row == 0, 0.0, hb)
    even = hf + hb0
    odd = hb0 - hf
    wk = jnp.where(row == 0, 1.0 / n_fft, 2.0 / n_fft)
    kr_ref[...] = _dot3_pre(ch_ref[...], cl_ref[...], even) * wk
    ki_ref[...] = _dot3_pre(sh_ref[...], sl_ref[...], odd) * wk
    sgn = jnp.where((row & 1) == 0, 1.0, -1.0)
    kn_ref[...] = jnp.sum(even * sgn, axis=0, keepdims=True) * (1.0 / n_fft)


def _hy_filter_call(seq, feats, w1, b1, w2, b2, w3, win, ch, cl, sh, sl):
    args = (feats, w1, b1, w2, b2, w3, win, ch, cl, sh, sl)
    return pl.pallas_call(
        functools.partial(_hy_filter_kernel, seq=seq),
        out_shape=[jax.ShapeDtypeStruct((seq, HY_W), F32),
                   jax.ShapeDtypeStruct((seq, HY_W), F32),
                   jax.ShapeDtypeStruct((1, HY_W), F32)],
        compiler_params=pltpu.CompilerParams(vmem_limit_bytes=VMEM_LIMIT),
        name=f"hyena_filter_{seq}",
    )(*args)


def _hyena_kernel(h_ref, w_ref, cw_ref, cb_ref, kr_ref, ki_ref, kn_ref, skip_ref,
                  c_ref, s_ref, y_ref, w_s, *, group, seq):
    _cast_weights_once((w_ref,), w_s)
    proj = jnp.dot(h_ref[...], w_s[...], preferred_element_type=F32)
    cw = cw_ref[...]
    cmat = c_ref[...]
    smat = s_ref[...]
    kr = kr_ref[...]
    ki = ki_ref[...]
    row = _row((seq, 3 * HY_W))
    row_w = _row((seq, HY_W))
    sgn = jnp.where((row_w & 1) == 0, 1.0, -1.0)
    for g in range(group):
        hy = proj[g * seq:(g + 1) * seq]
        prev = jnp.where(row == 0, 0.0, pltpu.roll(hy, 1, axis=0))
        nxt = jnp.where(row == seq - 1, 0.0, pltpu.roll(hy, seq - 1, axis=0))
        u = prev * cw[0:1] + hy * cw[1:2] + nxt * cw[2:3] + cb_ref[...]
        v, x0, x1 = u[:, :HY_W], u[:, HY_W:2 * HY_W], u[:, 2 * HY_W:]
        z = v * x1
        zb = z.astype(BF16)
        zr = jnp.dot(cmat, zb, preferred_element_type=F32)
        zs = jnp.dot(smat, zb, preferred_element_type=F32)
        yr = (zr * kr + zs * ki).astype(BF16)
        yi = (zr * ki - zs * kr).astype(BF16)
        nyq = jnp.sum(z * sgn, axis=0, keepdims=True) * kn_ref[...]
        conv = (jnp.dot(cmat, yr, preferred_element_type=F32)
                - jnp.dot(smat, yi, preferred_element_type=F32) + sgn * nyq)
        y_ref[g * seq:(g + 1) * seq, :] = (x0 * (conv + skip_ref[...] * z)).astype(BF16)


def _hyena_call(h, w_in, layer, cw, cb, kr, ki, kn, skip, cmat, smat, group, seq, name):
    nt, tm, _ = h.shape
    const2 = lambda a: pl.BlockSpec(a.shape, lambda i: (0, 0))
    return pl.pallas_call(
        functools.partial(_hyena_kernel, group=group, seq=seq),
        grid=(nt,),
        in_specs=[
            pl.BlockSpec((None, tm, D_MODEL), lambda i: (i, 0, 0)),
            _w_in_spec(layer, COL_HY, 3 * HY_W),
            const2(cw), const2(cb), const2(kr), const2(ki), const2(kn), const2(skip),
            const2(cmat), const2(smat),
        ],
        out_specs=pl.BlockSpec((None, tm, HY_W), lambda i: (i, 0, 0)),
        out_shape=jax.ShapeDtypeStruct((nt, tm, HY_W), BF16),
        scratch_shapes=[pltpu.VMEM((D_MODEL, 3 * HY_W), BF16)],
        compiler_params=_cparams(("arbitrary",)),
        name=name,
    )(h, w_in, cw, cb, kr, ki, kn, skip, cmat, smat)


def _diff_kernel(*refs, group, seq, past, lam_init, layer):
    has_cache = past > 0
    if has_cache:
        (h_ref, w_ref, gq_ref, gk_ref, lam_ref, gs_ref, cos_ref, sin_ref,
         ckt_ref, cv_ref, y_ref, w_s, k_s, v_s) = refs
    else:
        h_ref, w_ref, gq_ref, gk_ref, lam_ref, gs_ref = refs[:6]
        y_ref, ko_ref, vo_ref, w_s = refs[-4:]
    width = 2 * DIFF_HEADS * HEAD_DIM
    _cast_weights_once((w_ref,), w_s)
    proj = jnp.dot(h_ref[...], w_s[...], preferred_element_type=F32)
    q, k, v = proj[:, :width], proj[:, width:2 * width], proj[:, 2 * width:]
    q = _rms_heads(q, gq_ref[...])
    k = _rms_heads(k, gk_ref[...])
    dl = lam_ref[...]
    lam = (jnp.exp(jnp.sum(dl[0:1] * dl[1:2], axis=1, keepdims=True))
           - jnp.exp(jnp.sum(dl[2:3] * dl[3:4], axis=1, keepdims=True)) + lam_init)
    ones = jnp.ones((seq, LANE), BF16)
    if has_cache:
        cos, sin = cos_ref[...], sin_ref[...]
        q = jnp.concatenate([_rope128(q[:, s * LANE:(s + 1) * LANE], cos, sin)
                             for s in range(width // LANE)], axis=1)
        k = jnp.concatenate([_rope128(k[:, s * LANE:(s + 1) * LANE], cos, sin)
                             for s in range(width // LANE)], axis=1)
        k_s[...] = k.astype(BF16)
        for hh in range(DIFF_HEADS):
            v_s[0:seq, 2 * hh * LANE:(2 * hh + 1) * LANE] = v[:, hh * LANE:(hh + 1) * LANE].astype(BF16)
            v_s[seq:seq + past, 2 * hh * LANE:(2 * hh + 1) * LANE] = (
                cv_ref[pl.ds(hh, past, stride=DIFF_HEADS), :].astype(BF16))
            v_s[:, (2 * hh + 1) * LANE:(2 * hh + 2) * LANE] = jnp.ones((seq + past, LANE), BF16)
    else:
        for g in range(group):
            rows = slice(g * seq, (g + 1) * seq)
            _store_state(ko_ref, g, layer, k[rows])
            for hh in range(DIFF_HEADS):
                _store_state(vo_ref, g, layer, v[rows, hh * LANE:(hh + 1) * LANE],
                             pl.ds(hh, seq, stride=DIFF_HEADS))
    q = q * (HEAD_DIM ** -0.5 * LOG2_E)
    tq = min(seq, ATT_Q_TILE)
    lane = _lane((tq, LANE))
    for g in range(group):
        for hh in range(DIFF_HEADS):
            sl = slice(hh * LANE, (hh + 1) * LANE)
            if has_cache:
                keys = k_s[:, sl]
                vals = v_s[0:seq, 2 * hh * LANE:(2 * hh + 2) * LANE]
                vals_c = v_s[seq:seq + past, 2 * hh * LANE:(2 * hh + 2) * LANE]
                keys_t = ckt_ref[sl, :].astype(BF16)
            else:
                keys = k[g * seq:(g + 1) * seq, sl].astype(BF16)
                vals = jnp.concatenate([v[g * seq:(g + 1) * seq, sl].astype(BF16), ones], axis=1)
            for qi in range(seq // tq):
                r0 = g * seq + qi * tq
                q128 = q[r0:r0 + tq, sl]
                qs = jnp.concatenate([jnp.where(lane < HEAD_DIM, q128, 0.0),
                                      jnp.where(lane >= HEAD_DIM, q128, 0.0)], axis=0).astype(BF16)
                s = _bdot_nt(qs, keys)
                if has_cache:
                    sc = jnp.dot(qs, keys_t, preferred_element_type=F32)
                    m = jnp.maximum(jnp.max(s, axis=-1, keepdims=True), jnp.max(sc, axis=-1, keepdims=True))
                    oe = (jnp.dot(_exp2_bf16(s - m), vals, preferred_element_type=F32)
                          + jnp.dot(_exp2_bf16(sc - m), vals_c, preferred_element_type=F32))
                else:
                    e = _exp2_bf16(s - jnp.max(s, axis=-1, keepdims=True))
                    oe = jnp.dot(e, vals, preferred_element_type=F32)
                r = 1.0 / oe[:, LANE:]
                o = oe[:tq, :LANE] * r[:tq] - oe[tq:, :LANE] * (lam * r[tq:])
                y = _rms(o, gs_ref[:, sl]) * (1.0 - lam_init)
                y_ref[r0:r0 + tq, sl] = y.astype(BF16)


def _diff_call(h, w_in, layer, gq, gk, lam, gs, lam_init, group, seq,
               rope=None, cache=None, states=(), name="diff"):
    nt, tm = h.shape[0], h.shape[1]
    width = 2 * DIFF_HEADS * HEAD_DIM
    const2 = lambda a: pl.BlockSpec(a.shape, lambda i: (0, 0))
    in_specs = [
        pl.BlockSpec((None, tm, D_MODEL), lambda i: (i, 0, 0)),
        _w_in_spec(layer, COL_DIFF, 3 * width),
        const2(gq), const2(gk), const2(lam), const2(gs),
    ]
    args = [h, w_in, gq, gk, lam, gs]
    y_spec = pl.BlockSpec((None, tm, width), lambda i: (i, 0, 0))
    y_shape = jax.ShapeDtypeStruct((nt, tm, width), BF16)
    scratch = [pltpu.VMEM((D_MODEL, 3 * width), BF16)]
    aliases = {}
    if cache is None:
        past = 0
        k_spec = _state_spec(layer, group, (seq, width))
        k_shape = jax.ShapeDtypeStruct((nt * group, DEPTH, seq, width), F32)
        v_spec = _state_spec(layer, group, (seq * DIFF_HEADS, 2 * HEAD_DIM))
        v_shape = jax.ShapeDtypeStruct((nt * group, DEPTH, seq * DIFF_HEADS, 2 * HEAD_DIM), F32)
        out_specs = [y_spec, k_spec, v_spec]
        out_shape = [y_shape, k_shape, v_shape]
        aliases = {len(args) + n: 1 + n for n in range(len(states))}
        in_specs += [pl.BlockSpec(memory_space=pl.ANY)] * len(states)
        args += list(states)
    else:
        cos, sin = rope
        ckt, cv = cache
        past = ckt.shape[3]
        in_specs += [const2(cos), const2(sin),
                     pl.BlockSpec((None, None, width, past), lambda i: (i, layer, 0, 0)),
                     pl.BlockSpec((None, None, past * DIFF_HEADS, 2 * HEAD_DIM), lambda i: (i, layer, 0, 0))]
        args += [cos, sin, ckt, cv]
        out_specs = [y_spec]
        out_shape = [y_shape]
        scratch += [pltpu.VMEM((seq, width), BF16), pltpu.VMEM((seq + past, 2 * width), BF16)]
    return pl.pallas_call(
        functools.partial(_diff_kernel, group=group, seq=seq, past=past, lam_init=lam_init, layer=layer),
        grid=(nt,),
        in_specs=in_specs,
        out_specs=out_specs,
        out_shape=out_shape,
        scratch_shapes=scratch,
        input_output_aliases=aliases,
        compiler_params=_cparams(("arbitrary",)),
        name=name,
    )(*args)


def _win_heads(q, hk, lane):
    keep = lane >= HEAD_DIM if hk == 1 else lane < HEAD_DIM
    return [jnp.where(keep, q[:, s * LANE:(s + 1) * LANE], 0.0) for s in range(WIN_GROUP)]


def _win_place(o_by_kv, lane):
    return [jnp.where(lane < HEAD_DIM, o_by_kv[0][s], o_by_kv[1][s]) for s in range(WIN_GROUP)]


def _sink_col(sink_ref, layer, hk, rows_per_head):
    rows = WIN_GROUP * rows_per_head
    r = _row((rows, 1))
    col = jnp.full((rows, 1), sink_ref[layer, hk * WIN_GROUP], F32)
    for gq in range(1, WIN_GROUP):
        col = jnp.where(r >= gq * rows_per_head, sink_ref[layer, hk * WIN_GROUP + gq], col)
    return col


def _win_kernel(*refs, group, seq, past, layer):
    has_cache = past > 0
    if has_cache:
        (sink_ref, h_ref, w_ref, gq_ref, gk_ref, cos_ref, sin_ref, ck_ref, cv_ref,
         y_ref, w_s, k_s, v_s) = refs
    else:
        sink_ref, h_ref, w_ref, gq_ref, gk_ref = refs[:5]
        y_ref, ko_ref, vo_ref, w_s = refs[-4:]
    qw = WIN_Q_HEADS * HEAD_DIM
    kw = WIN_KV_HEADS * HEAD_DIM

    @pl.when(pl.program_id(0) == 0)
    def _():
        for s in range(WIN_GROUP):
            first = w_ref[:, s * HEAD_DIM:(s + 1) * HEAD_DIM]
            second = w_ref[:, (WIN_GROUP + s) * HEAD_DIM:(WIN_GROUP + s + 1) * HEAD_DIM]
            w_s[:, s * LANE:(s + 1) * LANE] = jnp.concatenate([first, second], axis=1).astype(BF16)
        w_s[:, qw:] = w_ref[:, qw:].astype(BF16)

    proj = jnp.dot(h_ref[...], w_s[...], preferred_element_type=F32)
    q, k, v = proj[:, :qw], proj[:, qw:qw + kw], proj[:, qw + kw:]
    q = _rms_heads(q, gq_ref[...])
    k = _rms_heads(k, gk_ref[...])
    scale = HEAD_DIM ** -0.5 * LOG2_E
    if not has_cache:
        for g in range(group):
            _store_state(ko_ref, g, layer, k[g * seq:(g + 1) * seq])
            _store_state(vo_ref, g, layer, v[g * seq:(g + 1) * seq])
        q = q * scale
        lane = _lane((seq, LANE))
        ones = jnp.ones((seq, LANE), BF16)
        for g in range(group):
            rows = slice(g * seq, (g + 1) * seq)
            kb = k[rows].astype(BF16)
            vb = jnp.concatenate([v[rows].astype(BF16), ones], axis=1)
            qg = q[rows]
            o_by_kv = []
            for hk in range(WIN_KV_HEADS):
                qs = jnp.concatenate(_win_heads(qg, hk, lane), axis=0)
                s = _bdot_nt(qs, kb)
                sink = _sink_col(sink_ref, layer, hk, seq) * LOG2_E
                m = jnp.maximum(jnp.max(s, axis=-1, keepdims=True), sink)
                oe = jnp.dot(_exp2_bf16(s - m), vb, preferred_element_type=F32)
                o = oe[:, :LANE] * (1.0 / (oe[:, LANE:] + jnp.exp2(sink - m)))
                o_by_kv.append([o[gq * seq:(gq + 1) * seq] for gq in range(WIN_GROUP)])
            for s_idx, slab in enumerate(_win_place(o_by_kv, lane)):
                y_ref[rows, s_idx * LANE:(s_idx + 1) * LANE] = slab.astype(BF16)
        return

    cos, sin = cos_ref[...], sin_ref[...]
    q = jnp.concatenate([_rope128(q[:, s * LANE:(s + 1) * LANE], cos, sin)
                         for s in range(qw // LANE)], axis=1) * scale
    k = _rope128(k, cos, sin)
    zpad = jnp.zeros((BLOCK, kw), BF16)
    k_s[0:BLOCK, :] = zpad
    k_s[BLOCK:BLOCK + seq, :] = k.astype(BF16)
    k_s[BLOCK + seq:2 * BLOCK + seq, :] = zpad
    v_s[0:BLOCK, 0:kw] = zpad
    v_s[BLOCK:BLOCK + seq, 0:kw] = v.astype(BF16)
    v_s[BLOCK + seq:2 * BLOCK + seq, 0:kw] = zpad
    v_s[:, kw:2 * kw] = jnp.ones((seq + 2 * BLOCK, kw), BF16)
    ckt = ck_ref[...].astype(BF16)
    cvt = jnp.concatenate([cv_ref[...].astype(BF16), jnp.ones((kw, past), BF16)], axis=0)
    lane_l = _lane((seq, LANE))
    lane_b = _lane((BLOCK, LANE))
    nb = seq // BLOCK
    rows_q = WIN_GROUP * BLOCK
    kk = _lane((rows_q, 3 * BLOCK))
    qi = _row((rows_q, 3 * BLOCK)) & (BLOCK - 1)
    rel = kk - BLOCK - qi
    bias_mid = jnp.where(rel <= WINDOW, jnp.where(rel >= -WINDOW, 0.0, NEG_INF), NEG_INF)
    bias = {(False, False): bias_mid,
            (True, False): jnp.where(kk >= BLOCK, bias_mid, NEG_INF),
            (False, True): jnp.where(kk < 2 * BLOCK, bias_mid, NEG_INF)}
    bias[(True, True)] = jnp.where(kk < 2 * BLOCK, bias[(True, False)], NEG_INF)
    heads = [_win_heads(q, hk, lane_l) for hk in range(WIN_KV_HEADS)]
    sinks = [_sink_col(sink_ref, layer, hk, BLOCK) * LOG2_E for hk in range(WIN_KV_HEADS)]
    for n in range(nb):
        o_by_kv = []
        for hk in range(WIN_KV_HEADS):
            sink = sinks[hk]
            qs = jnp.concatenate([hd[n * BLOCK:(n + 1) * BLOCK] for hd in heads[hk]], axis=0)
            sb = _bdot_nt(qs, k_s[n * BLOCK:(n + 3) * BLOCK, :]) + bias[(n == 0, n == nb - 1)]
            sc = jnp.dot(qs.astype(BF16), ckt, preferred_element_type=F32)
            m = jnp.maximum(jnp.maximum(jnp.max(sb, axis=-1, keepdims=True),
                                        jnp.max(sc, axis=-1, keepdims=True)), sink)
            eb = _exp2_bf16(sb - m)
            ec = _exp2_bf16(sc - m)
            oe = (jnp.dot(eb, v_s[n * BLOCK:(n + 3) * BLOCK, :], preferred_element_type=F32)
                  + _bdot_nt(ec, cvt))
            o = oe[:, :LANE] * (1.0 / (oe[:, LANE:] + jnp.exp2(sink - m)))
            o_by_kv.append([o[gq * BLOCK:(gq + 1) * BLOCK] for gq in range(WIN_GROUP)])
        for s_idx, slab in enumerate(_win_place(o_by_kv, lane_b)):
            y_ref[n * BLOCK:(n + 1) * BLOCK, s_idx * LANE:(s_idx + 1) * LANE] = slab.astype(BF16)


def _win_call(h, w_in, layer, sink, gq, gk, group, seq, rope=None, cache=None, states=(), name="win"):
    nt, tm = h.shape[0], h.shape[1]
    qw = WIN_Q_HEADS * HEAD_DIM
    kw = WIN_KV_HEADS * HEAD_DIM
    assert WIN_KV_HEADS * HEAD_DIM == LANE and WIN_GROUP * LANE == qw
    const2 = lambda a: pl.BlockSpec(a.shape, lambda i: (0, 0))
    in_specs = [
        pl.BlockSpec(memory_space=pltpu.SMEM),
        pl.BlockSpec((None, tm, D_MODEL), lambda i: (i, 0, 0)),
        _w_in_spec(layer, COL_WIN, qw + 2 * kw),
        const2(gq), const2(gk),
    ]
    args = [sink, h, w_in, gq, gk]
    y_spec = pl.BlockSpec((None, tm, qw), lambda i: (i, 0, 0))
    y_shape = jax.ShapeDtypeStruct((nt, tm, qw), BF16)
    scratch = [pltpu.VMEM((D_MODEL, qw + 2 * kw), BF16)]
    aliases = {}
    if cache is None:
        past = 0
        kv_spec = _state_spec(layer, group, (seq, kw))
        kv_shape = jax.ShapeDtypeStruct((nt * group, DEPTH, seq, kw), F32)
        out_specs = [y_spec, kv_spec, kv_spec]
        out_shape = [y_shape, kv_shape, kv_shape]
        aliases = {len(args) + n: 1 + n for n in range(len(states))}
        in_specs += [pl.BlockSpec(memory_space=pl.ANY)] * len(states)
        args += list(states)
    else:
        cos, sin = rope
        ck, cv = cache
        past = ck.shape[3]
        cspec = pl.BlockSpec((None, None, kw, past), lambda i: (i, layer, 0, 0))
        in_specs += [const2(cos), const2(sin), cspec, cspec]
        args += [cos, sin, ck, cv]
        out_specs = [y_spec]
        out_shape = [y_shape]
        scratch += [pltpu.VMEM((seq + 2 * BLOCK, kw), BF16), pltpu.VMEM((seq + 2 * BLOCK, 2 * kw), BF16)]
    return pl.pallas_call(
        functools.partial(_win_kernel, group=group, seq=seq, past=past, layer=layer),
        grid=(nt,),
        in_specs=in_specs,
        out_specs=out_specs,
        out_shape=out_shape,
        scratch_shapes=scratch,
        input_output_aliases=aliases,
        compiler_params=_cparams(("arbitrary",)),
        name=name,
    )(*args)


def _ret_kernel(*refs, group, seq, has_state, layer):
    if has_state:
        h_ref, wa_ref, wb_ref, df_ref, db_ref, s0f_ref, s0b_ref, y_ref, w_s, dec_s = refs
    else:
        h_ref, wa_ref, wb_ref, df_ref, db_ref = refs[:5]
        y_ref, sf_ref, sb_ref, w_s, dec_s = refs[-5:]
    qk_w = RET_HEADS * RET_DK
    v_w = RET_HEADS * RET_DV
    _cast_weights_once((wa_ref, wb_ref), w_s)
    proj = jnp.dot(h_ref[...], w_s[...], preferred_element_type=F32)
    q = proj[:, :qk_w]
    k = proj[:, qk_w:2 * qk_w] * (RET_DK ** -0.5)
    v = proj[:, 2 * qk_w:2 * qk_w + v_w]
    rg = proj[:, 2 * qk_w + v_w:]
    lgf_all = _log_gamma(df_ref[...])
    lgb_all = _log_gamma(db_ref[...])
    ch = min(seq, ATT_Q_TILE)
    n_ch = seq // ch

    @pl.when(pl.program_id(0) == 0)
    def _():
        rel = (_row((ch, ch)) - _lane((ch, ch))).astype(F32)
        for hd in range(RET_HEADS):
            lgf = lgf_all[hd:hd + 1, 0:1]
            lgb = lgb_all[hd:hd + 1, 0:1]
            dec_s[hd] = (jnp.where(rel >= 0, jnp.exp(jnp.maximum(rel, 0.0) * lgf), 0.0)
                         + jnp.where(rel <= 0, jnp.exp(jnp.maximum(-rel, 0.0) * lgb), 0.0))

    lane = _lane((ch, LANE))
    i_loc = _row((ch, LANE)).astype(F32)
    for hd in range(RET_HEADS):
        slab = slice((hd // 2) * LANE, (hd // 2 + 1) * LANE)
        vsl = slice(hd * RET_DV, (hd + 1) * RET_DV)
        half = hd % 2
        keep = lane >= RET_DK if half == 1 else lane < RET_DK
        lgf = lgf_all[hd:hd + 1, 0:1]
        lgb = lgb_all[hd:hd + 1, 0:1]
        q_in_f = jnp.exp((i_loc + 1.0) * lgf)
        q_in_b = jnp.exp((ch - i_loc) * lgb)
        k_out_f = jnp.exp((ch - 1.0 - i_loc) * lgf)
        k_out_b = jnp.exp(i_loc * lgb)
        carry_f = jnp.exp(ch * lgf)
        carry_b = jnp.exp(ch * lgb)
        dec = dec_s[hd]
        for g in range(group):
            rows = [slice(g * seq + c * ch, g * seq + (c + 1) * ch) for c in range(n_ch)]
            qm = [jnp.where(keep, q[r, slab], 0.0) for r in rows]
            kc = [k[r, slab] for r in rows]
            vb = [v[r, vsl].astype(BF16) for r in rows]
            o_state = [None] * n_ch
            for q_in, k_out, carry, s_ref, order in (
                    (q_in_f, k_out_f, carry_f, s0f_ref if has_state else sf_ref, range(n_ch)),
                    (q_in_b, k_out_b, carry_b, s0b_ref if has_state else sb_ref, range(n_ch - 1, -1, -1))):
                state = s_ref[hd // 2] if has_state else None
                for pos, c in enumerate(order):
                    if state is not None:
                        term = _bdot(qm[c] * q_in, state)
                        o_state[c] = term if o_state[c] is None else o_state[c] + term
                    if pos + 1 < n_ch or not has_state:
                        update = _bdot((kc[c] * k_out).T, vb[c])
                        state = update if state is None else state * carry + update
                if not has_state:
                    _store_state(s_ref, g, layer, state[half * RET_DK:(half + 1) * RET_DK], hd)
            for c, r in enumerate(rows):
                s = _bdot_nt(qm[c], kc[c])
                o = jnp.dot((s * dec).astype(BF16), vb[c], preferred_element_type=F32)
                if o_state[c] is not None:
                    o = o + o_state[c]
                y = _rms(o, None) * _silu(rg[r, vsl])
                y_ref[r, vsl] = y.astype(BF16)


def _ret_call(h, w_in, layer, dec_f, dec_b, group, seq, state=None, states=(), name="ret"):
    nt, tm = h.shape[0], h.shape[1]
    v_w = RET_HEADS * RET_DV
    ret_cols = COL_GATE - COL_RET
    const2 = lambda a: pl.BlockSpec(a.shape, lambda i: (0, 0))
    in_specs = [
        pl.BlockSpec((None, tm, D_MODEL), lambda i: (i, 0, 0)),
        _w_in_spec(layer, COL_RET, ret_cols // 2),
        _w_in_spec(layer, COL_RET + ret_cols // 2, ret_cols // 2),
        const2(dec_f), const2(dec_b),
    ]
    args = [h, w_in, w_in, dec_f, dec_b]
    y_spec = pl.BlockSpec((None, tm, v_w), lambda i: (i, 0, 0))
    y_shape = jax.ShapeDtypeStruct((nt, tm, v_w), BF16)
    aliases = {}
    if state is None:
        s_spec = _state_spec(layer, group, (RET_HEADS, RET_DK, RET_DV))
        s_shape = jax.ShapeDtypeStruct((nt * group, DEPTH, RET_HEADS, RET_DK, RET_DV), F32)
        out_specs = [y_spec, s_spec, s_spec]
        out_shape = [y_shape, s_shape, s_shape]
        aliases = {len(args) + n: 1 + n for n in range(len(states))}
        in_specs += [pl.BlockSpec(memory_space=pl.ANY)] * len(states)
        args += list(states)
    else:
        s0f, s0b = state
        sspec = pl.BlockSpec((None, None, 2, LANE, RET_DV), lambda i: (i, layer, 0, 0, 0))
        in_specs += [sspec, sspec]
        args += [s0f, s0b]
        out_specs = [y_spec]
        out_shape = [y_shape]
    return pl.pallas_call(
        functools.partial(_ret_kernel, group=group, seq=seq, has_state=state is not None, layer=layer),
        grid=(nt,),
        in_specs=in_specs,
        out_specs=out_specs,
        out_shape=out_shape,
        input_output_aliases=aliases,
        scratch_shapes=[pltpu.VMEM((D_MODEL, ret_cols), BF16),
                        pltpu.VMEM((RET_HEADS, min(seq, ATT_Q_TILE), min(seq, ATT_Q_TILE)), F32)],
        compiler_params=_cparams(("arbitrary",)),
        name=name,
    )(*args)


def _merge_ffn_kernel(x_ref, h_ref, mod_ref, gain_ref, y0_ref, y1_ref, y2_ref, y3_ref,
                      wg_ref, wb_ref, wbw_ref, wo_ref, wi_ref, wo2_ref, o_ref, merged_s, act_s):
    m = mod_ref[...]
    for rows in _row_groups(x_ref.shape[0]):
        h = h_ref[rows, :]
        ys = [r[rows, :] for r in (y0_ref, y1_ref, y2_ref, y3_ref)]
        for c in range(D_MODEL // MERGE_TILE):
            cols = slice(c * MERGE_TILE, (c + 1) * MERGE_TILE)
            acc = None
            for b in range(N_BRANCH):
                gate = jax.nn.sigmoid(jnp.dot(
                    h, wg_ref[:, b * D_MODEL + c * MERGE_TILE:b * D_MODEL + (c + 1) * MERGE_TILE],
                    preferred_element_type=F32))
                w_b = wbw_ref[:, cols] if b == WIN_BRANCH else wb_ref[b, :, cols]
                part = gate * jnp.dot(ys[b], w_b, preferred_element_type=F32)
                acc = part if acc is None else acc + part
            merged_s[rows, cols] = acc.astype(BF16)
        mixed = jnp.dot(merged_s[rows, :], wo_ref[...], preferred_element_type=F32)
        x = x_ref[rows, :] + m[5:6] * mixed
        o_ref[rows, :] = _swiglu_rows(x, m, 6, gain_ref[...], wi_ref, wo2_ref, act_s, rows)


def _merge_ffn_call(x, h, mod, mod_row, gain, ys, w_gate, w_branch, w_branch_win, w_out, w_ff_in, w_ff_out,
                    layer, name):
    nt, tm, _ = x.shape
    tok = pl.BlockSpec((None, tm, D_MODEL), lambda i: (i, 0, 0))
    ysp = pl.BlockSpec((None, tm, BRANCH_W), lambda i: (i, 0, 0))
    return pl.pallas_call(
        _merge_ffn_kernel,
        grid=(nt,),
        in_specs=[
            tok, tok,
            pl.BlockSpec((None, N_MOD, D_MODEL), lambda i: (mod_row(i), 0, 0)),
            pl.BlockSpec((1, D_MODEL), lambda i: (0, 0)),
            ysp, ysp, ysp, ysp,
            _resident((None, D_MODEL, N_BRANCH * D_MODEL), lambda i: (layer, 0, 0)),
            _resident((None, N_BRANCH, BRANCH_W, D_MODEL), lambda i: (layer, 0, 0, 0)),
            _resident((None, BRANCH_W, D_MODEL), lambda i: (layer, 0, 0)),
            _resident((None, D_MODEL, D_MODEL), lambda i: (layer, 0, 0)),
            _resident((None, D_MODEL, 2 * D_FF), lambda i: (layer, 0, 0)),
            _resident((None, D_FF, D_MODEL), lambda i: (layer, 0, 0)),
        ],
        out_specs=tok,
        out_shape=jax.ShapeDtypeStruct(x.shape, F32),
        scratch_shapes=[pltpu.VMEM((tm, D_MODEL), BF16), pltpu.VMEM((tm, D_FF), BF16)],
        compiler_params=_cparams(("parallel",)),
        name=name,
    )(x, h, mod, gain, *ys, w_gate, w_branch, w_branch_win, w_out, w_ff_in, w_ff_out)


def _dft_tables(seq):
    n_fft = 2 * seq
    idx = np.arange(seq, dtype=np.int64)
    ang = 2.0 * np.pi * ((idx[:, None] * idx[None, :]) % n_fft).astype(np.float64) / n_fft
    out = []
    for m in (np.cos(ang), np.sin(ang)):
        hi = jnp.asarray(m, F32).astype(BF16)
        lo = (jnp.asarray(m, F32) - hi.astype(F32)).astype(BF16)
        out += [hi, lo]
    return out


def _hyena_tables(seq):
    t = np.arange(seq, dtype=np.float32) / np.float32(seq)
    f = np.arange(1, HY_BANDS + 1, dtype=np.float32)
    ang = np.float32(2.0 * math.pi) * t[:, None] * f[None, :]
    feats = np.zeros((seq, LANE), np.float32)
    feats[:, 0] = t
    feats[:, 1:1 + HY_BANDS] = np.sin(ang)
    feats[:, 1 + HY_BANDS:HY_EMB] = np.cos(ang)
    min_decay = math.log(HY_TARGET) / HY_DECAY_LONG_PCT
    max_decay = math.log(HY_TARGET) / HY_DECAY_SHORT_PCT
    deltas = np.linspace(min_decay, max_decay, HY_W, dtype=np.float32)
    window = np.exp(-t[:, None] * np.abs(deltas)[None, :]).astype(np.float32)
    return jnp.asarray(feats), jnp.asarray(window)


def _rope_tables(seq):
    pos = np.arange(seq)
    row = (pos // GRID_W).astype(np.float32)
    col = (pos % GRID_W).astype(np.float32)
    inv_freq = (np.float32(ROPE_BASE) ** (-np.arange(ROPE_PAIRS, dtype=np.float32) / np.float32(ROPE_PAIRS)))
    lane = np.arange(LANE)
    in_head = lane % HEAD_DIM
    use_col = in_head >= HEAD_DIM // 2
    pair = in_head % ROPE_PAIRS
    second = (in_head % (2 * ROPE_PAIRS)) >= ROPE_PAIRS
    p = np.where(use_col[None, :], col[:, None], row[:, None]).astype(np.float32)
    ang = (p * inv_freq[pair][None, :]).astype(np.float32)
    cos = np.cos(ang).astype(np.float32)
    sin = np.sin(ang).astype(np.float32)
    sin = np.where(second[None, :], sin, -sin)
    return jnp.asarray(cos), jnp.asarray(sin)


def kernel(x_prompt, x_sample, c, cache_diff_k, cache_diff_v, cache_win_k, cache_win_v, state_ret_f, state_ret_b, c_ctx, norm_ffa, norm_mix, norm_ffb, w_ada, b_ada, w_ffa_in, w_ffa_out, w_ffb_in, w_ffb_out, w_in, hy_conv_w, hy_conv_b, hy_f_w1, hy_f_b1, hy_f_w2, hy_f_b2, hy_f_w3, hy_skip, diff_q_norm, diff_k_norm, diff_lambda, diff_subln, win_q_norm, win_k_norm, win_sink, ret_decay_f, ret_decay_b, w_branch, w_out):
    batch, seq, _ = x_prompt.shape
    dec_batch, dec_seq, _ = x_sample.shape
    past = cache_diff_k.shape[2]
    ctx_group = TOKEN_TILE // seq
    assert TOKEN_TILE % seq == 0 and batch % ctx_group == 0 and dec_seq == TOKEN_TILE
    assert 1 + dec_batch <= ADA_ROWS

    cond = jnp.zeros((ADA_ROWS, D_MODEL), F32).at[0].set(c_ctx).at[1:1 + dec_batch].set(c)
    mod = _ada_call(cond, w_ada, b_ada).reshape(DEPTH, ADA_ROWS, N_MOD, D_MODEL)

    bf = lambda a: a.astype(BF16)
    w_ffa_in_b, w_ffa_out_b, w_ffb_in_b, w_ffb_out_b = bf(w_ffa_in), bf(w_ffa_out), bf(w_ffb_in), bf(w_ffb_out)
    w_gate = bf(w_in[:, :, COL_GATE:])
    w_branch_b, w_out_b = bf(w_branch), bf(w_out)
    w_branch_win = jnp.swapaxes(
        w_branch_b[:, WIN_BRANCH].reshape(DEPTH, WIN_KV_HEADS, WIN_GROUP, HEAD_DIM, D_MODEL), 1, 2
    ).reshape(DEPTH, BRANCH_W, D_MODEL)

    rope = _rope_tables(dec_seq)
    tables = {s: (_hyena_tables(s), _dft_tables(s)) for s in (seq, dec_seq)}

    ck_d = jnp.transpose(cache_diff_k, (0, 1, 3, 4, 5, 2)).reshape(dec_batch, DEPTH, -1, past)
    cv_d = cache_diff_v.reshape(dec_batch, DEPTH, past * DIFF_HEADS, 2 * HEAD_DIM)
    ck_w = jnp.transpose(cache_win_k, (0, 1, 3, 4, 2)).reshape(dec_batch, DEPTH, -1, past)
    cv_w = jnp.transpose(cache_win_v, (0, 1, 3, 4, 2)).reshape(dec_batch, DEPTH, -1, past)
    s0_f = state_ret_f.reshape(dec_batch, DEPTH, 2, LANE, RET_DV)
    s0_b = state_ret_b.reshape(dec_batch, DEPTH, 2, LANE, RET_DV)

    pad_rows = lambda a, n: jnp.pad(a, ((0, n - a.shape[0]), (0, 0)))
    tile_lanes = lambda a, n: jnp.tile(a.reshape(1, -1), (1, n))

    tiled = lambda a, tile: a.reshape(-1, tile, a.shape[-1])
    y_ctx = x_prompt
    y_lat = x_sample
    ctx_row = lambda tile: (lambda i: 0)
    lat_row = lambda tile: (lambda i: i // (dec_seq // tile) + 1)
    st_diff, st_win, st_ret = (), (), ()

    for l in range(DEPTH):
        lam_init = 0.8 - 0.6 * math.exp(-0.3 * l)
        gain = lambda a: a[l].reshape(1, -1)
        gq_d, gk_d = tile_lanes(diff_q_norm[l], 2 * DIFF_HEADS), tile_lanes(diff_k_norm[l], 2 * DIFF_HEADS)
        gs_d = tile_lanes(diff_subln[l], DIFF_HEADS)
        gq_w, gk_w = tile_lanes(win_q_norm[l], WIN_Q_HEADS), tile_lanes(win_k_norm[l], WIN_KV_HEADS)
        dec_f = jnp.broadcast_to(ret_decay_f[l][:, None], (RET_HEADS, LANE))
        dec_b = jnp.broadcast_to(ret_decay_b[l][:, None], (RET_HEADS, LANE))
        w1 = jnp.pad(hy_f_w1[l], ((0, LANE - HY_EMB), (0, LANE - HY_FH)))
        b1 = jnp.pad(hy_f_b1[l].reshape(1, -1), ((0, 0), (0, LANE - HY_FH)))
        w2 = jnp.pad(hy_f_w2[l], ((0, LANE - HY_FH), (0, LANE - HY_FH)))
        b2 = jnp.pad(hy_f_b2[l].reshape(1, -1), ((0, 0), (0, LANE - HY_FH)))
        w3 = pad_rows(hy_f_w3[l], LANE)
        cw, cb, skip = hy_conv_w[l], hy_conv_b[l].reshape(1, -1), hy_skip[l].reshape(1, -1)

        for is_lat in (False, True):
            x = y_lat if is_lat else y_ctx
            s_len = dec_seq if is_lat else seq
            group = 1 if is_lat else ctx_group
            mod_row = lat_row if is_lat else ctx_row
            tag = f"{'lat' if is_lat else 'ctx'}{l}"
            (feats, window), (c_hi, c_lo, s_hi, s_lo) = tables[s_len]

            x, h = _ffn_a_call(tiled(x, FFN_A_TILE), mod[l], mod_row(FFN_A_TILE), gain(norm_ffa), gain(norm_mix),
                               w_ffa_in_b, w_ffa_out_b, l, f"ffn_a_{tag}")
            h = tiled(h, TOKEN_TILE)

            kr, ki, kn = _hy_filter_call(s_len, feats, w1, b1, w2, b2, w3, window, c_hi, c_lo, s_hi, s_lo)
            y_hy = _hyena_call(h, w_in, l, cw, cb, kr, ki, kn, skip, c_hi, s_hi, group, s_len, f"hyena_{tag}")
            if is_lat:
                (y_diff,) = _diff_call(h, w_in, l, gq_d, gk_d, diff_lambda[l], gs_d, lam_init,
                                       group, s_len, rope=rope, cache=(ck_d, cv_d), name=f"diff_{tag}")
                (y_win,) = _win_call(h, w_in, l, win_sink, gq_w, gk_w, group, s_len,
                                     rope=rope, cache=(ck_w, cv_w), name=f"win_{tag}")
                (y_ret,) = _ret_call(h, w_in, l, dec_f, dec_b, group, s_len, state=(s0_f, s0_b),
                                     name=f"ret_{tag}")
            else:
                y_diff, *st_diff = _diff_call(h, w_in, l, gq_d, gk_d, diff_lambda[l], gs_d, lam_init,
                                              group, s_len, states=st_diff, name=f"diff_{tag}")
                y_win, *st_win = _win_call(h, w_in, l, win_sink, gq_w, gk_w, group, s_len,
                                           states=st_win, name=f"win_{tag}")
                y_ret, *st_ret = _ret_call(h, w_in, l, dec_f, dec_b, group, s_len, states=st_ret,
                                           name=f"ret_{tag}")

            x = _merge_ffn_call(tiled(x, MERGE_FFN_TILE), tiled(h, MERGE_FFN_TILE), mod[l], mod_row(MERGE_FFN_TILE),
                                gain(norm_ffb),
                                tuple(tiled(y, MERGE_FFN_TILE) for y in (y_hy, y_diff, y_win, y_ret)),
                                w_gate, w_branch_b, w_branch_win, w_out_b, w_ffb_in_b, w_ffb_out_b, l,
                                f"merge_ffn_b_{tag}")
            if is_lat:
                y_lat = x
            else:
                y_ctx = x

    return (y_ctx.reshape(batch, seq, D_MODEL), y_lat.reshape(dec_batch, dec_seq, D_MODEL),
            st_diff[0].reshape(batch, DEPTH, seq, DIFF_HEADS, 2, HEAD_DIM),
            st_diff[1].reshape(batch, DEPTH, seq, DIFF_HEADS, 2 * HEAD_DIM),
            st_win[0].reshape(batch, DEPTH, seq, WIN_KV_HEADS, HEAD_DIM),
            st_win[1].reshape(batch, DEPTH, seq, WIN_KV_HEADS, HEAD_DIM),
            st_ret[0], st_ret[1])
```

```python
import functools
import math

import jax
import jax.numpy as jnp
import numpy as np
from jax import lax
from jax.experimental import pallas as pl
from jax.experimental.pallas import tpu as pltpu

F32 = jnp.float32
BF16 = jnp.bfloat16

D_MODEL = 1024
DEPTH = 2
GRID_W = 64
HEAD_DIM = 64
ROPE_PAIRS = HEAD_DIM // 4
ROPE_BASE = 10000.0
EPS = 1e-6
NEG_INF = -1e30
LOG2_E = math.log2(math.e)
D_FF = 2816
N_MOD = 9
N_BRANCH = 4
WIN_BRANCH = 2
BRANCH_W = 512

HY_W = BRANCH_W
HY_BANDS = 16
HY_EMB = 1 + 2 * HY_BANDS
HY_FH = 64
HY_SIN_W = 1.0
HY_TARGET = 1e-2
HY_DECAY_SHORT_PCT = 0.3
HY_DECAY_LONG_PCT = 1.5

DIFF_HEADS = 4
WIN_Q_HEADS = 8
WIN_KV_HEADS = 2
WIN_GROUP = WIN_Q_HEADS // WIN_KV_HEADS
WINDOW = 128
BLOCK = 128
RET_HEADS = 4
RET_DK = 64
RET_DV = 128

COL_HY = 0
COL_DIFF = 3 * HY_W
COL_WIN = COL_DIFF + 3 * (2 * DIFF_HEADS * HEAD_DIM)
COL_RET = COL_WIN + (WIN_Q_HEADS + 2 * WIN_KV_HEADS) * HEAD_DIM
COL_GATE = COL_RET + 2 * RET_HEADS * (RET_DK + RET_DV)
IN_COLS = COL_GATE + N_BRANCH * D_MODEL

LANE = 128
TOKEN_TILE = 1024
FFN_A_TILE = 512
MERGE_FFN_TILE = 512
FF_TILE = 256
ROW_GROUP = 256
MERGE_TILE = 256
ADA_TILE = 2304
ADA_ROWS = 16
ATT_Q_TILE = 256
VMEM_LIMIT = 56 * 1024 * 1024


def _cparams(sem):
    return pltpu.CompilerParams(dimension_semantics=sem, vmem_limit_bytes=VMEM_LIMIT)


def _bdot(a, b):
    return jnp.dot(a.astype(BF16), b.astype(BF16), preferred_element_type=F32)


def _bdot_nt(a, b):
    return lax.dot_general(a.astype(BF16), b.astype(BF16), (((1,), (1,)), ((), ())),
                           preferred_element_type=F32)


def _split(a):
    hi = a.astype(BF16)
    lo = (a - hi.astype(F32)).astype(BF16)
    return hi, lo


def _dot3_pre(ah, al, b):
    bh, bl = _split(b)
    d = functools.partial(jnp.dot, preferred_element_type=F32)
    return d(ah, bh) + d(ah, bl) + d(al, bh)


def _dot3(a, b):
    return _dot3_pre(*_split(a), b)


def _rms_heads(x, gain):
    first = _lane((x.shape[0], LANE)) < HEAD_DIM
    out = []
    for s in range(x.shape[1] // LANE):
        xs = x[:, s * LANE:(s + 1) * LANE]
        sq = xs * xs
        lo = jnp.sum(jnp.where(first, sq, 0.0), axis=-1, keepdims=True)
        hi = jnp.sum(jnp.where(first, 0.0, sq), axis=-1, keepdims=True)
        r_lo = lax.rsqrt(lo * (1.0 / HEAD_DIM) + EPS)
        r_hi = lax.rsqrt(hi * (1.0 / HEAD_DIM) + EPS)
        out.append(xs * jnp.where(first, r_lo, r_hi))
    return jnp.concatenate(out, axis=1) * gain


def _rms(x, gain):
    y = x * lax.rsqrt(jnp.mean(x * x, axis=-1, keepdims=True) + EPS)
    return y if gain is None else y * gain


def _silu(x):
    return x * jax.nn.sigmoid(x)


def _exp2_bf16(x):
    return jnp.exp2(x).astype(BF16)


def _lane(shape):
    return lax.broadcasted_iota(jnp.int32, shape, 1)


def _row(shape):
    return lax.broadcasted_iota(jnp.int32, shape, 0)


def _rope128(x, cos, sin):
    partner = jnp.where((_lane(x.shape) & 31) < 16,
                        pltpu.roll(x, LANE - ROPE_PAIRS, axis=1),
                        pltpu.roll(x, ROPE_PAIRS, axis=1))
    return x * cos + partner * sin


def _cast_weights_once(w_refs, w_s):
    @pl.when(pl.program_id(0) == 0)
    def _():
        off = 0
        for r in w_refs:
            w_s[:, off:off + r.shape[1]] = r[...].astype(BF16)
            off += r.shape[1]


def _w_in_spec(layer, col, width):
    assert col % width == 0
    return pl.BlockSpec((None, D_MODEL, width), lambda i: (layer, 0, col // width),
                        pipeline_mode=pl.Buffered(1))


def _state_spec(layer, group, tail):
    zeros = (0,) * len(tail)
    if layer == 0:
        return pl.BlockSpec((group, DEPTH) + tail, lambda i: (i, 0) + zeros)
    return pl.BlockSpec((group, None) + tail, lambda i: (i, layer) + zeros)


def _store_state(ref, g, layer, value, *sub):
    if layer == 0:
        ref[(g, 0) + sub] = value
        for later in range(1, DEPTH):
            ref[(g, later) + sub] = jnp.zeros_like(value)
    else:
        ref[(g,) + sub] = value


def _log_gamma(decay):
    x = -decay
    return -(jnp.maximum(x, 0.0) + jnp.log1p(jnp.exp(-jnp.abs(x))))


def _ada_kernel(cond_ref, w_ref, b_ref, o_ref):
    s = _silu(cond_ref[...])
    o_ref[...] = _bdot(s, w_ref[...]) + b_ref[...]


def _ada_call(cond, w_ada, b_ada):
    rows = cond.shape[0]
    n_out = N_MOD * D_MODEL
    return pl.pallas_call(
        _ada_kernel,
        grid=(DEPTH, n_out // ADA_TILE),
        in_specs=[
            pl.BlockSpec((rows, D_MODEL), lambda l, j: (0, 0)),
            pl.BlockSpec((None, D_MODEL, ADA_TILE), lambda l, j: (l, 0, j)),
            pl.BlockSpec((None, 1, ADA_TILE), lambda l, j: (l, 0, j)),
        ],
        out_specs=pl.BlockSpec((None, rows, ADA_TILE), lambda l, j: (l, 0, j)),
        out_shape=jax.ShapeDtypeStruct((DEPTH, rows, n_out), F32),
        compiler_params=_cparams(("parallel", "parallel")),
        name="ada_mod",
    )(cond, w_ada, b_ada.reshape(DEPTH, 1, n_out))


def _swiglu_rows(x, m, mod_base, gain, wi_ref, wo_ref, act_s, rows):
    h = (_rms(x, gain) * (1.0 + m[mod_base + 1:mod_base + 2]) + m[mod_base:mod_base + 1]).astype(BF16)
    for c in range(D_FF // FF_TILE):
        a = jnp.dot(h, wi_ref[:, c * FF_TILE:(c + 1) * FF_TILE], preferred_element_type=F32)
        g = jnp.dot(h, wi_ref[:, D_FF + c * FF_TILE:D_FF + (c + 1) * FF_TILE], preferred_element_type=F32)
        act_s[rows, c * FF_TILE:(c + 1) * FF_TILE] = (_silu(a) * g).astype(BF16)
    out = jnp.dot(act_s[rows, :], wo_ref[...], preferred_element_type=F32)
    return x + 0.5 * m[mod_base + 2:mod_base + 3] * out


def _row_groups(n_rows):
    return [slice(p * ROW_GROUP, (p + 1) * ROW_GROUP) for p in range(n_rows // ROW_GROUP)]


def _ffn_a_kernel(x_ref, mod_ref, gain_ref, gain2_ref, wi_ref, wo_ref, y_ref, h2_ref, act_s):
    m = mod_ref[...]
    for rows in _row_groups(x_ref.shape[0]):
        y = _swiglu_rows(x_ref[rows, :], m, 0, gain_ref[...], wi_ref, wo_ref, act_s, rows)
        y_ref[rows, :] = y
        h2_ref[rows, :] = (_rms(y, gain2_ref[...]) * (1.0 + m[4:5]) + m[3:4]).astype(BF16)


def _resident(block_shape, index_map):
    return pl.BlockSpec(block_shape, index_map, pipeline_mode=pl.Buffered(1))


def _ffn_a_call(x, mod, mod_row, gain, gain2, w_in, w_out, layer, name):
    nt, tm, _ = x.shape
    tok = pl.BlockSpec((None, tm, D_MODEL), lambda i: (i, 0, 0))
    vec = pl.BlockSpec((1, D_MODEL), lambda i: (0, 0))
    return pl.pallas_call(
        _ffn_a_kernel,
        grid=(nt,),
        in_specs=[
            tok, pl.BlockSpec((None, N_MOD, D_MODEL), lambda i: (mod_row(i), 0, 0)), vec, vec,
            _resident((None, D_MODEL, 2 * D_FF), lambda i: (layer, 0, 0)),
            _resident((None, D_FF, D_MODEL), lambda i: (layer, 0, 0)),
        ],
        out_specs=[tok, tok],
        out_shape=[jax.ShapeDtypeStruct(x.shape, F32), jax.ShapeDtypeStruct(x.shape, BF16)],
        scratch_shapes=[pltpu.VMEM((tm, D_FF), BF16)],
        compiler_params=_cparams(("parallel",)),
        name=name,
    )(x, mod, gain, gain2, w_in, w_out)


def _hy_filter_kernel(feats_ref, w1_ref, b1_ref, w2_ref, b2_ref, w3_ref, win_ref,
                      ch_ref, cl_ref, sh_ref, sl_ref, kr_ref, ki_ref, kn_ref, *, seq):
    n_fft = 2 * seq
    z = jnp.sin(HY_SIN_W * (_dot3(feats_ref[...], w1_ref[...]) + b1_ref[...]))
    z = jnp.sin(HY_SIN_W * (_dot3(z, w2_ref[...]) + b2_ref[...]))
    zz = _dot3(z, w3_ref[...])
    win = win_ref[...]
    hf = zz[:, :HY_W] * win
    hb = zz[:, HY_W:] * win
    norm = (jnp.sum(jnp.abs(hf), axis=0, keepdims=True)
            + jnp.sum(jnp.abs(hb), axis=0, keepdims=True))
    hf = hf / norm
    hb = hb / norm
    row = _row(hf.shape)
    hb0 = jnp.where(row == 0, 0.0, hb)
    even = hf + hb0
    odd = hb0 - hf
    wk = jnp.where(row == 0, 1.0 / n_fft, 2.0 / n_fft)
    kr_ref[...] = _dot3_pre(ch_ref[...], cl_ref[...], even) * wk
    ki_ref[...] = _dot3_pre(sh_ref[...], sl_ref[...], odd) * wk
    sgn = jnp.where((row & 1) == 0, 1.0, -1.0)
    kn_ref[...] = jnp.sum(even * sgn, axis=0, keepdims=True) * (1.0 / n_fft)


def _hy_filter_call(seq, feats, w1, b1, w2, b2, w3, win, ch, cl, sh, sl):
    args = (feats, w1, b1, w2, b2, w3, win, ch, cl, sh, sl)
    return pl.pallas_call(
        functools.partial(_hy_filter_kernel, seq=seq),
        out_shape=[jax.ShapeDtypeStruct((seq, HY_W), F32),
                   jax.ShapeDtypeStruct((seq, HY_W), F32),
                   jax.ShapeDtypeStruct((1, HY_W), F32)],
        compiler_params=pltpu.CompilerParams(vmem_limit_bytes=VMEM_LIMIT),
        name=f"hyena_filter_{seq}",
    )(*args)


def _hyena_kernel(h_ref, w_ref, cw_ref, cb_ref, kr_ref, ki_ref, kn_ref, skip_ref,
                  c_ref, s_ref, y_ref, w_s, *, group, seq):
    _cast_weights_once((w_ref,), w_s)
    proj = jnp.dot(h_ref[...], w_s[...], preferred_element_type=F32)
    cw = cw_ref[...]
    cmat = c_ref[...]
    smat = s_ref[...]
    kr = kr_ref[...]
    ki = ki_ref[...]
    row = _row((seq, 3 * HY_W))
    row_w = _row((seq, HY_W))
    sgn = jnp.where((row_w & 1) == 0, 1.0, -1.0)
    for g in range(group):
        hy = proj[g * seq:(g + 1) * seq]
        prev = jnp.where(row == 0, 0.0, pltpu.roll(hy, 1, axis=0))
        nxt = jnp.where(row == seq - 1, 0.0, pltpu.roll(hy, seq - 1, axis=0))
        u = prev * cw[0:1] + hy * cw[1:2] + nxt * cw[2:3] + cb_ref[...]
        v, x0, x1 = u[:, :HY_W], u[:, HY_W:2 * HY_W], u[:, 2 * HY_W:]
        z = v * x1
        zb = z.astype(BF16)
        zr = jnp.dot(cmat, zb, preferred_element_type=F32)
        zs = jnp.dot(smat, zb, preferred_element_type=F32)
        yr = (zr * kr + zs * ki).astype(BF16)
        yi = (zr * ki - zs * kr).astype(BF16)
        nyq = jnp.sum(z * sgn, axis=0, keepdims=True) * kn_ref[...]
        conv = (jnp.dot(cmat, yr, preferred_element_type=F32)
                - jnp.dot(smat, yi, preferred_element_type=F32) + sgn * nyq)
        y_ref[g * seq:(g + 1) * seq, :] = (x0 * (conv + skip_ref[...] * z)).astype(BF16)


def _hyena_call(h, w_in, layer, cw, cb, kr, ki, kn, skip, cmat, smat, group, seq, name):
    nt, tm, _ = h.shape
    const2 = lambda a: pl.BlockSpec(a.shape, lambda i: (0, 0))
    return pl.pallas_call(
        functools.partial(_hyena_kernel, group=group, seq=seq),
        grid=(nt,),
        in_specs=[
            pl.BlockSpec((None, tm, D_MODEL), lambda i: (i, 0, 0)),
            _w_in_spec(layer, COL_HY, 3 * HY_W),
            const2(cw), const2(cb), const2(kr), const2(ki), const2(kn), const2(skip),
            const2(cmat), const2(smat),
        ],
        out_specs=pl.BlockSpec((None, tm, HY_W), lambda i: (i, 0, 0)),
        out_shape=jax.ShapeDtypeStruct((nt, tm, HY_W), BF16),
        scratch_shapes=[pltpu.VMEM((D_MODEL, 3 * HY_W), BF16)],
        compiler_params=_cparams(("arbitrary",)),
        name=name,
    )(h, w_in, cw, cb, kr, ki, kn, skip, cmat, smat)


def _diff_kernel(*refs, group, seq, past, lam_init, layer):
    has_cache = past > 0
    if has_cache:
        (h_ref, w_ref, gq_ref, gk_ref, lam_ref, gs_ref, cos_ref, sin_ref,
         ckt_ref, cv_ref, y_ref, w_s, k_s, v_s) = refs
    else:
        h_ref, w_ref, gq_ref, gk_ref, lam_ref, gs_ref = refs[:6]
        y_ref, ko_ref, vo_ref, w_s = refs[-4:]
    width = 2 * DIFF_HEADS * HEAD_DIM
    _cast_weights_once((w_ref,), w_s)
    proj = jnp.dot(h_ref[...], w_s[...], preferred_element_type=F32)
    q, k, v = proj[:, :width], proj[:, width:2 * width], proj[:, 2 * width:]
    q = _rms_heads(q, gq_ref[...])
    k = _rms_heads(k, gk_ref[...])
    dl = lam_ref[...]
    lam = (jnp.exp(jnp.sum(dl[0:1] * dl[1:2], axis=1, keepdims=True))
           - jnp.exp(jnp.sum(dl[2:3] * dl[3:4], axis=1, keepdims=True)) + lam_init)
    ones = jnp.ones((seq, LANE), BF16)
    if has_cache:
        cos, sin = cos_ref[...], sin_ref[...]
        q = jnp.concatenate([_rope128(q[:, s * LANE:(s + 1) * LANE], cos, sin)
                             for s in range(width // LANE)], axis=1)
        k = jnp.concatenate([_rope128(k[:, s * LANE:(s + 1) * LANE], cos, sin)
                             for s in range(width // LANE)], axis=1)
        k_s[...] = k.astype(BF16)
        for hh in range(DIFF_HEADS):
            v_s[0:seq, 2 * hh * LANE:(2 * hh + 1) * LANE] = v[:, hh * LANE:(hh + 1) * LANE].astype(BF16)
            v_s[seq:seq + past, 2 * hh * LANE:(2 * hh + 1) * LANE] = (
                cv_ref[pl.ds(hh, past, stride=DIFF_HEADS), :].astype(BF16))
            v_s[:, (2 * hh + 1) * LANE:(2 * hh + 2) * LANE] = jnp.ones((seq + past, LANE), BF16)
    else:
        for g in range(group):
            rows = slice(g * seq, (g + 1) * seq)
            _store_state(ko_ref, g, layer, k[rows])
            for hh in range(DIFF_HEADS):
                _store_state(vo_ref, g, layer, v[rows, hh * LANE:(hh + 1) * LANE],
                             pl.ds(hh, seq, stride=DIFF_HEADS))
    q = q * (HEAD_DIM ** -0.5 * LOG2_E)
    tq = min(seq, ATT_Q_TILE)
    lane = _lane((tq, LANE))
    for g in range(group):
        for hh in range(DIFF_HEADS):
            sl = slice(hh * LANE, (hh + 1) * LANE)
            if has_cache:
                keys = k_s[:, sl]
                vals = v_s[0:seq, 2 * hh * LANE:(2 * hh + 2) * LANE]
                vals_c = v_s[seq:seq + past, 2 * hh * LANE:(2 * hh + 2) * LANE]
                keys_t = ckt_ref[sl, :].astype(BF16)
            else:
                keys = k[g * seq:(g + 1) * seq, sl].astype(BF16)
                vals = jnp.concatenate([v[g * seq:(g + 1) * seq, sl].astype(BF16), ones], axis=1)
            for qi in range(seq // tq):
                r0 = g * seq + qi * tq
                q128 = q[r0:r0 + tq, sl]
                qs = jnp.concatenate([jnp.where(lane < HEAD_DIM, q128, 0.0),
                                      jnp.where(lane >= HEAD_DIM, q128, 0.0)], axis=0).astype(BF16)
                s = _bdot_nt(qs, keys)
                if has_cache:
                    sc = jnp.dot(qs, keys_t, preferred_element_type=F32)
                    m = jnp.maximum(jnp.max(s, axis=-1, keepdims=True), jnp.max(sc, axis=-1, keepdims=True))
                    oe = (jnp.dot(_exp2_bf16(s - m), vals, preferred_element_type=F32)
                          + jnp.dot(_exp2_bf16(sc - m), vals_c, preferred_element_type=F32))
                else:
                    e = _exp2_bf16(s - jnp.max(s, axis=-1, keepdims=True))
                    oe = jnp.dot(e, vals, preferred_element_type=F32)
                r = 1.0 / oe[:, LANE:]
                o = oe[:tq, :LANE] * r[:tq] - oe[tq:, :LANE] * (lam * r[tq:])
                y = _rms(o, gs_ref[:, sl]) * (1.0 - lam_init)
                y_ref[r0:r0 + tq, sl] = y.astype(BF16)


def _diff_call(h, w_in, layer, gq, gk, lam, gs, lam_init, group, seq,
               rope=None, cache=None, states=(), name="diff"):
    nt, tm = h.shape[0], h.shape[1]
    width = 2 * DIFF_HEADS * HEAD_DIM
    const2 = lambda a: pl.BlockSpec(a.shape, lambda i: (0, 0))
    in_specs = [
        pl.BlockSpec((None, tm, D_MODEL), lambda i: (i, 0, 0)),
        _w_in_spec(layer, COL_DIFF, 3 * width),
        const2(gq), const2(gk), const2(lam), const2(gs),
    ]
    args = [h, w_in, gq, gk, lam, gs]
    y_spec = pl.BlockSpec((None, tm, width), lambda i: (i, 0, 0))
    y_shape = jax.ShapeDtypeStruct((nt, tm, width), BF16)
    scratch = [pltpu.VMEM((D_MODEL, 3 * width), BF16)]
    aliases = {}
    if cache is None:
        past = 0
        k_spec = _state_spec(layer, group, (seq, width))
        k_shape = jax.ShapeDtypeStruct((nt * group, DEPTH, seq, width), F32)
        v_spec = _state_spec(layer, group, (seq * DIFF_HEADS, 2 * HEAD_DIM))
        v_shape = jax.ShapeDtypeStruct((nt * group, DEPTH, seq * DIFF_HEADS, 2 * HEAD_DIM), F32)
        out_specs = [y_spec, k_spec, v_spec]
        out_shape = [y_shape, k_shape, v_shape]
        aliases = {len(args) + n: 1 + n for n in range(len(states))}
        in_specs += [pl.BlockSpec(memory_space=pl.ANY)] * len(states)
        args += list(states)
    else:
        cos, sin = rope
        ckt, cv = cache
        past = ckt.shape[3]
        in_specs += [const2(cos), const2(sin),
                     pl.BlockSpec((None, None, width, past), lambda i: (i, layer, 0, 0)),
                     pl.BlockSpec((None, None, past * DIFF_HEADS, 2 * HEAD_DIM), lambda i: (i, layer, 0, 0))]
        args += [cos, sin, ckt, cv]
        out_specs = [y_spec]
        out_shape = [y_shape]
        scratch += [pltpu.VMEM((seq, width), BF16), pltpu.VMEM((seq + past, 2 * width), BF16)]
    return pl.pallas_call(
        functools.partial(_diff_kernel, group=group, seq=seq, past=past, lam_init=lam_init, layer=layer),
        grid=(nt,),
        in_specs=in_specs,
        out_specs=out_specs,
        out_shape=out_shape,
        scratch_shapes=scratch,
        input_output_aliases=aliases,
        compiler_params=_cparams(("arbitrary",)),
        name=name,
    )(*args)


def _win_heads(q, hk, lane):
    keep = lane >= HEAD_DIM if hk == 1 else lane < HEAD_DIM
    return [jnp.where(keep, q[:, s * LANE:(s + 1) * LANE], 0.0) for s in range(WIN_GROUP)]


def _win_place(o_by_kv, lane):
    return [jnp.where(lane < HEAD_DIM, o_by_kv[0][s], o_by_kv[1][s]) for s in range(WIN_GROUP)]


def _sink_col(sink_ref, layer, hk, rows_per_head):
    rows = WIN_GROUP * rows_per_head
    r = _row((rows, 1))
    col = jnp.full((rows, 1), sink_ref[layer, hk * WIN_GROUP], F32)
    for gq in range(1, WIN_GROUP):
        col = jnp.where(r >= gq * rows_per_head, sink_ref[layer, hk * WIN_GROUP + gq], col)
    return col


def _win_kernel(*refs, group, seq, past, layer):
    has_cache = past > 0
    if has_cache:
        (sink_ref, h_ref, w_ref, gq_ref, gk_ref, cos_ref, sin_ref, ck_ref, cv_ref,
         y_ref, w_s, k_s, v_s) = refs
    else:
        sink_ref, h_ref, w_ref, gq_ref, gk_ref = refs[:5]
        y_ref, ko_ref, vo_ref, w_s = refs[-4:]
    qw = WIN_Q_HEADS * HEAD_DIM
    kw = WIN_KV_HEADS * HEAD_DIM

    @pl.when(pl.program_id(0) == 0)
    def _():
        for s in range(WIN_GROUP):
            first = w_ref[:, s * HEAD_DIM:(s + 1) * HEAD_DIM]
            second = w_ref[:, (WIN_GROUP + s) * HEAD_DIM:(WIN_GROUP + s + 1) * HEAD_DIM]
            w_s[:, s * LANE:(s + 1) * LANE] = jnp.concatenate([first, second], axis=1).astype(BF16)
        w_s[:, qw:] = w_ref[:, qw:].astype(BF16)

    proj = jnp.dot(h_ref[...], w_s[...], preferred_element_type=F32)
    q, k, v = proj[:, :qw], proj[:, qw:qw + kw], proj[:, qw + kw:]
    q = _rms_heads(q, gq_ref[...])
    k = _rms_heads(k, gk_ref[...])
    scale = HEAD_DIM ** -0.5 * LOG2_E
    if not has_cache:
        for g in range(group):
            _store_state(ko_ref, g, layer, k[g * seq:(g + 1) * seq])
            _store_state(vo_ref, g, layer, v[g * seq:(g + 1) * seq])
        q = q * scale
        lane = _lane((seq, LANE))
        ones = jnp.ones((seq, LANE), BF16)
        for g in range(group):
            rows = slice(g * seq, (g + 1) * seq)
            kb = k[rows].astype(BF16)
            vb = jnp.concatenate([v[rows].astype(BF16), ones], axis=1)
            qg = q[rows]
            o_by_kv = []
            for hk in range(WIN_KV_HEADS):
                qs = jnp.concatenate(_win_heads(qg, hk, lane), axis=0)
                s = _bdot_nt(qs, kb)
                sink = _sink_col(sink_ref, layer, hk, seq) * LOG2_E
                m = jnp.maximum(jnp.max(s, axis=-1, keepdims=True), sink)
                oe = jnp.dot(_exp2_bf16(s - m), vb, preferred_element_type=F32)
                o = oe[:, :LANE] * (1.0 / (oe[:, LANE:] + jnp.exp2(sink - m)))
                o_by_kv.append([o[gq * seq:(gq + 1) * seq] for gq in range(WIN_GROUP)])
            for s_idx, slab in enumerate(_win_place(o_by_kv, lane)):
                y_ref[rows, s_idx * LANE:(s_idx + 1) * LANE] = slab.astype(BF16)
        return

    cos, sin = cos_ref[...], sin_ref[...]
    q = jnp.concatenate([_rope128(q[:, s * LANE:(s + 1) * LANE], cos, sin)
                         for s in range(qw // LANE)], axis=1) * scale
    k = _rope128(k, cos, sin)
    zpad = jnp.zeros((BLOCK, kw), BF16)
    k_s[0:BLOCK, :] = zpad
    k_s[BLOCK:BLOCK + seq, :] = k.astype(BF16)
    k_s[BLOCK + seq:2 * BLOCK + seq, :] = zpad
    v_s[0:BLOCK, 0:kw] = zpad
    v_s[BLOCK:BLOCK + seq, 0:kw] = v.astype(BF16)
    v_s[BLOCK + seq:2 * BLOCK + seq, 0:kw] = zpad
    v_s[:, kw:2 * kw] = jnp.ones((seq + 2 * BLOCK, kw), BF16)
    ckt = ck_ref[...].astype(BF16)
    cvt = jnp.concatenate([cv_ref[...].astype(BF16), jnp.ones((kw, past), BF16)], axis=0)
    lane_l = _lane((seq, LANE))
    lane_b = _lane((BLOCK, LANE))
    nb = seq // BLOCK
    rows_q = WIN_GROUP * BLOCK
    kk = _lane((rows_q, 3 * BLOCK))
    qi = _row((rows_q, 3 * BLOCK)) & (BLOCK - 1)
    rel = kk - BLOCK - qi
    bias_mid = jnp.where(rel <= WINDOW, jnp.where(rel >= -WINDOW, 0.0, NEG_INF), NEG_INF)
    bias = {(False, False): bias_mid,
            (True, False): jnp.where(kk >= BLOCK, bias_mid, NEG_INF),
            (False, True): jnp.where(kk < 2 * BLOCK, bias_mid, NEG_INF)}
    bias[(True, True)] = jnp.where(kk < 2 * BLOCK, bias[(True, False)], NEG_INF)
    heads = [_win_heads(q, hk, lane_l) for hk in range(WIN_KV_HEADS)]
    sinks = [_sink_col(sink_ref, layer, hk, BLOCK) * LOG2_E for hk in range(WIN_KV_HEADS)]
    for n in range(nb):
        o_by_kv = []
        for hk in range(WIN_KV_HEADS):
            sink = sinks[hk]
            qs = jnp.concatenate([hd[n * BLOCK:(n + 1) * BLOCK] for hd in heads[hk]], axis=0)
            sb = _bdot_nt(qs, k_s[n * BLOCK:(n + 3) * BLOCK, :]) + bias[(n == 0, n == nb - 1)]
            sc = jnp.dot(qs.astype(BF16), ckt, preferred_element_type=F32)
            m = jnp.maximum(jnp.maximum(jnp.max(sb, axis=-1, keepdims=True),
                                        jnp.max(sc, axis=-1, keepdims=True)), sink)
            eb = _exp2_bf16(sb - m)
            ec = _exp2_bf16(sc - m)
            oe = (jnp.dot(eb, v_s[n * BLOCK:(n + 3) * BLOCK, :], preferred_element_type=F32)
                  + _bdot_nt(ec, cvt))
            o = oe[:, :LANE] * (1.0 / (oe[:, LANE:] + jnp.exp2(sink - m)))
            o_by_kv.append([o[gq * BLOCK:(gq + 1) * BLOCK] for gq in range(WIN_GROUP)])
        for s_idx, slab in enumerate(_win_place(o_by_kv, lane_b)):
            y_ref[n * BLOCK:(n + 1) * BLOCK, s_idx * LANE:(s_idx + 1) * LANE] = slab.astype(BF16)


def _win_call(h, w_in, layer, sink, gq, gk, group, seq, rope=None, cache=None, states=(), name="win"):
    nt, tm = h.shape[0], h.shape[1]
    qw = WIN_Q_HEADS * HEAD_DIM
    kw = WIN_KV_HEADS * HEAD_DIM
    assert WIN_KV_HEADS * HEAD_DIM == LANE and WIN_GROUP * LANE == qw
    const2 = lambda a: pl.BlockSpec(a.shape, lambda i: (0, 0))
    in_specs = [
        pl.BlockSpec(memory_space=pltpu.SMEM),
        pl.BlockSpec((None, tm, D_MODEL), lambda i: (i, 0, 0)),
        _w_in_spec(layer, COL_WIN, qw + 2 * kw),
        const2(gq), const2(gk),
    ]
    args = [sink, h, w_in, gq, gk]
    y_spec = pl.BlockSpec((None, tm, qw), lambda i: (i, 0, 0))
    y_shape = jax.ShapeDtypeStruct((nt, tm, qw), BF16)
    scratch = [pltpu.VMEM((D_MODEL, qw + 2 * kw), BF16)]
    aliases = {}
    if cache is None:
        past = 0
        kv_spec = _state_spec(layer, group, (seq, kw))
        kv_shape = jax.ShapeDtypeStruct((nt * group, DEPTH, seq, kw), F32)
        out_specs = [y_spec, kv_spec, kv_spec]
        out_shape = [y_shape, kv_shape, kv_shape]
        aliases = {len(args) + n: 1 + n for n in range(len(states))}
        in_specs += [pl.BlockSpec(memory_space=pl.ANY)] * len(states)
        args += list(states)
    else:
        cos, sin = rope
        ck, cv = cache
        past = ck.shape[3]
        cspec = pl.BlockSpec((None, None, kw, past), lambda i: (i, layer, 0, 0))
        in_specs += [const2(cos), const2(sin), cspec, cspec]
        args += [cos, sin, ck, cv]
        out_specs = [y_spec]
        out_shape = [y_shape]
        scratch += [pltpu.VMEM((seq + 2 * BLOCK, kw), BF16), pltpu.VMEM((seq + 2 * BLOCK, 2 * kw), BF16)]
    return pl.pallas_call(
        functools.partial(_win_kernel, group=group, seq=seq, past=past, layer=layer),
        grid=(nt,),
        in_specs=in_specs,
        out_specs=out_specs,
        out_shape=out_shape,
        scratch_shapes=scratch,
        input_output_aliases=aliases,
        compiler_params=_cparams(("arbitrary",)),
        name=name,
    )(*args)


def _ret_kernel(*refs, group, seq, has_state, layer):
    if has_state:
        h_ref, wa_ref, wb_ref, df_ref, db_ref, s0f_ref, s0b_ref, y_ref, w_s, dec_s = refs
    else:
        h_ref, wa_ref, wb_ref, df_ref, db_ref = refs[:5]
        y_ref, sf_ref, sb_ref, w_s, dec_s = refs[-5:]
    qk_w = RET_HEADS * RET_DK
    v_w = RET_HEADS * RET_DV
    _cast_weights_once((wa_ref, wb_ref), w_s)
    proj = jnp.dot(h_ref[...], w_s[...], preferred_element_type=F32)
    q = proj[:, :qk_w]
    k = proj[:, qk_w:2 * qk_w] * (RET_DK ** -0.5)
    v = proj[:, 2 * qk_w:2 * qk_w + v_w]
    rg = proj[:, 2 * qk_w + v_w:]
    lgf_all = _log_gamma(df_ref[...])
    lgb_all = _log_gamma(db_ref[...])
    ch = min(seq, ATT_Q_TILE)
    n_ch = seq // ch

    @pl.when(pl.program_id(0) == 0)
    def _():
        rel = (_row((ch, ch)) - _lane((ch, ch))).astype(F32)
        for hd in range(RET_HEADS):
            lgf = lgf_all[hd:hd + 1, 0:1]
            lgb = lgb_all[hd:hd + 1, 0:1]
            dec_s[hd] = (jnp.where(rel >= 0, jnp.exp(jnp.maximum(rel, 0.0) * lgf), 0.0)
                         + jnp.where(rel <= 0, jnp.exp(jnp.maximum(-rel, 0.0) * lgb), 0.0))

    lane = _lane((ch, LANE))
    i_loc = _row((ch, LANE)).astype(F32)
    for hd in range(RET_HEADS):
        slab = slice((hd // 2) * LANE, (hd // 2 + 1) * LANE)
        vsl = slice(hd * RET_DV, (hd + 1) * RET_DV)
        half = hd % 2
        keep = lane >= RET_DK if half == 1 else lane < RET_DK
        lgf = lgf_all[hd:hd + 1, 0:1]
        lgb = lgb_all[hd:hd + 1, 0:1]
        q_in_f = jnp.exp((i_loc + 1.0) * lgf)
        q_in_b = jnp.exp((ch - i_loc) * lgb)
        k_out_f = jnp.exp((ch - 1.0 - i_loc) * lgf)
        k_out_b = jnp.exp(i_loc * lgb)
        carry_f = jnp.exp(ch * lgf)
        carry_b = jnp.exp(ch * lgb)
        dec = dec_s[hd]
        for g in range(group):
            rows = [slice(g * seq + c * ch, g * seq + (c + 1) * ch) for c in range(n_ch)]
            qm = [jnp.where(keep, q[r, slab], 0.0) for r in rows]
            kc = [k[r, slab] for r in rows]
            vb = [v[r, vsl].astype(BF16) for r in rows]
            o_state = [None] * n_ch
            for q_in, k_out, carry, s_ref, order in (
                    (q_in_f, k_out_f, carry_f, s0f_ref if has_state else sf_ref, range(n_ch)),
                    (q_in_b, k_out_b, carry_b, s0b_ref if has_state else sb_ref, range(n_ch - 1, -1, -1))):
                state = s_ref[hd // 2] if has_state else None
                for pos, c in enumerate(order):
                    if state is not None:
                        term = _bdot(qm[c] * q_in, state)
                        o_state[c] = term if o_state[c] is None else o_state[c] + term
                    if pos + 1 < n_ch or not has_state:
                        update = _bdot((kc[c] * k_out).T, vb[c])
                        state = update if state is None else state * carry + update
                if not has_state:
                    _store_state(s_ref, g, layer, state[half * RET_DK:(half + 1) * RET_DK], hd)
            for c, r in enumerate(rows):
                s = _bdot_nt(qm[c], kc[c])
                o = jnp.dot((s * dec).astype(BF16), vb[c], preferred_element_type=F32)
                if o_state[c] is not None:
                    o = o + o_state[c]
                y = _rms(o, None) * _silu(rg[r, vsl])
                y_ref[r, vsl] = y.astype(BF16)


def _ret_call(h, w_in, layer, dec_f, dec_b, group, seq, state=None, states=(), name="ret"):
    nt, tm = h.shape[0], h.shape[1]
    v_w = RET_HEADS * RET_DV
    ret_cols = COL_GATE - COL_RET
    const2 = lambda a: pl.BlockSpec(a.shape, lambda i: (0, 0))
    in_specs = [
        pl.BlockSpec((None, tm, D_MODEL), lambda i: (i, 0, 0)),
        _w_in_spec(layer, COL_RET, ret_cols // 2),
        _w_in_spec(layer, COL_RET + ret_cols // 2, ret_cols // 2),
        const2(dec_f), const2(dec_b),
    ]
    args = [h, w_in, w_in, dec_f, dec_b]
    y_spec = pl.BlockSpec((None, tm, v_w), lambda i: (i, 0, 0))
    y_shape = jax.ShapeDtypeStruct((nt, tm, v_w), BF16)
    aliases = {}
    if state is None:
        s_spec = _state_spec(layer, group, (RET_HEADS, RET_DK, RET_DV))
        s_shape = jax.ShapeDtypeStruct((nt * group, DEPTH, RET_HEADS, RET_DK, RET_DV), F32)
        out_specs = [y_spec, s_spec, s_spec]
        out_shape = [y_shape, s_shape, s_shape]
        aliases = {len(args) + n: 1 + n for n in range(len(states))}
        in_specs += [pl.BlockSpec(memory_space=pl.ANY)] * len(states)
        args += list(states)
    else:
        s0f, s0b = state
        sspec = pl.BlockSpec((None, None, 2, LANE, RET_DV), lambda i: (i, layer, 0, 0, 0))
        in_specs += [sspec, sspec]
        args += [s0f, s0b]
        out_specs = [y_spec]
        out_shape = [y_shape]
    return pl.pallas_call(
        functools.partial(_ret_kernel, group=group, seq=seq, has_state=state is not None, layer=layer),
        grid=(nt,),
        in_specs=in_specs,
        out_specs=out_specs,
        out_shape=out_shape,
        input_output_aliases=aliases,
        scratch_shapes=[pltpu.VMEM((D_MODEL, ret_cols), BF16),
                        pltpu.VMEM((RET_HEADS, min(seq, ATT_Q_TILE), min(seq, ATT_Q_TILE)), F32)],
        compiler_params=_cparams(("arbitrary",)),
        name=name,
    )(*args)


def _merge_ffn_kernel(x_ref, h_ref, mod_ref, gain_ref, y0_ref, y1_ref, y2_ref, y3_ref,
                      wg_ref, wb_ref, wbw_ref, wo_ref, wi_ref, wo2_ref, o_ref, merged_s, act_s):
    m = mod_ref[...]
    for rows in _row_groups(x_ref.shape[0]):
        h = h_ref[rows, :]
        ys = [r[rows, :] for r in (y0_ref, y1_ref, y2_ref, y3_ref)]
        for c in range(D_MODEL // MERGE_TILE):
            cols = slice(c * MERGE_TILE, (c + 1) * MERGE_TILE)
            acc = None
            for b in range(N_BRANCH):
                gate = jax.nn.sigmoid(jnp.dot(
                    h, wg_ref[:, b * D_MODEL + c * MERGE_TILE:b * D_MODEL + (c + 1) * MERGE_TILE],
                    preferred_element_type=F32))
                w_b = wbw_ref[:, cols] if b == WIN_BRANCH else wb_ref[b, :, cols]
                part = gate * jnp.dot(ys[b], w_b, preferred_element_type=F32)
                acc = part if acc is None else acc + part
            merged_s[rows, cols] = acc.astype(BF16)
        mixed = jnp.dot(merged_s[rows, :], wo_ref[...], preferred_element_type=F32)
        x = x_ref[rows, :] + m[5:6] * mixed
        o_ref[rows, :] = _swiglu_rows(x, m, 6, gain_ref[...], wi_ref, wo2_ref, act_s, rows)


def _merge_ffn_call(x, h, mod, mod_row, gain, ys, w_gate, w_branch, w_branch_win, w_out, w_ff_in, w_ff_out,
                    layer, name):
    nt, tm, _ = x.shape
    tok = pl.BlockSpec((None, tm, D_MODEL), lambda i: (i, 0, 0))
    ysp = pl.BlockSpec((None, tm, BRANCH_W), lambda i: (i, 0, 0))
    return pl.pallas_call(
        _merge_ffn_kernel,
        grid=(nt,),
        in_specs=[
            tok, tok,
            pl.BlockSpec((None, N_MOD, D_MODEL), lambda i: (mod_row(i), 0, 0)),
            pl.BlockSpec((1, D_MODEL), lambda i: (0, 0)),
            ysp, ysp, ysp, ysp,
            _resident((None, D_MODEL, N_BRANCH * D_MODEL), lambda i: (layer, 0, 0)),
            _resident((None, N_BRANCH, BRANCH_W, D_MODEL), lambda i: (layer, 0, 0, 0)),
            _resident((None, BRANCH_W, D_MODEL), lambda i: (layer, 0, 0)),
            _resident((None, D_MODEL, D_MODEL), lambda i: (layer, 0, 0)),
            _resident((None, D_MODEL, 2 * D_FF), lambda i: (layer, 0, 0)),
            _resident((None, D_FF, D_MODEL), lambda i: (layer, 0, 0)),
        ],
        out_specs=tok,
        out_shape=jax.ShapeDtypeStruct(x.shape, F32),
        scratch_shapes=[pltpu.VMEM((tm, D_MODEL), BF16), pltpu.VMEM((tm, D_FF), BF16)],
        compiler_params=_cparams(("parallel",)),
        name=name,
    )(x, h, mod, gain, *ys, w_gate, w_branch, w_branch_win, w_out, w_ff_in, w_ff_out)


def _dft_tables(seq):
    n_fft = 2 * seq
    idx = np.arange(seq, dtype=np.int64)
    ang = 2.0 * np.pi * ((idx[:, None] * idx[None, :]) % n_fft).astype(np.float64) / n_fft
    out = []
    for m in (np.cos(ang), np.sin(ang)):
        hi = jnp.asarray(m, F32).astype(BF16)
        lo = (jnp.asarray(m, F32) - hi.astype(F32)).astype(BF16)
        out += [hi, lo]
    return out


def _hyena_tables(seq):
    t = np.arange(seq, dtype=np.float32) / np.float32(seq)
    f = np.arange(1, HY_BANDS + 1, dtype=np.float32)
    ang = np.float32(2.0 * math.pi) * t[:, None] * f[None, :]
    feats = np.zeros((seq, LANE), np.float32)
    feats[:, 0] = t
    feats[:, 1:1 + HY_BANDS] = np.sin(ang)
    feats[:, 1 + HY_BANDS:HY_EMB] = np.cos(ang)
    min_decay = math.log(HY_TARGET) / HY_DECAY_LONG_PCT
    max_decay = math.log(HY_TARGET) / HY_DECAY_SHORT_PCT
    deltas = np.linspace(min_decay, max_decay, HY_W, dtype=np.float32)
    window = np.exp(-t[:, None] * np.abs(deltas)[None, :]).astype(np.float32)
    return jnp.asarray(feats), jnp.asarray(window)


def _rope_tables(seq):
    pos = np.arange(seq)
    row = (pos // GRID_W).astype(np.float32)
    col = (pos % GRID_W).astype(np.float32)
    inv_freq = (np.float32(ROPE_BASE) ** (-np.arange(ROPE_PAIRS, dtype=np.float32) / np.float32(ROPE_PAIRS)))
    lane = np.arange(LANE)
    in_head = lane % HEAD_DIM
    use_col = in_head >= HEAD_DIM // 2
    pair = in_head % ROPE_PAIRS
    second = (in_head % (2 * ROPE_PAIRS)) >= ROPE_PAIRS
    p = np.where(use_col[None, :], col[:, None], row[:, None]).astype(np.float32)
    ang = (p * inv_freq[pair][None, :]).astype(np.float32)
    cos = np.cos(ang).astype(np.float32)
    sin = np.sin(ang).astype(np.float32)
    sin = np.where(second[None, :], sin, -sin)
    return jnp.asarray(cos), jnp.asarray(sin)


def kernel(x_prompt, x_sample, c, cache_diff_k, cache_diff_v, cache_win_k, cache_win_v, state_ret_f, state_ret_b, c_ctx, norm_ffa, norm_mix, norm_ffb, w_ada, b_ada, w_ffa_in, w_ffa_out, w_ffb_in, w_ffb_out, w_in, hy_conv_w, hy_conv_b, hy_f_w1, hy_f_b1, hy_f_w2, hy_f_b2, hy_f_w3, hy_skip, diff_q_norm, diff_k_norm, diff_lambda, diff_subln, win_q_norm, win_k_norm, win_sink, ret_decay_f, ret_decay_b, w_branch, w_out):
    batch, seq, _ = x_prompt.shape
    dec_batch, dec_seq, _ = x_sample.shape
    past = cache_diff_k.shape[2]
    ctx_group = TOKEN_TILE // seq
    assert TOKEN_TILE % seq == 0 and batch % ctx_group == 0 and dec_seq == TOKEN_TILE
    assert 1 + dec_batch <= ADA_ROWS and w_in.shape == (DEPTH, D_MODEL, IN_COLS)

    cond = jnp.zeros((ADA_ROWS, D_MODEL), F32).at[0].set(c_ctx).at[1:1 + dec_batch].set(c)
    mod = _ada_call(cond, w_ada, b_ada).reshape(DEPTH, ADA_ROWS, N_MOD, D_MODEL)

    bf = lambda a: a.astype(BF16)
    w_ffa_in_b, w_ffa_out_b, w_ffb_in_b, w_ffb_out_b = bf(w_ffa_in), bf(w_ffa_out), bf(w_ffb_in), bf(w_ffb_out)
    w_gate = bf(w_in[:, :, COL_GATE:])
    w_branch_b, w_out_b = bf(w_branch), bf(w_out)
    w_branch_win = jnp.swapaxes(
        w_branch_b[:, WIN_BRANCH].reshape(DEPTH, WIN_KV_HEADS, WIN_GROUP, HEAD_DIM, D_MODEL), 1, 2
    ).reshape(DEPTH, BRANCH_W, D_MODEL)

    rope = _rope_tables(dec_seq)
    tables = {s: (_hyena_tables(s), _dft_tables(s)) for s in (seq, dec_seq)}

    ck_d = jnp.transpose(cache_diff_k, (0, 1, 3, 4, 5, 2)).reshape(dec_batch, DEPTH, -1, past)
    cv_d = cache_diff_v.reshape(dec_batch, DEPTH, past * DIFF_HEADS, 2 * HEAD_DIM)
    ck_w = jnp.transpose(cache_win_k, (0, 1, 3, 4, 2)).reshape(dec_batch, DEPTH, -1, past)
    cv_w = jnp.transpose(cache_win_v, (0, 1, 3, 4, 2)).reshape(dec_batch, DEPTH, -1, past)
    s0_f = state_ret_f.reshape(dec_batch, DEPTH, 2, LANE, RET_DV)
    s0_b = state_ret_b.reshape(dec_batch, DEPTH, 2, LANE, RET_DV)

    pad_rows = lambda a, n: jnp.pad(a, ((0, n - a.shape[0]), (0, 0)))
    tile_lanes = lambda a, n: jnp.tile(a.reshape(1, -1), (1, n))

    tiled = lambda a, tile: a.reshape(-1, tile, a.shape[-1])
    y_ctx = x_prompt
    y_lat = x_sample
    ctx_row = lambda tile: (lambda i: 0)
    lat_row = lambda tile: (lambda i: i // (dec_seq // tile) + 1)
    st_diff, st_win, st_ret = (), (), ()

    for l in range(DEPTH):
        lam_init = 0.8 - 0.6 * math.exp(-0.3 * l)
        gain = lambda a: a[l].reshape(1, -1)
        gq_d, gk_d = tile_lanes(diff_q_norm[l], 2 * DIFF_HEADS), tile_lanes(diff_k_norm[l], 2 * DIFF_HEADS)
        gs_d = tile_lanes(diff_subln[l], DIFF_HEADS)
        gq_w, gk_w = tile_lanes(win_q_norm[l], WIN_Q_HEADS), tile_lanes(win_k_norm[l], WIN_KV_HEADS)
        dec_f = jnp.broadcast_to(ret_decay_f[l][:, None], (RET_HEADS, LANE))
        dec_b = jnp.broadcast_to(ret_decay_b[l][:, None], (RET_HEADS, LANE))
        w1 = jnp.pad(hy_f_w1[l], ((0, LANE - HY_EMB), (0, LANE - HY_FH)))
        b1 = jnp.pad(hy_f_b1[l].reshape(1, -1), ((0, 0), (0, LANE - HY_FH)))
        w2 = jnp.pad(hy_f_w2[l], ((0, LANE - HY_FH), (0, LANE - HY_FH)))
        b2 = jnp.pad(hy_f_b2[l].reshape(1, -1), ((0, 0), (0, LANE - HY_FH)))
        w3 = pad_rows(hy_f_w3[l], LANE)
        cw, cb, skip = hy_conv_w[l], hy_conv_b[l].reshape(1, -1), hy_skip[l].reshape(1, -1)

        for is_lat in (False, True):
            x = y_lat if is_lat else y_ctx
            s_len = dec_seq if is_lat else seq
            group = 1 if is_lat else ctx_group
            mod_row = lat_row if is_lat else ctx_row
            tag = f"{'lat' if is_lat else 'ctx'}{l}"
            (feats, window), (c_hi, c_lo, s_hi, s_lo) = tables[s_len]

            x, h = _ffn_a_call(tiled(x, FFN_A_TILE), mod[l], mod_row(FFN_A_TILE), gain(norm_ffa), gain(norm_mix),
                               w_ffa_in_b, w_ffa_out_b, l, f"ffn_a_{tag}")
            h = tiled(h, TOKEN_TILE)

            kr, ki, kn = _hy_filter_call(s_len, feats, w1, b1, w2, b2, w3, window, c_hi, c_lo, s_hi, s_lo)
            y_hy = _hyena_call(h, w_in, l, cw, cb, kr, ki, kn, skip, c_hi, s_hi, group, s_len, f"hyena_{tag}")
            if is_lat:
                (y_diff,) = _diff_call(h, w_in, l, gq_d, gk_d, diff_lambda[l], gs_d, lam_init,
                                       group, s_len, rope=rope, cache=(ck_d, cv_d), name=f"diff_{tag}")
                (y_win,) = _win_call(h, w_in, l, win_sink, gq_w, gk_w, group, s_len,
                                     rope=rope, cache=(ck_w, cv_w), name=f"win_{tag}")
                (y_ret,) = _ret_call(h, w_in, l, dec_f, dec_b, group, s_len, state=(s0_f, s0_b),
                                     name=f"ret_{tag}")
            else:
                y_diff, *st_diff = _diff_call(h, w_in, l, gq_d, gk_d, diff_lambda[l], gs_d, lam_init,
                                              group, s_len, states=st_diff, name=f"diff_{tag}")
                y_win, *st_win = _win_call(h, w_in, l, win_sink, gq_w, gk_w, group, s_len,
                                           states=st_win, name=f"win_{tag}")
                y_ret, *st_ret = _ret_call(h, w_in, l, dec_f, dec_b, group, s_len, states=st_ret,
                                           name=f"ret_{tag}")

            x = _merge_ffn_call(tiled(x, MERGE_FFN_TILE), tiled(h, MERGE_FFN_TILE), mod[l], mod_row(MERGE_FFN_TILE),
                                gain(norm_ffb),
                                tuple(tiled(y, MERGE_FFN_TILE) for y in (y_hy, y_diff, y_win, y_ret)),
                                w_gate, w_branch_b, w_branch_win, w_out_b, w_ffb_in_b, w_ffb_out_b, l,
                                f"merge_ffn_b_{tag}")
            if is_lat:
                y_lat = x
            else:
                y_ctx = x

    return (y_ctx.reshape(batch, seq, D_MODEL), y_lat.reshape(dec_batch, dec_seq, D_MODEL),
            st_diff[0].reshape(batch, DEPTH, seq, DIFF_HEADS, 2, HEAD_DIM),
            st_diff[1].reshape(batch, DEPTH, seq, DIFF_HEADS, 2 * HEAD_DIM),
            st_win[0].reshape(batch, DEPTH, seq, WIN_KV_HEADS, HEAD_DIM),
            st_win[1].reshape(batch, DEPTH, seq, WIN_KV_HEADS, HEAD_DIM),
            st_ret[0], st_ret[1])
```
